```python
import jax, jax.numpy as jnp
from jax import lax
import numpy as np

D_MODEL = 2048
BATCH = 4
SEQ = 4096
DEPTH = 4

GRID_W = 64
CTX_LEN = 256

N_MIXERS = 3
N_GLA_LAYERS = (DEPTH + 2) // 3
N_SWA_LAYERS = (DEPTH + 1) // 3
N_CONV_LAYERS = DEPTH // 3

GLA_HEADS = 4
GLA_KEY_WIDTH = D_MODEL // 2
GLA_VALUE_WIDTH = D_MODEL
GLA_HEAD_V = GLA_VALUE_WIDTH // GLA_HEADS
GLA_GATE_RANK = 16
GLA_GATE_NORM = 16.0
GLA_CHUNK = 64

SWA_HEAD_DIM = 64
SWA_Q_HEADS = D_MODEL // SWA_HEAD_DIM
SWA_KV_HEADS = SWA_Q_HEADS // 8
SWA_GROUP = SWA_Q_HEADS // SWA_KV_HEADS
SWA_Q_WIDTH = SWA_Q_HEADS * SWA_HEAD_DIM
SWA_KV_WIDTH = SWA_KV_HEADS * SWA_HEAD_DIM
WINDOW = 128
WIN_BLOCK = 128
BAND = WIN_BLOCK + 2 * WINDOW
ROPE_BASE = 10000.0
ROPE_AXIS_FREQS = SWA_HEAD_DIM // 4

CONV_WIDTH = 3

N_EXPERTS = 32
TOP_K = 4
EXPERT_FF = 3 * D_MODEL // 8
SWIGLU_LIMIT = 7.0
SWIGLU_ALPHA = 1.702
MOE_ROW_BLOCK = 256

N_MOD = 6
LN_EPS = 1e-5
DEEPNORM_ALPHA = (2 * DEPTH) ** 0.25
DEEPNORM_BETA = (8 * DEPTH) ** -0.25
NEG_INF = -1e30

kernel_name = 'hybrid_gla_swa_shortconv_moe_dit'

F32 = jnp.float32


def _layer_norm(x, g, b):
    xf = x.astype(F32)
    xc = xf - jnp.mean(xf, -1, keepdims=True)
    var = jnp.mean(xc * xc, -1, keepdims=True)
    return (xc * lax.rsqrt(var + LN_EPS) * g.astype(F32) + b.astype(F32)).astype(x.dtype)


def _split_apply(t, n_ctx, f_ctx, f_lat):
    return jnp.concatenate([f_ctx(t[:, :n_ctx]), f_lat(t[:, n_ctx:])], axis=1)


def _reverse_segments(t, n_ctx):
    return _split_apply(t, n_ctx, lambda s: jnp.flip(s, 1), lambda s: jnp.flip(s, 1))


def _axial_rope_tables(rows):
    row = jnp.repeat(jnp.arange(rows, dtype=jnp.int32), GRID_W).astype(F32)
    col = jnp.tile(jnp.arange(GRID_W, dtype=jnp.int32), rows).astype(F32)
    inv_freq = ROPE_BASE ** (-jnp.arange(ROPE_AXIS_FREQS, dtype=F32) / ROPE_AXIS_FREQS)
    ang_r = row[:, None] * inv_freq
    ang_c = col[:, None] * inv_freq
    return (jnp.cos(ang_r), jnp.sin(ang_r), jnp.cos(ang_c), jnp.sin(ang_c))


def _rotate(xp, cos, sin):
    x1, x2 = jnp.split(xp, 2, axis=-1)
    cos = cos[None, :, None, :].astype(xp.dtype)
    sin = sin[None, :, None, :].astype(xp.dtype)
    return jnp.concatenate([x1 * cos - x2 * sin, x2 * cos + x1 * sin], axis=-1)


def _axial_rope(x, tables):
    cos_r, sin_r, cos_c, sin_c = tables
    x_row, x_col = jnp.split(x, 2, axis=-1)
    return jnp.concatenate([_rotate(x_row, cos_r, sin_r), _rotate(x_col, cos_c, sin_c)], axis=-1)


def _gla_chunked(q, k, v, log_a):
    bsz, length, heads, dk = q.shape
    n_chunks = length // GLA_CHUNK

    def chunks(t):
        return t.astype(F32).reshape(bsz, n_chunks, GLA_CHUNK, heads, t.shape[-1]).transpose(1, 0, 3, 2, 4)

    qc = chunks(q) * (dk ** -0.5)
    kc, vc, ac = chunks(k), chunks(v), chunks(log_a)
    cum = jnp.cumsum(ac, axis=3)
    total = cum[:, :, :, -1:, :]
    q_dec = qc * jnp.exp(cum)
    k_intra = kc * jnp.exp(-cum)
    k_state = kc * jnp.exp(total - cum)
    lower = jnp.tril(jnp.ones((GLA_CHUNK, GLA_CHUNK), dtype=bool))
    scores = jnp.where(lower, jnp.einsum('nbhcd,nbhsd->nbhcs', q_dec, k_intra), 0.0)
    o_intra = jnp.einsum('nbhcs,nbhse->nbhce', scores, vc)

    def step(state, xs):
        qd, ks, vv, decay = xs
        out = jnp.einsum('bhcd,bhde->bhce', qd, state)
        state = state * decay[:, :, 0, :, None] + jnp.einsum('bhcd,bhce->bhde', ks, vv)
        return state, out

    init = jnp.zeros((bsz, heads, dk, vc.shape[-1]), F32)
    _, o_inter = lax.scan(step, init, (q_dec, k_state, vc, jnp.exp(total)))
    o = o_intra + o_inter
    return o.transpose(1, 0, 3, 2, 4).reshape(bsz, length, heads, -1)


def _gla_mixer(u, n_ctx, w_in, gate_w1, gate_w2, gate_b, norm_g, w_out):
    bsz, length, _ = u.shape
    q, k, v, g = jnp.split(u @ w_in, [GLA_KEY_WIDTH, 2 * GLA_KEY_WIDTH, 2 * GLA_KEY_WIDTH + GLA_VALUE_WIDTH], axis=-1)
    heads = lambda t: t.reshape(bsz, length, GLA_HEADS, -1)
    q, k, v = heads(q), heads(k), heads(v)

    def log_decay(d):
        z = (u @ gate_w1[d]) @ gate_w2[d] + gate_b[d]
        return heads(jax.nn.log_sigmoid(z.astype(F32)) / GLA_GATE_NORM)

    rev = lambda t: _reverse_segments(t, n_ctx)
    o_fwd = _gla_chunked(q, k, v, log_decay(0))
    o_bwd = rev(_gla_chunked(rev(q), rev(k), rev(v), rev(log_decay(1))))
    o = o_fwd + o_bwd
    o = o * lax.rsqrt(jnp.mean(o * o, -1, keepdims=True) + LN_EPS) * norm_g.astype(F32)
    o = o * jax.nn.silu(heads(g).astype(F32))
    return o.reshape(bsz, length, GLA_VALUE_WIDTH).astype(u.dtype) @ w_out


def _sink_attention(q, k, v, sink, mask):
    s = jnp.einsum('bqhgd,bkhd->bhgqk', q, k).astype(F32) * (SWA_HEAD_DIM ** -0.5)
    if mask is not None:
        s = jnp.where(mask, s, NEG_INF)
    sink_col = jnp.broadcast_to(sink.astype(F32)[None, :, :, None, None], s.shape[:-1] + (1,))
    p = jax.nn.softmax(jnp.concatenate([s, sink_col], axis=-1), axis=-1)[..., :-1]
    return jnp.einsum('bhgqk,bkhd->bqhgd', p.astype(v.dtype), v)


def _window_mixer(u, n_ctx, w_qkv, b_qkv, sink, w_out, b_out, rope):
    bsz, length, _ = u.shape
    q, k, v = jnp.split(u @ w_qkv + b_qkv, [SWA_Q_WIDTH, SWA_Q_WIDTH + SWA_KV_WIDTH], axis=-1)
    q = q.reshape(bsz, length, SWA_Q_HEADS, SWA_HEAD_DIM)
    k = k.reshape(bsz, length, SWA_KV_HEADS, SWA_HEAD_DIM)
    v = v.reshape(bsz, length, SWA_KV_HEADS, SWA_HEAD_DIM)
    sink = sink.reshape(SWA_KV_HEADS, SWA_GROUP)
    grp = lambda t: t.reshape(t.shape[0], t.shape[1], SWA_KV_HEADS, SWA_GROUP, SWA_HEAD_DIM)

    qc, kc, vc = grp(q[:, :n_ctx]), k[:, :n_ctx], v[:, :n_ctx]
    o_ctx = _sink_attention(qc, kc, vc, sink, None).reshape(bsz, n_ctx, SWA_Q_WIDTH)

    ql = grp(_axial_rope(q[:, n_ctx:], rope))
    kl = _axial_rope(k[:, n_ctx:], rope)
    vl = v[:, n_ctx:]
    seq = ql.shape[1]
    n_blocks = seq // WIN_BLOCK
    k_pad = jnp.pad(kl, ((0, 0), (WINDOW, WINDOW), (0, 0), (0, 0)))
    v_pad = jnp.pad(vl, ((0, 0), (WINDOW, WINDOW), (0, 0), (0, 0)))
    q_blocks = ql.reshape(bsz, n_blocks, WIN_BLOCK, SWA_KV_HEADS, SWA_GROUP, SWA_HEAD_DIM).transpose(1, 0, 2, 3, 4, 5)
    offs = jnp.arange(BAND)
    band = jnp.abs(offs[None, :] - WINDOW - jnp.arange(WIN_BLOCK)[:, None]) <= WINDOW

    def block(args):
        n, qb = args
        start = n * WIN_BLOCK
        kb = lax.dynamic_slice_in_dim(k_pad, start, BAND, axis=1)
        vb = lax.dynamic_slice_in_dim(v_pad, start, BAND, axis=1)
        key_pos = start - WINDOW + offs
        valid = band & ((key_pos >= 0) & (key_pos < seq))[None, :]
        mask = jnp.concatenate([jnp.ones((WIN_BLOCK, n_ctx), dtype=bool), valid], axis=1)
        return _sink_attention(qb, jnp.concatenate([kc, kb], 1), jnp.concatenate([vc, vb], 1), sink, mask)

    o_lat = lax.map(block, (jnp.arange(n_blocks), q_blocks))
    o_lat = o_lat.transpose(1, 0, 2, 3, 4, 5).reshape(bsz, seq, SWA_Q_WIDTH)
    return jnp.concatenate([o_ctx, o_lat], axis=1) @ w_out + b_out


def _depthwise_conv(z, w):
    return lax.conv_general_dilated(z, w.astype(z.dtype)[:, None, :], window_strides=(1,),
                                    padding=((CONV_WIDTH // 2, CONV_WIDTH // 2),),
                                    dimension_numbers=('NWC', 'WIO', 'NWC'),
                                    feature_group_count=z.shape[-1])


def _conv_mixer(u, n_ctx, w_in, w_conv, w_out):
    gate_in, gate_out, val = jnp.split(u @ w_in, 3, axis=-1)
    z = _split_apply(gate_in * val, n_ctx, lambda s: _depthwise_conv(s, w_conv), lambda s: _depthwise_conv(s, w_conv))
    return (gate_out * z) @ w_out


def _moe(u, w_router, b_router, w_gate_up, b_gate_up, w_down, b_down):
    bsz, length, d = u.shape
    xt = u.reshape(-1, d)
    n_tok = xt.shape[0]
    logits = (xt @ w_router + b_router).astype(F32)
    top_val, top_exp = lax.top_k(logits, TOP_K)
    gates = jax.nn.softmax(top_val, axis=-1)
    n_assign = n_tok * TOP_K
    flat_e = top_exp.reshape(-1)
    flat_t = jnp.repeat(jnp.arange(n_tok, dtype=jnp.int32), TOP_K)
    order = jnp.argsort(flat_e)
    e_s, t_s, g_s = flat_e[order], flat_t[order], gates.reshape(-1)[order]
    counts = jnp.bincount(flat_e, length=N_EXPERTS)
    padded = (counts + MOE_ROW_BLOCK - 1) // MOE_ROW_BLOCK * MOE_ROW_BLOCK
    start = jnp.cumsum(counts) - counts
    pad_end = jnp.cumsum(padded)
    pad_start = pad_end - padded
    dest = pad_start[e_s] + jnp.arange(n_assign) - start[e_s]
    n_blocks = -(-n_assign // MOE_ROW_BLOCK) + N_EXPERTS
    n_rows = n_blocks * MOE_ROW_BLOCK
    row_tok = jnp.zeros((n_rows,), jnp.int32).at[dest].set(t_s)
    row_gate = jnp.zeros((n_rows,), xt.dtype).at[dest].set(g_s.astype(xt.dtype))
    blk_exp = jnp.minimum(jnp.searchsorted(pad_end, jnp.arange(n_blocks) * MOE_ROW_BLOCK, side='right'), N_EXPERTS - 1)

    def expert_block(args):
        tok, gw, e = args
        gu = xt[tok] @ w_gate_up[e] + b_gate_up[e]
        glu, lin = jnp.split(gu, 2, axis=-1)
        glu = jnp.minimum(glu, SWIGLU_LIMIT)
        lin = jnp.clip(lin, -SWIGLU_LIMIT, SWIGLU_LIMIT)
        act = glu * jax.nn.sigmoid(SWIGLU_ALPHA * glu) * (lin + 1.0)
        return (act @ w_down[e] + b_down[e]) * gw[:, None]

    y_rows = lax.map(expert_block, (row_tok.reshape(n_blocks, MOE_ROW_BLOCK),
                                    row_gate.reshape(n_blocks, MOE_ROW_BLOCK), blk_exp))
    y = jax.ops.segment_sum(y_rows.reshape(n_rows, d), row_tok, num_segments=n_tok)
    return y.reshape(bsz, length, d)


def setup_inputs(seed: int = 0) -> dict:
    key = jax.random.key(seed)
    ks = list(jax.random.split(key, 32))
    D = D_MODEL
    F = EXPERT_FF

    def nrm(i, shape, scale):
        return jax.random.normal(ks[i], shape, F32) * scale

    return {
        'x': nrm(0, (BATCH, SEQ, D), 1.0),
        'c': nrm(1, (BATCH, D), 1.0),
        'ctx': nrm(2, (BATCH, CTX_LEN, D), 1.0),
        'c_ctx': nrm(3, (D,), 1.0),
        'ada_w': nrm(4, (DEPTH, D, N_MOD * D), 0.5 * D ** -0.5),
        'ada_b': nrm(5, (DEPTH, N_MOD * D), 0.01),
        'ln_g': 1.0 + nrm(6, (DEPTH, 2, D), 0.02),
        'ln_b': nrm(7, (DEPTH, 2, D), 0.02),
        'gla_w_in': nrm(8, (N_GLA_LAYERS, D, 2 * GLA_KEY_WIDTH + 2 * GLA_VALUE_WIDTH), D ** -0.5),
        'gla_gate_w1': nrm(9, (N_GLA_LAYERS, 2, D, GLA_GATE_RANK), D ** -0.5),
        'gla_gate_w2': nrm(10, (N_GLA_LAYERS, 2, GLA_GATE_RANK, GLA_KEY_WIDTH), GLA_GATE_RANK ** -0.5),
        'gla_gate_b': nrm(11, (N_GLA_LAYERS, 2, GLA_KEY_WIDTH), 0.1),
        'gla_norm_g': 1.0 + nrm(12, (N_GLA_LAYERS, GLA_HEAD_V), 0.02),
        'gla_w_out': nrm(13, (N_GLA_LAYERS, GLA_VALUE_WIDTH, D), DEEPNORM_BETA * GLA_VALUE_WIDTH ** -0.5),
        'swa_w_qkv': nrm(14, (N_SWA_LAYERS, D, SWA_Q_WIDTH + 2 * SWA_KV_WIDTH), D ** -0.5),
        'swa_b_qkv': nrm(15, (N_SWA_LAYERS, SWA_Q_WIDTH + 2 * SWA_KV_WIDTH), 0.02),
        'swa_sink': nrm(16, (N_SWA_LAYERS, SWA_Q_HEADS), 0.5),
        'swa_w_out': nrm(17, (N_SWA_LAYERS, SWA_Q_WIDTH, D), DEEPNORM_BETA * SWA_Q_WIDTH ** -0.5),
        'swa_b_out': nrm(18, (N_SWA_LAYERS, D), 0.02),
        'conv_w_in': nrm(19, (N_CONV_LAYERS, D, 3 * D), D ** -0.5),
        'conv_w': nrm(20, (N_CONV_LAYERS, CONV_WIDTH, D), CONV_WIDTH ** -0.5),
        'conv_w_out': nrm(21, (N_CONV_LAYERS, D, D), DEEPNORM_BETA * D ** -0.5),
        'moe_router_w': nrm(22, (DEPTH, D, N_EXPERTS), D ** -0.5),
        'moe_router_b': nrm(23, (DEPTH, N_EXPERTS), 0.01),
        'moe_w_gate_up': nrm(24, (DEPTH, N_EXPERTS, D, 2 * F), D ** -0.5),
        'moe_b_gate_up': nrm(25, (DEPTH, N_EXPERTS, 2 * F), 0.02),
        'moe_w_down': nrm(26, (DEPTH, N_EXPERTS, F, D), DEEPNORM_BETA * F ** -0.5),
        'moe_b_down': nrm(27, (DEPTH, N_EXPERTS, D), 0.02),
    }


def reference(x, c, ctx, c_ctx, ada_w, ada_b, ln_g, ln_b,
              gla_w_in, gla_gate_w1, gla_gate_w2, gla_gate_b, gla_norm_g, gla_w_out,
              swa_w_qkv, swa_b_qkv, swa_sink, swa_w_out, swa_b_out,
              conv_w_in, conv_w, conv_w_out,
              moe_router_w, moe_router_b, moe_w_gate_up, moe_b_gate_up, moe_w_down, moe_b_down):
    seq = x.shape[1]
    rows = seq // GRID_W
    rope = _axial_rope_tables(rows)
    n_ctx = ctx.shape[1]
    h = jnp.concatenate([ctx, x], axis=1)
    cond_lat = jax.nn.silu(c)
    cond_ctx = jax.nn.silu(c_ctx)[None, :]

    for i in range(DEPTH):
        last = i == DEPTH - 1
        mc = [t[:, None] for t in jnp.split(cond_ctx @ ada_w[i] + ada_b[i], N_MOD, axis=-1)]
        ml = [t[:, None] for t in jnp.split(cond_lat @ ada_w[i] + ada_b[i], N_MOD, axis=-1)]

        u = _split_apply(h, n_ctx, lambda t: t * (1.0 + mc[1]) + mc[0], lambda t: t * (1.0 + ml[1]) + ml[0])
        kind, j = i % N_MIXERS, i // N_MIXERS
        if kind == 0:
            y = _gla_mixer(u, n_ctx, gla_w_in[j], gla_gate_w1[j], gla_gate_w2[j], gla_gate_b[j], gla_norm_g[j], gla_w_out[j])
        elif kind == 1:
            y = _window_mixer(u, n_ctx, swa_w_qkv[j], swa_b_qkv[j], swa_sink[j], swa_w_out[j], swa_b_out[j], rope)
        else:
            y = _conv_mixer(u, n_ctx, conv_w_in[j], conv_w[j], conv_w_out[j])
        if last:
            h, y, n_ctx = h[:, n_ctx:], y[:, n_ctx:], 0
        y = _split_apply(y, n_ctx, lambda t: t * mc[2], lambda t: t * ml[2])
        h = _layer_norm(DEEPNORM_ALPHA * h + y, ln_g[i, 0], ln_b[i, 0])

        u = _split_apply(h, n_ctx, lambda t: t * (1.0 + mc[4]) + mc[3], lambda t: t * (1.0 + ml[4]) + ml[3])
        y = _moe(u, moe_router_w[i], moe_router_b[i], moe_w_gate_up[i], moe_b_gate_up[i], moe_w_down[i], moe_b_down[i])
        y = _split_apply(y, n_ctx, lambda t: t * mc[5], lambda t: t * ml[5])
        h = _layer_norm(DEEPNORM_ALPHA * h + y, ln_g[i, 1], ln_b[i, 1])

    return h
```

```python
import functools

import jax
import jax.numpy as jnp
from jax import lax
from jax.experimental import pallas as pl
from jax.experimental.pallas import tpu as pltpu

F32 = jnp.float32
BF16 = jnp.bfloat16
HIGHEST = lax.Precision.HIGHEST

GRID_W = 64
GLA_HEADS = 4
GLA_GATE_RANK = 16
GLA_GATE_NORM = 16.0
GLA_CHUNK = 64
SWA_HEAD_DIM = 64
SWA_GROUP = 8
WINDOW = 128
ROPE_BASE = 10000.0
CONV_WIDTH = 3
N_EXPERTS = 32
TOP_K = 4
SWIGLU_LIMIT = 7.0
SWIGLU_ALPHA = 1.702
N_MOD = 6
LN_EPS = 1e-5
NEG_INF = -1e30

LANES = 128
ROW_TILE = 256
MOE_ROW_BLOCK = 256
VMEM_LIMIT = 48 * 1024 * 1024


def _cparams(*sem):
    return pltpu.CompilerParams(dimension_semantics=sem, vmem_limit_bytes=VMEM_LIMIT)


def _sigmoid(x):
    return 1.0 / (1.0 + jnp.exp(-x))


def _dot(a, b):
    return jnp.dot(a, b, preferred_element_type=F32)


def _dot_nt(a, b):
    return lax.dot_general(a, b, (((1,), (1,)), ((), ())), preferred_element_type=F32)


class _Rows:
    def __init__(self, batch, nb_ctx, nb_lat, tile=ROW_TILE):
        self.batch, self.nb_ctx, self.nb_lat, self.tile = batch, nb_ctx, nb_lat, tile
        self.nb_seq = nb_ctx + nb_lat
        self.n_blocks = batch * self.nb_seq
        self.n_rows = self.n_blocks * tile

    def batch_of(self, i):
        return i // self.nb_seq

    def seg_of(self, i):
        if self.nb_ctx == 0:
            return 1
        return jnp.where(i % self.nb_seq >= self.nb_ctx, 1, 0)

    def latent_block(self, i):
        return (i // self.nb_lat) * self.nb_seq + self.nb_ctx + i % self.nb_lat


def _ada_kernel(c_ref, w_ref, b_ref, o_ref):
    c = c_ref[...]
    s = c * _sigmoid(c)
    o_ref[...] = jnp.dot(s, w_ref[...], precision=HIGHEST, preferred_element_type=F32) + b_ref[...]


def _ada_mods(cond, ada_w, ada_b):
    depth, d, n = ada_w.shape
    tn = 1024 if n % 1024 == 0 else n
    return pl.pallas_call(
        _ada_kernel,
        out_shape=jax.ShapeDtypeStruct((depth, 8, n), F32),
        grid=(depth, n // tn),
        in_specs=[
            pl.BlockSpec((8, d), lambda l, j: (0, 0)),
            pl.BlockSpec((None, d, tn), lambda l, j: (l, 0, j)),
            pl.BlockSpec((None, 1, tn), lambda l, j: (l, 0, j)),
        ],
        out_specs=pl.BlockSpec((None, 8, tn), lambda l, j: (l, 0, j)),
        compiler_params=_cparams("parallel", "parallel"),
        name="ada_mods",
    )(cond, ada_w, ada_b.reshape(depth, 1, n))


def _modulate_kernel(h_ref, m_ref, u_ref, *, shift_idx, scale_idx):
    h = h_ref[...]
    u = h * (1.0 + m_ref[scale_idx:scale_idx + 1, :]) + m_ref[shift_idx:shift_idx + 1, :]
    u_ref[...] = u.astype(u_ref.dtype)


def _modulate(h, mods, rows, shift_idx, scale_idx):
    d = h.shape[1]
    t = rows.tile
    return pl.pallas_call(
        functools.partial(_modulate_kernel, shift_idx=shift_idx, scale_idx=scale_idx),
        out_shape=jax.ShapeDtypeStruct(h.shape, BF16),
        grid=(rows.n_blocks,),
        in_specs=[
            pl.BlockSpec((t, d), lambda i: (i, 0)),
            pl.BlockSpec((None, None, N_MOD, d), lambda i: (rows.batch_of(i), rows.seg_of(i), 0, 0)),
        ],
        out_specs=pl.BlockSpec((t, d), lambda i: (i, 0)),
        compiler_params=_cparams("parallel"),
        name="modulate",
    )(h, mods)


def _mm_kernel(x_ref, w_ref, o_ref):
    o_ref[...] = _dot(x_ref[...], w_ref[...]).astype(o_ref.dtype)


def _mm_bias_kernel(x_ref, w_ref, b_ref, o_ref):
    o_ref[...] = (_dot(x_ref[...], w_ref[...]) + b_ref[...]).astype(o_ref.dtype)


def _pick_tile(n, candidates):
    for c in candidates:
        if n % c == 0:
            return c
    return n


def _matmul(x, w, bias=None, out_dtype=BF16, name="proj"):
    m, k = x.shape
    n = w.shape[1]
    tm = _pick_tile(m, (1024, 512, 256))
    tn = _pick_tile(n, (1024, 768, 640, 512, 256, 128))
    in_specs = [
        pl.BlockSpec((tm, k), lambda i, j: (i, 0)),
        pl.BlockSpec((k, tn), lambda i, j: (0, j)),
    ]
    args = [x, w]
    body = _mm_kernel
    if bias is not None:
        in_specs.append(pl.BlockSpec((1, tn), lambda i, j: (0, j)))
        args.append(bias.reshape(1, n).astype(F32))
        body = _mm_bias_kernel
    return pl.pallas_call(
        body,
        out_shape=jax.ShapeDtypeStruct((m, n), out_dtype),
        grid=(m // tm, n // tn),
        in_specs=in_specs,
        out_specs=pl.BlockSpec((tm, tn), lambda i, j: (i, j)),
        compiler_params=_cparams("parallel", "parallel"),
        name=name,
    )(*args)


def _resid_ln_kernel(*refs, alpha, gate_idx, shift_idx, scale_idx, with_u, with_router):
    h_ref, y_ref, m_ref, mn_ref, g_ref, b_ref = refs[:6]
    pos = 6
    if with_router:
        wr_ref, br_ref = refs[pos:pos + 2]
        pos += 2
    hn_ref = refs[pos]
    pos += 1
    z = alpha * h_ref[...] + y_ref[...].astype(F32) * m_ref[gate_idx:gate_idx + 1, :]
    zc = z - jnp.mean(z, axis=-1, keepdims=True)
    var = jnp.mean(zc * zc, axis=-1, keepdims=True)
    hn = zc * lax.rsqrt(var + LN_EPS) * g_ref[...] + b_ref[...]
    hn_ref[...] = hn
    if with_u:
        u_ref = refs[pos]
        pos += 1
        u = hn * (1.0 + mn_ref[scale_idx:scale_idx + 1, :]) + mn_ref[shift_idx:shift_idx + 1, :]
        u_ref[...] = u.astype(u_ref.dtype)
        if with_router:
            logit_ref = refs[pos]
            logit_ref[...] = jnp.dot(u, wr_ref[...], precision=HIGHEST, preferred_element_type=F32) + br_ref[...]


def _resid_ln(h, y, mods, mods_next, ln_g, ln_b, rows_in, rows_out, *, alpha, gate_idx, next_mod=None, router=None):
    d = h.shape[1]
    t = rows_out.tile
    if rows_in is rows_out:
        in_row = lambda i: i
    else:
        in_row = rows_in.latent_block
    with_u = next_mod is not None
    with_router = router is not None
    shift_idx, scale_idx = next_mod if with_u else (0, 0)
    in_specs = [
        pl.BlockSpec((t, d), lambda i: (in_row(i), 0)),
        pl.BlockSpec((t, d), lambda i: (in_row(i), 0)),
        pl.BlockSpec((None, None, N_MOD, d), lambda i: (rows_out.batch_of(i), rows_out.seg_of(i), 0, 0)),
        pl.BlockSpec((None, None, N_MOD, d), lambda i: (rows_out.batch_of(i), rows_out.seg_of(i), 0, 0)),
        pl.BlockSpec((1, d), lambda i: (0, 0)),
        pl.BlockSpec((1, d), lambda i: (0, 0)),
    ]
    args = [h, y, mods, mods_next, ln_g.reshape(1, d), ln_b.reshape(1, d)]
    out_shape = [jax.ShapeDtypeStruct((rows_out.n_rows, d), F32)]
    out_specs = [pl.BlockSpec((t, d), lambda i: (i, 0))]
    if with_router:
        wr, br = router
        in_specs += [pl.BlockSpec((d, LANES), lambda i: (0, 0)), pl.BlockSpec((1, LANES), lambda i: (0, 0))]
        args += [wr, br]
    if with_u:
        out_shape.append(jax.ShapeDtypeStruct((rows_out.n_rows, d), BF16))
        out_specs.append(pl.BlockSpec((t, d), lambda i: (i, 0)))
    if with_router:
        out_shape.append(jax.ShapeDtypeStruct((rows_out.n_rows, LANES), F32))
        out_specs.append(pl.BlockSpec((t, LANES), lambda i: (i, 0)))
    return pl.pallas_call(
        functools.partial(_resid_ln_kernel, alpha=alpha, gate_idx=gate_idx, shift_idx=shift_idx,
                          scale_idx=scale_idx, with_u=with_u, with_router=with_router),
        out_shape=out_shape,
        grid=(rows_out.n_blocks,),
        in_specs=in_specs,
        out_specs=out_specs,
        compiler_params=_cparams("parallel"),
        name="resid_ln",
    )(*args)


def _gla_kernel(q_ref, k_ref, v_ref, r_ref, w2_ref, gb_ref, o_ref, state_ref, *, q_scale):
    direction = pl.program_id(0)
    step = pl.program_id(3)

    @pl.when(step == 0)
    def _():
        state_ref[...] = jnp.zeros_like(state_ref)

    c = q_ref.shape[0]
    z = jnp.dot(r_ref[...], w2_ref[...], precision=HIGHEST, preferred_element_type=F32) + gb_ref[...]
    log_a = (jnp.minimum(z, 0.0) - jnp.log(1.0 + jnp.exp(-jnp.abs(z)))) * (1.0 / GLA_GATE_NORM)
    row = lax.broadcasted_iota(jnp.int32, (c, c), 0)
    col = lax.broadcasted_iota(jnp.int32, (c, c), 1)
    fwd = direction == 0
    tri = jnp.where(fwd, row, col) >= jnp.where(fwd, col, row)
    cum = jnp.dot(tri.astype(F32), log_a, precision=HIGHEST, preferred_element_type=F32)
    total = jnp.sum(log_a, axis=0, keepdims=True)
    q_dec = (q_ref[...].astype(F32) * q_scale * jnp.exp(cum)).astype(BF16)
    k = k_ref[...].astype(F32)
    k_intra = (k * jnp.exp(-cum)).astype(BF16)
    k_state = (k * jnp.exp(total - cum)).astype(BF16)
    v = v_ref[...]
    scores = jnp.where(tri, _dot_nt(q_dec, k_intra), 0.0)
    state_t = state_ref[...]
    o = _dot(scores.astype(BF16), v) + _dot_nt(q_dec, state_t.astype(BF16))
    o_ref[...] = o
    kv_t = lax.dot_general(v, k_state, (((0,), (0,)), ((), ())), preferred_element_type=F32)
    state_ref[...] = state_t * jnp.exp(total) + kv_t


def _gla_scan(qkvg, r, w2, gate_b, rows):
    t_rows = qkvg.shape[0]
    dk_tot = w2.shape[2]
    dk = dk_tot // GLA_HEADS
    dv_tot = (qkvg.shape[1] - 2 * dk_tot) // 2
    dv = dv_tot // GLA_HEADS
    c = GLA_CHUNK
    per_tile = rows.tile // c
    nc_ctx, nc_lat = rows.nb_ctx * per_tile, rows.nb_lat * per_tile
    nc_seq = nc_ctx + nc_lat

    def chunk_row(d, b, n):
        bwd = jnp.where(n < nc_ctx, nc_ctx - 1 - n, nc_seq + nc_ctx - 1 - n)
        return b * nc_seq + jnp.where(d == 0, n, bwd)

    return pl.pallas_call(
        functools.partial(_gla_kernel, q_scale=float(dk) ** -0.5),
        out_shape=jax.ShapeDtypeStruct((2, t_rows, dv_tot), F32),
        grid=(2, rows.batch, GLA_HEADS, nc_seq),
        in_specs=[
            pl.BlockSpec((c, dk), lambda d, b, h, n: (chunk_row(d, b, n), h)),
            pl.BlockSpec((c, dk), lambda d, b, h, n: (chunk_row(d, b, n), GLA_HEADS + h)),
            pl.BlockSpec((c, dv), lambda d, b, h, n: (chunk_row(d, b, n), (2 * dk_tot) // dv + h)),
            pl.BlockSpec((c, LANES), lambda d, b, h, n: (chunk_row(d, b, n), d)),
            pl.BlockSpec((None, LANES, dk), lambda d, b, h, n: (d, 0, h)),
            pl.BlockSpec((None, 1, dk), lambda d, b, h, n: (d, 0, h)),
        ],
        out_specs=pl.BlockSpec((None, c, dv), lambda d, b, h, n: (d, chunk_row(d, b, n), h)),
        scratch_shapes=[pltpu.VMEM((dv, dk), F32)],
        compiler_params=_cparams("parallel", "parallel", "parallel", "arbitrary"),
        name="gla_scan",
    )(qkvg, qkvg, qkvg, r, w2, gate_b)


def _gla_post_kernel(of_ref, ob_ref, g_ref, ng_ref, y_ref, *, dv):
    for h in range(GLA_HEADS):
        sl = slice(h * dv, (h + 1) * dv)
        o = of_ref[:, sl] + ob_ref[:, sl]
        o = o * lax.rsqrt(jnp.mean(o * o, axis=-1, keepdims=True) + LN_EPS) * ng_ref[...]
        g = g_ref[:, sl].astype(F32)
        y_ref[:, sl] = (o * (g * _sigmoid(g))).astype(y_ref.dtype)


def _gla_post(o2, qkvg, norm_g, rows):
    _, t_rows, dv_tot = o2.shape
    dv = dv_tot // GLA_HEADS
    t = rows.tile
    g_block = (qkvg.shape[1] - dv_tot) // dv_tot
    return pl.pallas_call(
        functools.partial(_gla_post_kernel, dv=dv),
        out_shape=jax.ShapeDtypeStruct((t_rows, dv_tot), BF16),
        grid=(rows.n_blocks,),
        in_specs=[
            pl.BlockSpec((None, t, dv_tot), lambda i: (0, i, 0)),
            pl.BlockSpec((None, t, dv_tot), lambda i: (1, i, 0)),
            pl.BlockSpec((t, dv_tot), lambda i: (i, g_block)),
            pl.BlockSpec((1, dv), lambda i: (0, 0)),
        ],
        out_specs=pl.BlockSpec((t, dv_tot), lambda i: (i, 0)),
        compiler_params=_cparams("parallel"),
        name="gla_post",
    )(o2, o2, qkvg, norm_g.reshape(1, dv).astype(F32))


def _gla_mixer(u, rows, w_in, gate_w1, gate_w2, gate_b, norm_g, w_out):
    d = u.shape[1]
    dk_tot = gate_w2.shape[2]
    qkvg = _matmul(u, w_in.astype(BF16), name="gla_in")
    w1 = jnp.zeros((d, 2 * LANES), F32)
    w2 = jnp.zeros((2, LANES, dk_tot), F32)
    for dd in range(2):
        w1 = w1.at[:, dd * LANES:dd * LANES + GLA_GATE_RANK].set(gate_w1[dd])
        w2 = w2.at[dd, :GLA_GATE_RANK].set(gate_w2[dd])
    r = _matmul(u, w1.astype(BF16), out_dtype=F32, name="gla_gate_in")
    o2 = _gla_scan(qkvg, r, w2, gate_b.reshape(2, 1, dk_tot).astype(F32), rows)
    y = _gla_post(o2, qkvg, norm_g, rows)
    return _matmul(y, w_out.astype(BF16), name="gla_out")


def _rope_kernel(x_ref, cos_ref, sin_ref, q_ref, k2_ref, v2_ref, *, q_width, kv_width):
    cos = cos_ref[...]
    sin = sin_ref[...]
    lane = lax.broadcasted_iota(jnp.int32, cos.shape, 1)
    first = (lane % 32) < 16
    low = lane < SWA_HEAD_DIM

    def rope(slab):
        partner = jnp.where(first, pltpu.roll(slab, LANES - 16, 1), pltpu.roll(slab, 16, 1))
        return slab * cos + partner * sin

    scale = SWA_HEAD_DIM ** -0.5
    for s in range(q_width // LANES):
        slab = x_ref[:, s * LANES:(s + 1) * LANES].astype(F32)
        q_ref[:, s * LANES:(s + 1) * LANES] = (rope(slab) * scale).astype(q_ref.dtype)
    for s in range(kv_width // LANES):
        k_slab = rope(x_ref[:, q_width + s * LANES:q_width + (s + 1) * LANES].astype(F32))
        v_slab = x_ref[:, q_width + kv_width + s * LANES:q_width + kv_width + (s + 1) * LANES].astype(F32)
        for slab, out_ref in ((k_slab, k2_ref), (v_slab, v2_ref)):
            swapped = pltpu.roll(slab, SWA_HEAD_DIM, 1)
            out_ref[2 * s] = jnp.where(low, slab, swapped).astype(out_ref.dtype)
            out_ref[2 * s + 1] = jnp.where(low, swapped, slab).astype(out_ref.dtype)


def _rope_tables(n_ctx, seq):
    t = jnp.arange(seq, dtype=jnp.int32)
    row = (t // GRID_W).astype(F32)
    col = (t % GRID_W).astype(F32)
    n_freq = SWA_HEAD_DIM // 4
    inv_freq = ROPE_BASE ** (-jnp.arange(n_freq, dtype=F32) / n_freq)
    lane = jnp.arange(LANES)
    within = lane % SWA_HEAD_DIM
    pos = jnp.where((within < SWA_HEAD_DIM // 2)[None, :], row[:, None], col[:, None])
    ang = pos * inv_freq[lane % n_freq][None, :]
    sign = jnp.where((lane % 32) < 16, -1.0, 1.0)[None, :]
    cos = jnp.concatenate([jnp.ones((n_ctx, LANES), F32), jnp.cos(ang)], axis=0)
    sin = jnp.concatenate([jnp.zeros((n_ctx, LANES), F32), jnp.sin(ang) * sign], axis=0)
    return cos, sin


def _rope_split(qkv, cos, sin, rows, q_width, kv_width):
    t_rows = qkv.shape[0]
    t = rows.tile
    n_kv = kv_width // SWA_HEAD_DIM
    seq_rows = rows.nb_seq * t
    kv_shape = jax.ShapeDtypeStruct((rows.batch, n_kv, seq_rows, LANES), BF16)
    kv_spec = pl.BlockSpec((None, n_kv, t, LANES), lambda i: (i // rows.nb_seq, 0, i % rows.nb_seq, 0))
    return pl.pallas_call(
        functools.partial(_rope_kernel, q_width=q_width, kv_width=kv_width),
        out_shape=[jax.ShapeDtypeStruct((t_rows, q_width), BF16), kv_shape, kv_shape],
        grid=(rows.n_blocks,),
        in_specs=[
            pl.BlockSpec((t, qkv.shape[1]), lambda i: (i, 0)),
            pl.BlockSpec((t, LANES), lambda i: (i % rows.nb_seq, 0)),
            pl.BlockSpec((t, LANES), lambda i: (i % rows.nb_seq, 0)),
        ],
        out_specs=[pl.BlockSpec((t, q_width), lambda i: (i, 0)), kv_spec, kv_spec],
        compiler_params=_cparams("parallel"),
        name="rope_split",
    )(qkv, cos, sin)


def _attend_heads(q_ref, kk, vv, mask, sink_ref, head0, o_ref):
    tq = q_ref.shape[0]
    lane = lax.broadcasted_iota(jnp.int32, (tq, LANES), 1)
    low = lane < SWA_HEAD_DIM
    for pair in range(SWA_GROUP // 2):
        q2 = q_ref[:, pair * LANES:(pair + 1) * LANES]
        outs = []
        for half in range(2):
            qh = jnp.where(low if half == 0 else lane >= SWA_HEAD_DIM, q2, jnp.zeros_like(q2))
            s = _dot_nt(qh, kk)
            if mask is not None:
                s = jnp.where(mask, s, NEG_INF)
            sink = sink_ref[head0 + 2 * pair + half]
            m = jnp.maximum(jnp.max(s, axis=-1, keepdims=True), sink)
            p = jnp.exp(s - m)
            denom = jnp.sum(p, axis=-1, keepdims=True) + jnp.exp(sink - m)
            outs.append(_dot(p.astype(BF16), vv) / denom)
        o_ref[:, pair * LANES:(pair + 1) * LANES] = jnp.where(low, outs[0], outs[1]).astype(o_ref.dtype)


def _attn_ctx_kernel(sink_ref, q_ref, k_ref, v_ref, o_ref):
    head0 = pl.program_id(1) * SWA_GROUP
    _attend_heads(q_ref, k_ref[...], v_ref[...], None, sink_ref, head0, o_ref)


def _attn_lat_kernel(sink_ref, q_ref, kc_ref, vc_ref, kp_ref, vp_ref, km_ref, vm_ref, kn_ref, vn_ref,
                     o_ref, kk_ref, vv_ref, *, n_ctx, n_qblocks):
    n = pl.program_id(2)
    head0 = pl.program_id(1) * SWA_GROUP
    w = WINDOW
    for ref, parts in ((kk_ref, (kc_ref, kp_ref, km_ref, kn_ref)), (vv_ref, (vc_ref, vp_ref, vm_ref, vn_ref))):
        ref[0:n_ctx] = parts[0][...]
        ref[n_ctx:n_ctx + w] = parts[1][...]
        ref[n_ctx + w:n_ctx + 2 * w] = parts[2][...]
        ref[n_ctx + 2 * w:n_ctx + 3 * w] = parts[3][...]
    nk = n_ctx + 3 * w
    qpos = lax.broadcasted_iota(jnp.int32, (w, nk), 0)
    col = lax.broadcasted_iota(jnp.int32, (w, nk), 1)
    rel = jnp.where(col < n_ctx, 0, col - (n_ctx + w) - qpos)
    far = 4 * w
    no_prev = jnp.where(n > 0, 0, far)
    no_next = jnp.where(n < n_qblocks - 1, 0, far)
    missing = jnp.where(col < n_ctx, 0, jnp.where(col < n_ctx + w, no_prev, jnp.where(col >= n_ctx + 2 * w, no_next, 0)))
    mask = jnp.abs(rel) + missing <= w
    _attend_heads(q_ref, kk_ref[...], vv_ref[...], mask, sink_ref, head0, o_ref)


def _window_attention(q, k2, v2, sink, rows, n_ctx):
    t_rows, q_width = q.shape
    batch, n_kv, seq_rows, _ = k2.shape
    w = WINDOW
    gw = SWA_GROUP * SWA_HEAD_DIM
    nq_seq = seq_rows // w
    nq_ctx = n_ctx // w
    nq_lat = nq_seq - nq_ctx
    smem = pl.BlockSpec(memory_space=pltpu.SMEM)
    out_shape = jax.ShapeDtypeStruct((t_rows, q_width), BF16)

    ctx_kv = pl.BlockSpec((None, None, n_ctx, LANES), lambda b, g, n: (b, g, 0, 0))
    o_ctx = pl.pallas_call(
        _attn_ctx_kernel,
        out_shape=out_shape,
        grid=(batch, n_kv, nq_ctx),
        in_specs=[smem, pl.BlockSpec((w, gw), lambda b, g, n: (b * nq_seq + n, g)), ctx_kv, ctx_kv],
        out_specs=pl.BlockSpec((w, gw), lambda b, g, n: (b * nq_seq + n, g)),
        compiler_params=_cparams("parallel", "parallel", "parallel"),
        name="attn_ctx",
    )(sink, q, k2, v2)

    def band(shift):
        def index(b, g, n):
            return (b, g, nq_ctx + jnp.clip(n + shift, 0, nq_lat - 1), 0)
        return pl.BlockSpec((None, None, w, LANES), index)

    nk = n_ctx + 3 * w
    o = pl.pallas_call(
        functools.partial(_attn_lat_kernel_aliased, n_ctx=n_ctx, n_qblocks=nq_lat),
        out_shape=out_shape,
        grid=(batch, n_kv, nq_lat),
        in_specs=[smem, pl.BlockSpec(memory_space=pl.ANY),
                  pl.BlockSpec((w, gw), lambda b, g, n: (b * nq_seq + nq_ctx + n, g)),
                  ctx_kv, ctx_kv, band(-1), band(-1), band(0), band(0), band(1), band(1)],
        out_specs=pl.BlockSpec((w, gw), lambda b, g, n: (b * nq_seq + nq_ctx + n, g)),
        scratch_shapes=[pltpu.VMEM((nk, LANES), BF16), pltpu.VMEM((nk, LANES), BF16)],
        input_output_aliases={1: 0},
        compiler_params=_cparams("parallel", "parallel", "parallel"),
        name="attn_lat",
    )(sink, o_ctx, q, k2, v2, k2, v2, k2, v2, k2, v2)
    return o


def _attn_lat_kernel_aliased(sink_ref, o_in_ref, *rest, n_ctx, n_qblocks):
    del o_in_ref
    _attn_lat_kernel(sink_ref, *rest, n_ctx=n_ctx, n_qblocks=n_qblocks)


def _window_mixer(u, rows, n_ctx, seq, w_qkv, b_qkv, sink, w_out, b_out):
    q_width = w_out.shape[0]
    kv_width = (w_qkv.shape[1] - q_width) // 2
    qkv = _matmul(u, w_qkv.astype(BF16), bias=b_qkv, name="swa_qkv")
    cos, sin = _rope_tables(n_ctx, seq)
    q, k2, v2 = _rope_split(qkv, cos, sin, rows, q_width, kv_width)
    o = _window_attention(q, k2, v2, sink.astype(F32), rows, n_ctx)
    return _matmul(o, w_out.astype(BF16), bias=b_out, name="swa_out")


def _conv_kernel(gi_ref, go_ref, val_ref, w_ref, o_ref, p_ref, *, n_ctx):
    length = gi_ref.shape[0]
    pad = 8
    p_ref[0:pad] = jnp.zeros((pad, p_ref.shape[1]), F32)
    p_ref[pad + length:pad + length + pad] = jnp.zeros((pad, p_ref.shape[1]), F32)
    p_ref[pad:pad + length] = gi_ref[...].astype(F32) * val_ref[...].astype(F32)
    t = lax.broadcasted_iota(jnp.int32, (length, 1), 0)
    seg_start = t * (t - n_ctx) == 0
    seg_end = (t - (n_ctx - 1)) * (t - (length - 1)) == 0
    prev = jnp.where(seg_start, 0.0, p_ref[pad - 1:pad - 1 + length])
    nxt = jnp.where(seg_end, 0.0, p_ref[pad + 1:pad + 1 + length])
    z = prev * w_ref[0:1, :] + p_ref[pad:pad + length] * w_ref[1:2, :] + nxt * w_ref[2:3, :]
    o_ref[...] = (go_ref[...].astype(F32) * z).astype(o_ref.dtype)


def _conv_gate(proj, conv_w, rows, n_ctx):
    t_rows = proj.shape[0]
    d = proj.shape[1] // 3
    length = rows.nb_seq * rows.tile
    tc = LANES
    nc = d // tc
    return pl.pallas_call(
        functools.partial(_conv_kernel, n_ctx=n_ctx),
        out_shape=jax.ShapeDtypeStruct((t_rows, d), BF16),
        grid=(rows.batch, nc),
        in_specs=[
            pl.BlockSpec((length, tc), lambda b, c: (b, c)),
            pl.BlockSpec((length, tc), lambda b, c: (b, nc + c)),
            pl.BlockSpec((length, tc), lambda b, c: (b, 2 * nc + c)),
            pl.BlockSpec((CONV_WIDTH, tc), lambda b, c: (0, c)),
        ],
        out_specs=pl.BlockSpec((length, tc), lambda b, c: (b, c)),
        scratch_shapes=[pltpu.VMEM((length + 16, tc), F32)],
        compiler_params=_cparams("parallel", "parallel"),
        name="conv_gate",
    )(proj, proj, proj, conv_w.astype(F32))


def _conv_mixer(u, rows, n_ctx, w_in, w_conv, w_out):
    proj = _matmul(u, w_in.astype(BF16), name="conv_in")
    zg = _conv_gate(proj, w_conv, rows, n_ctx)
    return _matmul(zg, w_out.astype(BF16), name="conv_out")


def _moe_kernel(blk_exp_ref, n_used_ref, x_ref, wgu_ref, bgu_ref, wdn_ref, bdn_ref, o_ref):
    del blk_exp_ref

    @pl.when(pl.program_id(0) < n_used_ref[0])
    def _():
        ff = wdn_ref.shape[0]
        gu = _dot(x_ref[...], wgu_ref[...]) + bgu_ref[...]
        glu = jnp.minimum(gu[:, :ff], SWIGLU_LIMIT)
        lin = jnp.clip(gu[:, ff:], -SWIGLU_LIMIT, SWIGLU_LIMIT)
        act = glu * _sigmoid(SWIGLU_ALPHA * glu) * (lin + 1.0)
        o_ref[...] = (_dot(act.astype(BF16), wdn_ref[...]) + bdn_ref[...]).astype(o_ref.dtype)


def _expert_mlp(xs, blk_exp, n_used, w_gate_up, b_gate_up, w_down, b_down):
    n_rows, d = xs.shape
    n_exp, _, ff2 = w_gate_up.shape
    ff = ff2 // 2
    tm = MOE_ROW_BLOCK
    grid_spec = pltpu.PrefetchScalarGridSpec(
        num_scalar_prefetch=2,
        grid=(n_rows // tm,),
        in_specs=[
            pl.BlockSpec((tm, d), lambda i, be, nu: (i, 0)),
            pl.BlockSpec((None, d, ff2), lambda i, be, nu: (be[i], 0, 0)),
            pl.BlockSpec((None, 1, ff2), lambda i, be, nu: (be[i], 0, 0)),
            pl.BlockSpec((None, ff, d), lambda i, be, nu: (be[i], 0, 0)),
            pl.BlockSpec((None, 1, d), lambda i, be, nu: (be[i], 0, 0)),
        ],
        out_specs=pl.BlockSpec((tm, d), lambda i, be, nu: (i, 0)),
    )
    return pl.pallas_call(
        _moe_kernel,
        out_shape=jax.ShapeDtypeStruct((n_rows, d), BF16),
        grid_spec=grid_spec,
        compiler_params=_cparams("arbitrary"),
        name="expert_mlp",
    )(blk_exp, n_used, xs, w_gate_up, b_gate_up.reshape(n_exp, 1, ff2).astype(F32),
      w_down, b_down.reshape(n_exp, 1, d).astype(F32))


def _moe(u, logits, w_gate_up, b_gate_up, w_down, b_down):
    n_tok, d = u.shape
    tm = MOE_ROW_BLOCK
    top_val, top_exp = lax.top_k(logits[:, :N_EXPERTS], TOP_K)
    gates = jax.nn.softmax(top_val, axis=-1)
    n_assign = n_tok * TOP_K
    flat_e = top_exp.reshape(-1).astype(jnp.int32)
    order = jnp.argsort(flat_e, stable=True).astype(jnp.int32)
    e_s = flat_e[order]
    counts = jnp.bincount(flat_e, length=N_EXPERTS).astype(jnp.int32)
    padded = (counts + tm - 1) // tm * tm
    start = jnp.cumsum(counts) - counts
    pad_end = jnp.cumsum(padded)
    pad_start = pad_end - padded
    dest = pad_start[e_s] + jnp.arange(n_assign, dtype=jnp.int32) - start[e_s]
    n_blocks = n_assign // tm + N_EXPERTS
    n_rows = n_blocks * tm
    row_tok = jnp.zeros((n_rows,), jnp.int32).at[dest].set(order // TOP_K)
    blk_exp = jnp.minimum(jnp.searchsorted(pad_end, jnp.arange(n_blocks, dtype=jnp.int32) * tm, side='right'),
                          N_EXPERTS - 1).astype(jnp.int32)
    n_used = (pad_end[-1:] // tm).astype(jnp.int32)
    xs = u[row_tok]
    y_rows = _expert_mlp(xs, blk_exp, n_used, w_gate_up.astype(BF16), b_gate_up, w_down.astype(BF16), b_down)
    pos = jnp.zeros((n_assign,), jnp.int32).at[order].set(dest)
    y = y_rows[pos].reshape(n_tok, TOP_K, d).astype(F32) * gates[:, :, None]
    return jnp.sum(y, axis=1)


def kernel(x, c, ctx, c_ctx, ada_w, ada_b, ln_g, ln_b, gla_w_in, gla_gate_w1, gla_gate_w2, gla_gate_b, gla_norm_g, gla_w_out, swa_w_qkv, swa_b_qkv, swa_sink, swa_w_out, swa_b_out, conv_w_in, conv_w, conv_w_out, moe_router_w, moe_router_b, moe_w_gate_up, moe_b_gate_up, moe_w_down, moe_b_down):
    batch, seq, d = x.shape
    n_ctx = ctx.shape[1]
    depth = ada_w.shape[0]
    alpha = (2 * depth) ** 0.25
    rows_full = _Rows(batch, n_ctx // ROW_TILE, seq // ROW_TILE)
    rows_lat = _Rows(batch, 0, seq // ROW_TILE)

    cond = jnp.zeros((8, d), F32).at[0].set(c_ctx).at[1:1 + batch].set(c)
    ada = _ada_mods(cond, ada_w, ada_b)
    mods_ctx = jnp.broadcast_to(ada[:, 0:1], (depth, batch, N_MOD * d))
    mods = jnp.stack([mods_ctx, ada[:, 1:1 + batch]], axis=2).reshape(depth, batch, 2, N_MOD, d)

    router_w = jnp.zeros((depth, d, LANES), F32).at[:, :, :N_EXPERTS].set(moe_router_w)
    router_b = jnp.zeros((depth, 1, LANES), F32).at[:, 0, :N_EXPERTS].set(moe_router_b)

    h = jnp.concatenate([ctx, x], axis=1).reshape(rows_full.n_rows, d)
    u = _modulate(h, mods[0], rows_full, 0, 1)
    rows = rows_full
    for i in range(depth):
        last = i == depth - 1
        kind, j = i % 3, i // 3
        if kind == 0:
            y = _gla_mixer(u, rows, gla_w_in[j], gla_gate_w1[j], gla_gate_w2[j], gla_gate_b[j], gla_norm_g[j], gla_w_out[j])
        elif kind == 1:
            y = _window_mixer(u, rows, n_ctx, seq, swa_w_qkv[j], swa_b_qkv[j], swa_sink[j], swa_w_out[j], swa_b_out[j])
        else:
            y = _conv_mixer(u, rows, n_ctx, conv_w_in[j], conv_w[j], conv_w_out[j])
        rows_out = rows_lat if last else rows
        h, u, logits = _resid_ln(h, y, mods[i], mods[i], ln_g[i, 0], ln_b[i, 0], rows, rows_out, alpha=alpha, gate_idx=2,
                                 next_mod=(3, 4), router=(router_w[i], router_b[i]))
        rows = rows_out
        y = _moe(u, logits, moe_w_gate_up[i], moe_b_gate_up[i], moe_w_down[i], moe_b_down[i])
        if last:
            (h,) = _resid_ln(h, y, mods[i], mods[i], ln_g[i, 1], ln_b[i, 1], rows, rows, alpha=alpha, gate_idx=5)
        else:
            h, u = _resid_ln(h, y, mods[i], mods[i + 1], ln_g[i, 1], ln_b[i, 1], rows, rows, alpha=alpha, gate_idx=5,
                             next_mod=(0, 1))
    return h.reshape(batch, seq, d)
```

```python
import functools

import jax
import jax.numpy as jnp
from jax import lax
from jax.experimental import pallas as pl
from jax.experimental.pallas import tpu as pltpu

F32 = jnp.float32
BF16 = jnp.bfloat16
HIGHEST = lax.Precision.HIGHEST

GRID_W = 64
GLA_HEADS = 4
GLA_GATE_RANK = 16
GLA_GATE_NORM = 16.0
GLA_CHUNK = 64
SWA_HEAD_DIM = 64
SWA_GROUP = 8
WINDOW = 128
ROPE_BASE = 10000.0
CONV_WIDTH = 3
N_EXPERTS = 32
TOP_K = 4
SWIGLU_LIMIT = 7.0
SWIGLU_ALPHA = 1.702
N_MOD = 6
LN_EPS = 1e-5
NEG_INF = -1e30

LANES = 128
ROW_TILE = 256
MOE_ROW_BLOCK = 256
VMEM_LIMIT = 48 * 1024 * 1024
MOE_VMEM_LIMIT = 60 * 1024 * 1024


def _cparams(*sem):
    return pltpu.CompilerParams(dimension_semantics=sem, vmem_limit_bytes=VMEM_LIMIT)


def _sigmoid(x):
    return 1.0 / (1.0 + jnp.exp(-x))


def _dot(a, b):
    return jnp.dot(a, b, preferred_element_type=F32)


def _dot_nt(a, b):
    return lax.dot_general(a, b, (((1,), (1,)), ((), ())), preferred_element_type=F32)


def _split_bf16(x):
    hi = x.astype(BF16)
    lo = (x - hi.astype(F32)).astype(BF16)
    return hi, lo


class _Rows:
    def __init__(self, batch, nb_ctx, nb_lat, tile=ROW_TILE):
        self.batch, self.nb_ctx, self.nb_lat, self.tile = batch, nb_ctx, nb_lat, tile
        self.nb_seq = nb_ctx + nb_lat
        self.n_blocks = batch * self.nb_seq
        self.n_rows = self.n_blocks * tile

    def batch_of(self, i):
        return i // self.nb_seq

    def seg_of(self, i):
        if self.nb_ctx == 0:
            return 1
        return jnp.where(i % self.nb_seq >= self.nb_ctx, 1, 0)

    def latent_block(self, i):
        return (i // self.nb_lat) * self.nb_seq + self.nb_ctx + i % self.nb_lat


def _ada_kernel(c_ref, w_ref, b_ref, o_ref):
    c = c_ref[...]
    s = c * _sigmoid(c)
    o_ref[...] = jnp.dot(s, w_ref[...], precision=HIGHEST, preferred_element_type=F32) + b_ref[...]


def _ada_mods(cond, ada_w, ada_b):
    depth, d, n = ada_w.shape
    tn = 1024 if n % 1024 == 0 else n
    return pl.pallas_call(
        _ada_kernel,
        out_shape=jax.ShapeDtypeStruct((depth, 8, n), F32),
        grid=(depth, n // tn),
        in_specs=[
            pl.BlockSpec((8, d), lambda l, j: (0, 0)),
            pl.BlockSpec((None, d, tn), lambda l, j: (l, 0, j)),
            pl.BlockSpec((None, 1, tn), lambda l, j: (l, 0, j)),
        ],
        out_specs=pl.BlockSpec((None, 8, tn), lambda l, j: (l, 0, j)),
        compiler_params=_cparams("parallel", "parallel"),
        name="ada_mods",
    )(cond, ada_w, ada_b.reshape(depth, 1, n))


def _modulate_kernel(h_ref, m_ref, u_ref, *, shift_idx, scale_idx):
    h = h_ref[...]
    u = h * (1.0 + m_ref[scale_idx:scale_idx + 1, :]) + m_ref[shift_idx:shift_idx + 1, :]
    u_ref[...] = u.astype(u_ref.dtype)


def _modulate(h, mods, rows, shift_idx, scale_idx):
    d = h.shape[1]
    t = rows.tile
    return pl.pallas_call(
        functools.partial(_modulate_kernel, shift_idx=shift_idx, scale_idx=scale_idx),
        out_shape=jax.ShapeDtypeStruct(h.shape, BF16),
        grid=(rows.n_blocks,),
        in_specs=[
            pl.BlockSpec((t, d), lambda i: (i, 0)),
            pl.BlockSpec((None, None, N_MOD, d), lambda i: (rows.batch_of(i), rows.seg_of(i), 0, 0)),
        ],
        out_specs=pl.BlockSpec((t, d), lambda i: (i, 0)),
        compiler_params=_cparams("parallel"),
        name="modulate",
    )(h, mods)


def _mm_kernel(x_ref, w_ref, o_ref):
    o_ref[...] = _dot(x_ref[...], w_ref[...]).astype(o_ref.dtype)


def _mm_bias_kernel(x_ref, w_ref, b_ref, o_ref):
    o_ref[...] = (_dot(x_ref[...], w_ref[...]) + b_ref[...]).astype(o_ref.dtype)


def _pick_tile(n, candidates):
    for c in candidates:
        if n % c == 0:
            return c
    return n


def _matmul(x, w, bias=None, out_dtype=BF16, name="proj"):
    m, k = x.shape
    n = w.shape[1]
    tm = _pick_tile(m, (1024, 512, 256))
    tn = _pick_tile(n, (1024, 768, 640, 512, 256, 128))
    in_specs = [
        pl.BlockSpec((tm, k), lambda i, j: (i, 0)),
        pl.BlockSpec((k, tn), lambda i, j: (0, j)),
    ]
    args = [x, w]
    body = _mm_kernel
    if bias is not None:
        in_specs.append(pl.BlockSpec((1, tn), lambda i, j: (0, j)))
        args.append(bias.reshape(1, n).astype(F32))
        body = _mm_bias_kernel
    return pl.pallas_call(
        body,
        out_shape=jax.ShapeDtypeStruct((m, n), out_dtype),
        grid=(m // tm, n // tn),
        in_specs=in_specs,
        out_specs=pl.BlockSpec((tm, tn), lambda i, j: (i, j)),
        compiler_params=_cparams("parallel", "parallel"),
        name=name,
    )(*args)


def _route_block(logits, count_ref):
    t = logits.shape[0]
    lane = lax.broadcasted_iota(jnp.int32, (t, LANES), 1)
    lane_f = lane.astype(F32)
    left = jnp.where(lane < N_EXPERTS, logits, NEG_INF)
    hot = jnp.zeros((t, LANES), F32)
    sels, vals = [], []
    for _ in range(TOP_K):
        m = jnp.max(left, axis=-1, keepdims=True)
        sel = jnp.min(jnp.where(left == m, lane_f, float(LANES)), axis=-1, keepdims=True)
        pick = lane_f == sel
        left = jnp.where(pick, NEG_INF, left)
        hot = jnp.where(pick, 1.0, hot)
        sels.append(sel)
        vals.append(m)
    row = lax.broadcasted_iota(jnp.int32, (t, t), 0)
    col = lax.broadcasted_iota(jnp.int32, (t, t), 1)
    earlier = jnp.where(row > col, 1.0, 0.0).astype(BF16)
    before = _dot(earlier, hot.astype(BF16)) + count_ref[...]
    count_ref[...] += jnp.sum(hot, axis=0, keepdims=True)
    exps = [jnp.exp(v - vals[0]) for v in vals]
    denom = exps[0]
    for e in exps[1:]:
        denom = denom + e
    idx = jnp.zeros((t, LANES), jnp.int32)
    gates = jnp.zeros((t, LANES), F32)
    for k in range(TOP_K):
        rank = jnp.sum(jnp.where(lane_f == sels[k], before, 0.0), axis=-1, keepdims=True)
        idx = jnp.where(lane == k, sels[k].astype(jnp.int32), idx)
        idx = jnp.where(lane == TOP_K + k, rank.astype(jnp.int32), idx)
        gates = jnp.where(lane == k, exps[k] / denom, gates)
    return idx, gates


def _resid_ln_kernel(*refs, alpha, gate_idx, shift_idx, scale_idx, with_u, with_router, combine):
    h_ref, y_ref = refs[:2]
    pos = 2
    if combine:
        yg_ref = refs[pos]
        pos += 1
    m_ref, mn_ref, g_ref, b_ref = refs[pos:pos + 4]
    pos += 4
    if with_router:
        wr_hi_ref, wr_lo_ref, br_ref = refs[pos:pos + 3]
        pos += 3
    hn_ref = refs[pos]
    pos += 1
    if combine:
        gates = yg_ref[...]
        y = gates[:, 0:1] * y_ref[0].astype(F32)
        for k in range(1, TOP_K):
            y = y + gates[:, k:k + 1] * y_ref[k].astype(F32)
    else:
        y = y_ref[...].astype(F32)
    z = alpha * h_ref[...] + y * m_ref[gate_idx:gate_idx + 1, :]
    zc = z - jnp.mean(z, axis=-1, keepdims=True)
    var = jnp.mean(zc * zc, axis=-1, keepdims=True)
    hn = zc * lax.rsqrt(var + LN_EPS) * g_ref[...] + b_ref[...]
    hn_ref[...] = hn
    if with_u:
        u_ref = refs[pos]
        pos += 1
        u = hn * (1.0 + mn_ref[scale_idx:scale_idx + 1, :]) + mn_ref[shift_idx:shift_idx + 1, :]
        u_ref[...] = u.astype(u_ref.dtype)
        if with_router:
            idx_ref, gate_ref, cnt_ref, count_ref = refs[pos:pos + 4]

            @pl.when(pl.program_id(0) == 0)
            def _():
                count_ref[...] = jnp.zeros_like(count_ref)

            u_hi, u_lo = _split_bf16(u)
            w_hi = wr_hi_ref[...]
            logits = _dot(u_hi, w_hi) + _dot(u_lo, w_hi) + _dot(u_hi, wr_lo_ref[...]) + br_ref[...]
            idx, gates = _route_block(logits, count_ref)
            idx_ref[...] = idx
            gate_ref[...] = gates
            cnt_ref[...] = count_ref[...]


def _resid_ln(h, y, mods, mods_next, ln_g, ln_b, rows_in, rows_out, *, alpha, gate_idx, next_mod=None, router=None,
              combine_gates=None):
    d = h.shape[1]
    t = rows_out.tile
    if rows_in is rows_out:
        in_row = lambda i: i
    else:
        in_row = rows_in.latent_block
    with_u = next_mod is not None
    with_router = router is not None
    combine = combine_gates is not None
    shift_idx, scale_idx = next_mod if with_u else (0, 0)
    mod_spec = pl.BlockSpec((None, None, N_MOD, d), lambda i: (rows_out.batch_of(i), rows_out.seg_of(i), 0, 0))
    in_specs = [pl.BlockSpec((t, d), lambda i: (in_row(i), 0))]
    args = [h, y]
    if combine:
        in_specs += [pl.BlockSpec((TOP_K, t, d), lambda i: (0, in_row(i), 0)),
                     pl.BlockSpec((t, LANES), lambda i: (in_row(i), 0))]
        args.append(combine_gates)
    else:
        in_specs.append(pl.BlockSpec((t, d), lambda i: (in_row(i), 0)))
    in_specs += [mod_spec, mod_spec, pl.BlockSpec((1, d), lambda i: (0, 0)), pl.BlockSpec((1, d), lambda i: (0, 0))]
    args += [mods, mods_next, ln_g.reshape(1, d), ln_b.reshape(1, d)]
    out_shape = [jax.ShapeDtypeStruct((rows_out.n_rows, d), F32)]
    out_specs = [pl.BlockSpec((t, d), lambda i: (i, 0))]
    scratch = []
    if with_router:
        wr, br = router
        w_spec = pl.BlockSpec((d, LANES), lambda i: (0, 0))
        in_specs += [w_spec, w_spec, pl.BlockSpec((1, LANES), lambda i: (0, 0))]
        args += [*_split_bf16(wr), br]
    if with_u:
        out_shape.append(jax.ShapeDtypeStruct((rows_out.n_rows, d), BF16))
        out_specs.append(pl.BlockSpec((t, d), lambda i: (i, 0)))
    if with_router:
        out_shape += [jax.ShapeDtypeStruct((rows_out.n_rows, LANES), jnp.int32),
                      jax.ShapeDtypeStruct((rows_out.n_rows, LANES), F32),
                      jax.ShapeDtypeStruct((1, LANES), F32)]
        out_specs += [pl.BlockSpec((t, LANES), lambda i: (i, 0)), pl.BlockSpec((t, LANES), lambda i: (i, 0)),
                      pl.BlockSpec((1, LANES), lambda i: (0, 0))]
        scratch.append(pltpu.VMEM((1, LANES), F32))
    return pl.pallas_call(
        functools.partial(_resid_ln_kernel, alpha=alpha, gate_idx=gate_idx, shift_idx=shift_idx,
                          scale_idx=scale_idx, with_u=with_u, with_router=with_router, combine=combine),
        out_shape=out_shape,
        grid=(rows_out.n_blocks,),
        in_specs=in_specs,
        out_specs=out_specs,
        scratch_shapes=scratch,
        compiler_params=_cparams("arbitrary" if with_router else "parallel"),
        name="resid_ln",
    )(*args)


def _gla_kernel(*refs, reverse, fuse_post, q_scale, dk, dv):
    if fuse_post:
        q_ref, k_ref, v_ref, r_ref, w2_ref, gb_ref, other_ref, g_ref, ng_ref, out_ref, state_ref = refs
    else:
        q_ref, k_ref, v_ref, r_ref, w2_ref, gb_ref, out_ref, state_ref = refs

    @pl.when(pl.program_id(1) == 0)
    def _():
        state_ref[...] = jnp.zeros_like(state_ref)

    t = q_ref.shape[0]
    c = GLA_CHUNK
    shift = c.bit_length() - 1
    r_hi, r_lo = _split_bf16(r_ref[...])
    w_hi, w_lo = _split_bf16(w2_ref[...])
    z = _dot(r_hi, w_hi) + _dot(r_lo, w_hi) + _dot(r_hi, w_lo) + gb_ref[...]
    log_a = (jnp.minimum(z, 0.0) - jnp.log(1.0 + jnp.exp(-jnp.abs(z)))) * (1.0 / GLA_GATE_NORM)
    row = lax.broadcasted_iota(jnp.int32, (t, t), 0)
    col = lax.broadcasted_iota(jnp.int32, (t, t), 1)
    same_chunk = lax.shift_right_logical(row, shift) == lax.shift_right_logical(col, shift)
    ahead = (col - row) if reverse else (row - col)
    tri = jnp.where(jnp.where(same_chunk, ahead, -1) >= 0, 1.0, 0.0).astype(BF16)
    a_hi, a_lo = _split_bf16(log_a)
    cum = _dot(tri, a_hi) + _dot(tri, a_lo)
    grow = jnp.exp(cum)
    shrink = jnp.exp(-cum)
    crow = lax.broadcasted_iota(jnp.int32, (c, c), 0)
    ccol = lax.broadcasted_iota(jnp.int32, (c, c), 1)
    causal = (ccol >= crow) if reverse else (crow >= ccol)
    n_chunks = t // c
    for h in range(GLA_HEADS):
        ks = slice(h * dk, (h + 1) * dk)
        vs = slice(h * dv, (h + 1) * dv)
        q_dec = (q_ref[:, ks].astype(F32) * q_scale * grow[:, ks]).astype(BF16)
        k_in = k_ref[:, ks].astype(F32) * shrink[:, ks]
        for step in range(n_chunks):
            j = n_chunks - 1 - step if reverse else step
            rs = slice(j * c, (j + 1) * c)
            edge = j * c if reverse else (j + 1) * c - 1
            decay = jnp.exp(cum[edge:edge + 1, ks])
            qd = q_dec[rs]
            k_intra = k_in[rs].astype(BF16)
            k_state = (k_in[rs] * decay).astype(BF16)
            v = v_ref[rs, vs]
            scores = jnp.where(causal, _dot_nt(qd, k_intra), 0.0)
            state_t = state_ref[h]
            o = _dot(scores.astype(BF16), v) + _dot_nt(qd, state_t.astype(BF16))
            kv_t = lax.dot_general(v, k_state, (((0,), (0,)), ((), ())), preferred_element_type=F32)
            state_ref[h] = state_t * decay + kv_t
            if fuse_post:
                o = o + other_ref[rs, vs]
                o = o * lax.rsqrt(jnp.mean(o * o, axis=-1, keepdims=True) + LN_EPS) * ng_ref[...]
                g = g_ref[rs, vs].astype(F32)
                out_ref[rs, vs] = (o * (g * _sigmoid(g))).astype(out_ref.dtype)
            else:
                out_ref[rs, vs] = o


def _gla_scan(qkvg, r, w2, gate_b, rows, direction, other=None, norm_g=None):
    t_rows = qkvg.shape[0]
    dk_tot = w2.shape[2]
    dk = dk_tot // GLA_HEADS
    dv_tot = (qkvg.shape[1] - 2 * dk_tot) // 2
    dv = dv_tot // GLA_HEADS
    t = rows.tile
    reverse = direction == 1
    fuse_post = other is not None

    def block(b, n):
        if reverse:
            n = jnp.where(n < rows.nb_ctx, rows.nb_ctx - 1 - n, rows.nb_seq + rows.nb_ctx - 1 - n)
        return b * rows.nb_seq + n

    in_specs = [
        pl.BlockSpec((t, dk_tot), lambda b, n: (block(b, n), 0)),
        pl.BlockSpec((t, dk_tot), lambda b, n: (block(b, n), 1)),
        pl.BlockSpec((t, dv_tot), lambda b, n: (block(b, n), (2 * dk_tot) // dv_tot)),
        pl.BlockSpec((t, LANES), lambda b, n: (block(b, n), direction)),
        pl.BlockSpec((None, LANES, dk_tot), lambda b, n: (direction, 0, 0)),
        pl.BlockSpec((None, 1, dk_tot), lambda b, n: (direction, 0, 0)),
    ]
    args = [qkvg, qkvg, qkvg, r, w2, gate_b]
    if fuse_post:
        in_specs += [
            pl.BlockSpec((t, dv_tot), lambda b, n: (block(b, n), 0)),
            pl.BlockSpec((t, dv_tot), lambda b, n: (block(b, n), (2 * dk_tot + dv_tot) // dv_tot)),
            pl.BlockSpec((1, dv), lambda b, n: (0, 0)),
        ]
        args += [other, qkvg, norm_g.reshape(1, dv).astype(F32)]
    return pl.pallas_call(
        functools.partial(_gla_kernel, reverse=reverse, fuse_post=fuse_post, q_scale=float(dk) ** -0.5, dk=dk, dv=dv),
        out_shape=jax.ShapeDtypeStruct((t_rows, dv_tot), BF16 if fuse_post else F32),
        grid=(rows.batch, rows.nb_seq),
        in_specs=in_specs,
        out_specs=pl.BlockSpec((t, dv_tot), lambda b, n: (block(b, n), 0)),
        scratch_shapes=[pltpu.VMEM((GLA_HEADS, dv, dk), F32)],
        compiler_params=_cparams("parallel", "arbitrary"),
        name="gla_scan_bwd" if reverse else "gla_scan_fwd",
    )(*args)


def _gla_mixer(u, rows, w_in, gate_w1, gate_w2, gate_b, norm_g, w_out):
    d = u.shape[1]
    dk_tot = gate_w2.shape[2]
    qkvg = _matmul(u, w_in.astype(BF16), name="gla_in")
    w1 = jnp.zeros((d, 2 * LANES), F32)
    w2 = jnp.zeros((2, LANES, dk_tot), F32)
    for dd in range(2):
        w1 = w1.at[:, dd * LANES:dd * LANES + GLA_GATE_RANK].set(gate_w1[dd])
        w2 = w2.at[dd, :GLA_GATE_RANK].set(gate_w2[dd])
    r = _matmul(u, w1.astype(BF16), out_dtype=F32, name="gla_gate_in")
    gate_b = gate_b.reshape(2, 1, dk_tot).astype(F32)
    o_fwd = _gla_scan(qkvg, r, w2, gate_b, rows, 0)
    y = _gla_scan(qkvg, r, w2, gate_b, rows, 1, other=o_fwd, norm_g=norm_g)
    return _matmul(y, w_out.astype(BF16), name="gla_out")


def _rope_kernel(x_ref, cos_ref, sin_ref, q_ref, k2_ref, v2_ref, *, q_width, kv_width):
    cos = cos_ref[...]
    sin = sin_ref[...]
    lane = lax.broadcasted_iota(jnp.int32, cos.shape, 1)
    first = (lane % 32) < 16
    low = lane < SWA_HEAD_DIM

    def rope(slab):
        partner = jnp.where(first, pltpu.roll(slab, LANES - 16, 1), pltpu.roll(slab, 16, 1))
        return slab * cos + partner * sin

    scale = SWA_HEAD_DIM ** -0.5
    for s in range(q_width // LANES):
        slab = x_ref[:, s * LANES:(s + 1) * LANES].astype(F32)
        q_ref[:, s * LANES:(s + 1) * LANES] = (rope(slab) * scale).astype(q_ref.dtype)
    for s in range(kv_width // LANES):
        k_slab = rope(x_ref[:, q_width + s * LANES:q_width + (s + 1) * LANES].astype(F32))
        v_slab = x_ref[:, q_width + kv_width + s * LANES:q_width + kv_width + (s + 1) * LANES].astype(F32)
        for slab, out_ref in ((k_slab, k2_ref), (v_slab, v2_ref)):
            swapped = pltpu.roll(slab, SWA_HEAD_DIM, 1)
            out_ref[2 * s] = jnp.where(low, slab, swapped).astype(out_ref.dtype)
            out_ref[2 * s + 1] = jnp.where(low, swapped, slab).astype(out_ref.dtype)


def _rope_tables(n_ctx, seq):
    t = jnp.arange(seq, dtype=jnp.int32)
    row = (t // GRID_W).astype(F32)
    col = (t % GRID_W).astype(F32)
    n_freq = SWA_HEAD_DIM // 4
    inv_freq = ROPE_BASE ** (-jnp.arange(n_freq, dtype=F32) / n_freq)
    lane = jnp.arange(LANES)
    within = lane % SWA_HEAD_DIM
    pos = jnp.where((within < SWA_HEAD_DIM // 2)[None, :], row[:, None], col[:, None])
    ang = pos * inv_freq[lane % n_freq][None, :]
    sign = jnp.where((lane % 32) < 16, -1.0, 1.0)[None, :]
    cos = jnp.concatenate([jnp.ones((n_ctx, LANES), F32), jnp.cos(ang)], axis=0)
    sin = jnp.concatenate([jnp.zeros((n_ctx, LANES), F32), jnp.sin(ang) * sign], axis=0)
    return cos, sin


def _rope_split(qkv, cos, sin, rows, q_width, kv_width):
    t_rows = qkv.shape[0]
    t = rows.tile
    n_kv = kv_width // SWA_HEAD_DIM
    seq_rows = rows.nb_seq * t
    kv_shape = jax.ShapeDtypeStruct((rows.batch, n_kv, seq_rows, LANES), BF16)
    kv_spec = pl.BlockSpec((None, n_kv, t, LANES), lambda i: (i // rows.nb_seq, 0, i % rows.nb_seq, 0))
    return pl.pallas_call(
        functools.partial(_rope_kernel, q_width=q_width, kv_width=kv_width),
        out_shape=[jax.ShapeDtypeStruct((t_rows, q_width), BF16), kv_shape, kv_shape],
        grid=(rows.n_blocks,),
        in_specs=[
            pl.BlockSpec((t, qkv.shape[1]), lambda i: (i, 0)),
            pl.BlockSpec((t, LANES), lambda i: (i % rows.nb_seq, 0)),
            pl.BlockSpec((t, LANES), lambda i: (i % rows.nb_seq, 0)),
        ],
        out_specs=[pl.BlockSpec((t, q_width), lambda i: (i, 0)), kv_spec, kv_spec],
        compiler_params=_cparams("parallel"),
        name="rope_split",
    )(qkv, cos, sin)


def _attend_heads(q_ref, kk, vv, mask, sink_ref, head0, o_ref):
    tq = q_ref.shape[0]
    lane = lax.broadcasted_iota(jnp.int32, (tq, LANES), 1)
    low = lane < SWA_HEAD_DIM
    for pair in range(SWA_GROUP // 2):
        q2 = q_ref[:, pair * LANES:(pair + 1) * LANES]
        outs = []
        for half in range(2):
            qh = jnp.where(low if half == 0 else lane >= SWA_HEAD_DIM, q2, jnp.zeros_like(q2))
            s = _dot_nt(qh, kk)
            if mask is not None:
                s = jnp.where(mask, s, NEG_INF)
            sink = sink_ref[head0 + 2 * pair + half]
            m = jnp.maximum(jnp.max(s, axis=-1, keepdims=True), sink)
            p = jnp.exp(s - m)
            denom = jnp.sum(p, axis=-1, keepdims=True) + jnp.exp(sink - m)
            outs.append(_dot(p.astype(BF16), vv) / denom)
        o_ref[:, pair * LANES:(pair + 1) * LANES] = jnp.where(low, outs[0], outs[1]).astype(o_ref.dtype)


def _attn_ctx_kernel(sink_ref, q_ref, k_ref, v_ref, o_ref):
    head0 = pl.program_id(1) * SWA_GROUP
    _attend_heads(q_ref, k_ref[...], v_ref[...], None, sink_ref, head0, o_ref)


def _attn_lat_kernel(sink_ref, q_ref, kc_ref, vc_ref, kp_ref, vp_ref, km_ref, vm_ref, kn_ref, vn_ref,
                     o_ref, kk_ref, vv_ref, *, n_ctx, n_qblocks):
    n = pl.program_id(2)
    head0 = pl.program_id(1) * SWA_GROUP
    w = WINDOW
    for ref, parts in ((kk_ref, (kc_ref, kp_ref, km_ref, kn_ref)), (vv_ref, (vc_ref, vp_ref, vm_ref, vn_ref))):
        ref[0:n_ctx] = parts[0][...]
        ref[n_ctx:n_ctx + w] = parts[1][...]
        ref[n_ctx + w:n_ctx + 2 * w] = parts[2][...]
        ref[n_ctx + 2 * w:n_ctx + 3 * w] = parts[3][...]
    nk = n_ctx + 3 * w
    qpos = lax.broadcasted_iota(jnp.int32, (w, nk), 0)
    col = lax.broadcasted_iota(jnp.int32, (w, nk), 1)
    rel = jnp.where(col < n_ctx, 0, col - (n_ctx + w) - qpos)
    far = 4 * w
    no_prev = jnp.where(n > 0, 0, far)
    no_next = jnp.where(n < n_qblocks - 1, 0, far)
    missing = jnp.where(col < n_ctx, 0, jnp.where(col < n_ctx + w, no_prev, jnp.where(col >= n_ctx + 2 * w, no_next, 0)))
    mask = jnp.abs(rel) + missing <= w
    _attend_heads(q_ref, kk_ref[...], vv_ref[...], mask, sink_ref, head0, o_ref)


def _window_attention(q, k2, v2, sink, rows, n_ctx):
    t_rows, q_width = q.shape
    batch, n_kv, seq_rows, _ = k2.shape
    w = WINDOW
    gw = SWA_GROUP * SWA_HEAD_DIM
    nq_seq = seq_rows // w
    nq_ctx = n_ctx // w
    nq_lat = nq_seq - nq_ctx
    smem = pl.BlockSpec(memory_space=pltpu.SMEM)
    out_shape = jax.ShapeDtypeStruct((t_rows, q_width), BF16)

    ctx_kv = pl.BlockSpec((None, None, n_ctx, LANES), lambda b, g, n: (b, g, 0, 0))
    o_ctx = pl.pallas_call(
        _attn_ctx_kernel,
        out_shape=out_shape,
        grid=(batch, n_kv, nq_ctx),
        in_specs=[smem, pl.BlockSpec((w, gw), lambda b, g, n: (b * nq_seq + n, g)), ctx_kv, ctx_kv],
        out_specs=pl.BlockSpec((w, gw), lambda b, g, n: (b * nq_seq + n, g)),
        compiler_params=_cparams("parallel", "parallel", "parallel"),
        name="attn_ctx",
    )(sink, q, k2, v2)

    def band(shift):
        def index(b, g, n):
            return (b, g, nq_ctx + jnp.clip(n + shift, 0, nq_lat - 1), 0)
        return pl.BlockSpec((None, None, w, LANES), index)

    nk = n_ctx + 3 * w
    o = pl.pallas_call(
        functools.partial(_attn_lat_kernel_aliased, n_ctx=n_ctx, n_qblocks=nq_lat),
        out_shape=out_shape,
        grid=(batch, n_kv, nq_lat),
        in_specs=[smem, pl.BlockSpec(memory_space=pl.ANY),
                  pl.BlockSpec((w, gw), lambda b, g, n: (b * nq_seq + nq_ctx + n, g)),
                  ctx_kv, ctx_kv, band(-1), band(-1), band(0), band(0), band(1), band(1)],
        out_specs=pl.BlockSpec((w, gw), lambda b, g, n: (b * nq_seq + nq_ctx + n, g)),
        scratch_shapes=[pltpu.VMEM((nk, LANES), BF16), pltpu.VMEM((nk, LANES), BF16)],
        input_output_aliases={1: 0},
        compiler_params=_cparams("parallel", "parallel", "parallel"),
        name="attn_lat",
    )(sink, o_ctx, q, k2, v2, k2, v2, k2, v2, k2, v2)
    return o


def _attn_lat_kernel_aliased(sink_ref, o_in_ref, *rest, n_ctx, n_qblocks):
    del o_in_ref
    _attn_lat_kernel(sink_ref, *rest, n_ctx=n_ctx, n_qblocks=n_qblocks)


def _window_mixer(u, rows, n_ctx, seq, w_qkv, b_qkv, sink, w_out, b_out):
    q_width = w_out.shape[0]
    kv_width = (w_qkv.shape[1] - q_width) // 2
    qkv = _matmul(u, w_qkv.astype(BF16), bias=b_qkv, name="swa_qkv")
    cos, sin = _rope_tables(n_ctx, seq)
    q, k2, v2 = _rope_split(qkv, cos, sin, rows, q_width, kv_width)
    o = _window_attention(q, k2, v2, sink.astype(F32), rows, n_ctx)
    return _matmul(o, w_out.astype(BF16), bias=b_out, name="swa_out")


def _conv_kernel(gi_ref, go_ref, val_ref, w_ref, o_ref, p_ref, *, n_ctx):
    length = gi_ref.shape[0]
    pad = 8
    p_ref[0:pad] = jnp.zeros((pad, p_ref.shape[1]), F32)
    p_ref[pad + length:pad + length + pad] = jnp.zeros((pad, p_ref.shape[1]), F32)
    p_ref[pad:pad + length] = gi_ref[...].astype(F32) * val_ref[...].astype(F32)
    t = lax.broadcasted_iota(jnp.int32, (length, 1), 0)
    seg_start = t * (t - n_ctx) == 0
    seg_end = (t - (n_ctx - 1)) * (t - (length - 1)) == 0
    prev = jnp.where(seg_start, 0.0, p_ref[pad - 1:pad - 1 + length])
    nxt = jnp.where(seg_end, 0.0, p_ref[pad + 1:pad + 1 + length])
    z = prev * w_ref[0:1, :] + p_ref[pad:pad + length] * w_ref[1:2, :] + nxt * w_ref[2:3, :]
    o_ref[...] = (go_ref[...].astype(F32) * z).astype(o_ref.dtype)


def _conv_gate(proj, conv_w, rows, n_ctx):
    t_rows = proj.shape[0]
    d = proj.shape[1] // 3
    length = rows.nb_seq * rows.tile
    tc = LANES
    nc = d // tc
    return pl.pallas_call(
        functools.partial(_conv_kernel, n_ctx=n_ctx),
        out_shape=jax.ShapeDtypeStruct((t_rows, d), BF16),
        grid=(rows.batch, nc),
        in_specs=[
            pl.BlockSpec((length, tc), lambda b, c: (b, c)),
            pl.BlockSpec((length, tc), lambda b, c: (b, nc + c)),
            pl.BlockSpec((length, tc), lambda b, c: (b, 2 * nc + c)),
            pl.BlockSpec((CONV_WIDTH, tc), lambda b, c: (0, c)),
        ],
        out_specs=pl.BlockSpec((length, tc), lambda b, c: (b, c)),
        scratch_shapes=[pltpu.VMEM((length + 16, tc), F32)],
        compiler_params=_cparams("parallel", "parallel"),
        name="conv_gate",
    )(proj, proj, proj, conv_w.astype(F32))


def _conv_mixer(u, rows, n_ctx, w_in, w_conv, w_out):
    proj = _matmul(u, w_in.astype(BF16), name="conv_in")
    zg = _conv_gate(proj, w_conv, rows, n_ctx)
    return _matmul(zg, w_out.astype(BF16), name="conv_out")


def _moe_kernel(blk_exp_ref, n_used_ref, x_ref, wgu_ref, bgu_ref, wdn_ref, bdn_ref, o_ref, wgu_bf_ref, wdn_bf_ref):
    i = pl.program_id(0)
    used = i < n_used_ref[0]
    new_expert = (i == 0) | (blk_exp_ref[i] != blk_exp_ref[jnp.maximum(i - 1, 0)])

    @pl.when(used & new_expert)
    def _():
        wgu_bf_ref[...] = wgu_ref[...].astype(BF16)
        wdn_bf_ref[...] = wdn_ref[...].astype(BF16)

    @pl.when(used)
    def _():
        ff = wdn_ref.shape[0]
        gu = _dot(x_ref[...], wgu_bf_ref[...]) + bgu_ref[...]
        glu = jnp.minimum(gu[:, :ff], SWIGLU_LIMIT)
        lin = jnp.clip(gu[:, ff:], -SWIGLU_LIMIT, SWIGLU_LIMIT)
        act = glu * _sigmoid(SWIGLU_ALPHA * glu) * (lin + 1.0)
        o_ref[...] = (_dot(act.astype(BF16), wdn_bf_ref[...]) + bdn_ref[...]).astype(o_ref.dtype)


def _expert_mlp(xs, blk_exp, n_used, w_gate_up, b_gate_up, w_down, b_down):
    n_rows, d = xs.shape
    n_exp, _, ff2 = w_gate_up.shape
    ff = ff2 // 2
    tm = MOE_ROW_BLOCK
    grid_spec = pltpu.PrefetchScalarGridSpec(
        num_scalar_prefetch=2,
        grid=(n_rows // tm,),
        in_specs=[
            pl.BlockSpec((tm, d), lambda i, be, nu: (i, 0)),
            pl.BlockSpec((None, d, ff2), lambda i, be, nu: (be[i], 0, 0)),
            pl.BlockSpec((None, 1, ff2), lambda i, be, nu: (be[i], 0, 0)),
            pl.BlockSpec((None, ff, d), lambda i, be, nu: (be[i], 0, 0)),
            pl.BlockSpec((None, 1, d), lambda i, be, nu: (be[i], 0, 0)),
        ],
        out_specs=pl.BlockSpec((tm, d), lambda i, be, nu: (i, 0)),
        scratch_shapes=[pltpu.VMEM((d, ff2), BF16), pltpu.VMEM((ff, d), BF16)],
    )
    return pl.pallas_call(
        _moe_kernel,
        out_shape=jax.ShapeDtypeStruct((n_rows, d), BF16),
        grid_spec=grid_spec,
        compiler_params=pltpu.CompilerParams(dimension_semantics=("arbitrary",), vmem_limit_bytes=MOE_VMEM_LIMIT),
        name="expert_mlp",
    )(blk_exp, n_used, xs, w_gate_up, b_gate_up.reshape(n_exp, 1, ff2).astype(F32),
      w_down, b_down.reshape(n_exp, 1, d).astype(F32))


def _moe(u, route_idx, counts, w_gate_up, b_gate_up, w_down, b_down):
    n_tok, d = u.shape
    tm = MOE_ROW_BLOCK
    n_assign = n_tok * TOP_K
    n_blocks = n_assign // tm + N_EXPERTS
    n_rows = n_blocks * tm
    top_exp = route_idx[:, :TOP_K]
    rank = route_idx[:, TOP_K:2 * TOP_K]
    counts = counts[0, :N_EXPERTS].astype(jnp.int32)
    padded = (counts + tm - 1) // tm * tm
    start = jnp.cumsum(counts) - counts
    pad_end = jnp.cumsum(padded)
    pad_start = pad_end - padded
    blk_start = jnp.arange(n_blocks, dtype=jnp.int32) * tm
    blk_exp = jnp.minimum(jnp.sum(pad_end[None, :] <= blk_start[:, None], axis=1), N_EXPERTS - 1).astype(jnp.int32)
    n_used = (pad_end[-1:] // tm).astype(jnp.int32)
    order = jnp.argsort(top_exp.reshape(-1), stable=True).astype(jnp.int32)
    row_exp = jnp.repeat(blk_exp, tm)
    within = jnp.arange(n_rows, dtype=jnp.int32) - pad_start[row_exp]
    src = jnp.clip(start[row_exp] + within, 0, n_assign - 1)
    row_tok = jnp.where(within < counts[row_exp], order[src] // TOP_K, 0)
    xs = u[row_tok]
    y_rows = _expert_mlp(xs, blk_exp, n_used, w_gate_up, b_gate_up, w_down, b_down)
    pos = pad_start[top_exp] + rank
    return y_rows[pos.T.reshape(-1)].reshape(TOP_K, n_tok, d)


def kernel(x, c, ctx, c_ctx, ada_w, ada_b, ln_g, ln_b, gla_w_in, gla_gate_w1, gla_gate_w2, gla_gate_b, gla_norm_g, gla_w_out, swa_w_qkv, swa_b_qkv, swa_sink, swa_w_out, swa_b_out, conv_w_in, conv_w, conv_w_out, moe_router_w, moe_router_b, moe_w_gate_up, moe_b_gate_up, moe_w_down, moe_b_down):
    batch, seq, d = x.shape
    n_ctx = ctx.shape[1]
    depth = ada_w.shape[0]
    alpha = (2 * depth) ** 0.25
    rows_full = _Rows(batch, n_ctx // ROW_TILE, seq // ROW_TILE)
    rows_lat = _Rows(batch, 0, seq // ROW_TILE)

    cond = jnp.zeros((8, d), F32).at[0].set(c_ctx).at[1:1 + batch].set(c)
    ada = _ada_mods(cond, ada_w, ada_b)
    mods_ctx = jnp.broadcast_to(ada[:, 0:1], (depth, batch, N_MOD * d))
    mods = jnp.stack([mods_ctx, ada[:, 1:1 + batch]], axis=2).reshape(depth, batch, 2, N_MOD, d)

    router_w = jnp.zeros((depth, d, LANES), F32).at[:, :, :N_EXPERTS].set(moe_router_w)
    router_b = jnp.zeros((depth, 1, LANES), F32).at[:, 0, :N_EXPERTS].set(moe_router_b)

    h = jnp.concatenate([ctx, x], axis=1).reshape(rows_full.n_rows, d)
    u = _modulate(h, mods[0], rows_full, 0, 1)
    rows = rows_full
    for i in range(depth):
        last = i == depth - 1
        kind, j = i % 3, i // 3
        if kind == 0:
            y = _gla_mixer(u, rows, gla_w_in[j], gla_gate_w1[j], gla_gate_w2[j], gla_gate_b[j], gla_norm_g[j], gla_w_out[j])
        elif kind == 1:
            y = _window_mixer(u, rows, n_ctx, seq, swa_w_qkv[j], swa_b_qkv[j], swa_sink[j], swa_w_out[j], swa_b_out[j])
        else:
            y = _conv_mixer(u, rows, n_ctx, conv_w_in[j], conv_w[j], conv_w_out[j])
        rows_out = rows_lat if last else rows
        h, u, route_idx, route_gate, counts = _resid_ln(
            h, y, mods[i], mods[i], ln_g[i, 0], ln_b[i, 0], rows, rows_out, alpha=alpha, gate_idx=2,
            next_mod=(3, 4), router=(router_w[i], router_b[i]))
        rows = rows_out
        y = _moe(u, route_idx, counts, moe_w_gate_up[i], moe_b_gate_up[i], moe_w_down[i], moe_b_down[i])
        if last:
            (h,) = _resid_ln(h, y, mods[i], mods[i], ln_g[i, 1], ln_b[i, 1], rows, rows, alpha=alpha, gate_idx=5,
                             combine_gates=route_gate)
        else:
            h, u = _resid_ln(h, y, mods[i], mods[i + 1], ln_g[i, 1], ln_b[i, 1], rows, rows, alpha=alpha, gate_idx=5,
                             next_mod=(0, 1), combine_gates=route_gate)
    return h.reshape(batch, seq, d)
```

```python
import functools

import jax
import jax.numpy as jnp
from jax import lax
from jax.experimental import pallas as pl
from jax.experimental.pallas import tpu as pltpu

F32 = jnp.float32
BF16 = jnp.bfloat16
HIGHEST = lax.Precision.HIGHEST

GRID_W = 64
GLA_HEADS = 4
GLA_GATE_RANK = 16
GLA_GATE_NORM = 16.0
GLA_CHUNK = 64
SWA_HEAD_DIM = 64
SWA_GROUP = 8
WINDOW = 128
ROPE_BASE = 10000.0
CONV_WIDTH = 3
N_EXPERTS = 32
TOP_K = 4
SWIGLU_LIMIT = 7.0
SWIGLU_ALPHA = 1.702
N_MOD = 6
LN_EPS = 1e-5
NEG_INF = -1e30

LANES = 128
ROW_TILE = 256
MOE_ROW_BLOCK = 256
VMEM_LIMIT = 48 * 1024 * 1024
MOE_VMEM_LIMIT = 60 * 1024 * 1024


def _cparams(*sem):
    return pltpu.CompilerParams(dimension_semantics=sem, vmem_limit_bytes=VMEM_LIMIT)


def _sigmoid(x):
    return 1.0 / (1.0 + jnp.exp(-x))


def _dot(a, b):
    return jnp.dot(a, b, preferred_element_type=F32)


def _dot_nt(a, b):
    return lax.dot_general(a, b, (((1,), (1,)), ((), ())), preferred_element_type=F32)


def _split_bf16(x):
    hi = x.astype(BF16)
    lo = (x - hi.astype(F32)).astype(BF16)
    return hi, lo


class _Rows:
    def __init__(self, batch, nb_ctx, nb_lat, tile=ROW_TILE):
        self.batch, self.nb_ctx, self.nb_lat, self.tile = batch, nb_ctx, nb_lat, tile
        self.nb_seq = nb_ctx + nb_lat
        self.n_blocks = batch * self.nb_seq
        self.n_rows = self.n_blocks * tile

    def batch_of(self, i):
        return i // self.nb_seq

    def seg_of(self, i):
        if self.nb_ctx == 0:
            return 1
        return jnp.where(i % self.nb_seq >= self.nb_ctx, 1, 0)

    def latent_block(self, i):
        return (i // self.nb_lat) * self.nb_seq + self.nb_ctx + i % self.nb_lat


def _ada_kernel(c_ref, w_ref, b_ref, o_ref):
    c = c_ref[...]
    s = c * _sigmoid(c)
    o_ref[...] = jnp.dot(s, w_ref[...], precision=HIGHEST, preferred_element_type=F32) + b_ref[...]


def _ada_mods(cond, ada_w, ada_b):
    depth, d, n = ada_w.shape
    tn = 1024 if n % 1024 == 0 else n
    return pl.pallas_call(
        _ada_kernel,
        out_shape=jax.ShapeDtypeStruct((depth, 8, n), F32),
        grid=(depth, n // tn),
        in_specs=[
            pl.BlockSpec((8, d), lambda l, j: (0, 0)),
            pl.BlockSpec((None, d, tn), lambda l, j: (l, 0, j)),
            pl.BlockSpec((None, 1, tn), lambda l, j: (l, 0, j)),
        ],
        out_specs=pl.BlockSpec((None, 8, tn), lambda l, j: (l, 0, j)),
        compiler_params=_cparams("parallel", "parallel"),
        name="ada_mods",
    )(cond, ada_w, ada_b.reshape(depth, 1, n))


def _modulate_kernel(h_ref, m_ref, u_ref, *, shift_idx, scale_idx):
    h = h_ref[...]
    u = h * (1.0 + m_ref[scale_idx:scale_idx + 1, :]) + m_ref[shift_idx:shift_idx + 1, :]
    u_ref[...] = u.astype(u_ref.dtype)


def _modulate(h, mods, rows, shift_idx, scale_idx):
    d = h.shape[1]
    t = rows.tile
    return pl.pallas_call(
        functools.partial(_modulate_kernel, shift_idx=shift_idx, scale_idx=scale_idx),
        out_shape=jax.ShapeDtypeStruct(h.shape, BF16),
        grid=(rows.n_blocks,),
        in_specs=[
            pl.BlockSpec((t, d), lambda i: (i, 0)),
            pl.BlockSpec((None, None, N_MOD, d), lambda i: (rows.batch_of(i), rows.seg_of(i), 0, 0)),
        ],
        out_specs=pl.BlockSpec((t, d), lambda i: (i, 0)),
        compiler_params=_cparams("parallel"),
        name="modulate",
    )(h, mods)


def _mm_kernel(x_ref, w_ref, o_ref):
    o_ref[...] = _dot(x_ref[...], w_ref[...]).astype(o_ref.dtype)


def _mm_bias_kernel(x_ref, w_ref, b_ref, o_ref):
    o_ref[...] = (_dot(x_ref[...], w_ref[...]) + b_ref[...]).astype(o_ref.dtype)


def _pick_tile(n, candidates):
    for c in candidates:
        if n % c == 0:
            return c
    return n


def _matmul(x, w, bias=None, out_dtype=BF16, name="proj"):
    m, k = x.shape
    n = w.shape[1]
    tm = _pick_tile(m, (1024, 512, 256))
    tn = _pick_tile(n, (1024, 768, 640, 512, 256, 128))
    in_specs = [
        pl.BlockSpec((tm, k), lambda i, j: (i, 0)),
        pl.BlockSpec((k, tn), lambda i, j: (0, j)),
    ]
    args = [x, w]
    body = _mm_kernel
    if bias is not None:
        in_specs.append(pl.BlockSpec((1, tn), lambda i, j: (0, j)))
        args.append(bias.reshape(1, n).astype(F32))
        body = _mm_bias_kernel
    return pl.pallas_call(
        body,
        out_shape=jax.ShapeDtypeStruct((m, n), out_dtype),
        grid=(m // tm, n // tn),
        in_specs=in_specs,
        out_specs=pl.BlockSpec((tm, tn), lambda i, j: (i, j)),
        compiler_params=_cparams("parallel", "parallel"),
        name=name,
    )(*args)


def _route_block(logits, count_ref):
    t = logits.shape[0]
    lane = lax.broadcasted_iota(jnp.int32, (t, LANES), 1)
    lane_f = lane.astype(F32)
    left = jnp.where(lane < N_EXPERTS, logits, NEG_INF)
    hot = jnp.zeros((t, LANES), F32)
    sels, vals = [], []
    for _ in range(TOP_K):
        m = jnp.max(left, axis=-1, keepdims=True)
        sel = jnp.min(jnp.where(left == m, lane_f, float(LANES)), axis=-1, keepdims=True)
        pick = lane_f == sel
        left = jnp.where(pick, NEG_INF, left)
        hot = jnp.where(pick, 1.0, hot)
        sels.append(sel)
        vals.append(m)
    row = lax.broadcasted_iota(jnp.int32, (t, t), 0)
    col = lax.broadcasted_iota(jnp.int32, (t, t), 1)
    earlier = jnp.where(row > col, 1.0, 0.0).astype(BF16)
    before = _dot(earlier, hot.astype(BF16)) + count_ref[...]
    count_ref[...] += jnp.sum(hot, axis=0, keepdims=True)
    exps = [jnp.exp(v - vals[0]) for v in vals]
    denom = exps[0]
    for e in exps[1:]:
        denom = denom + e
    idx = jnp.zeros((t, LANES), jnp.int32)
    gates = jnp.zeros((t, LANES), F32)
    for k in range(TOP_K):
        rank = jnp.sum(jnp.where(lane_f == sels[k], before, 0.0), axis=-1, keepdims=True)
        idx = jnp.where(lane == k, sels[k].astype(jnp.int32), idx)
        idx = jnp.where(lane == TOP_K + k, rank.astype(jnp.int32), idx)
        gates = jnp.where(lane == k, exps[k] / denom, gates)
    return idx, gates


def _resid_ln_kernel(*refs, alpha, gate_idx, shift_idx, scale_idx, with_u, with_router, combine):
    h_ref, y_ref = refs[:2]
    pos = 2
    if combine:
        yg_ref = refs[pos]
        pos += 1
    m_ref, mn_ref, g_ref, b_ref = refs[pos:pos + 4]
    pos += 4
    if with_router:
        wr_hi_ref, wr_lo_ref, br_ref = refs[pos:pos + 3]
        pos += 3
    hn_ref = refs[pos]
    pos += 1
    if combine:
        gates = yg_ref[...]
        y = gates[:, 0:1] * y_ref[0].astype(F32)
        for k in range(1, TOP_K):
            y = y + gates[:, k:k + 1] * y_ref[k].astype(F32)
    else:
        y = y_ref[...].astype(F32)
    z = alpha * h_ref[...] + y * m_ref[gate_idx:gate_idx + 1, :]
    zc = z - jnp.mean(z, axis=-1, keepdims=True)
    var = jnp.mean(zc * zc, axis=-1, keepdims=True)
    hn = zc * lax.rsqrt(var + LN_EPS) * g_ref[...] + b_ref[...]
    hn_ref[...] = hn
    if with_u:
        u_ref = refs[pos]
        pos += 1
        u = hn * (1.0 + mn_ref[scale_idx:scale_idx + 1, :]) + mn_ref[shift_idx:shift_idx + 1, :]
        u_ref[...] = u.astype(u_ref.dtype)
        if with_router:
            idx_ref, gate_ref, cnt_ref, count_ref = refs[pos:pos + 4]

            @pl.when(pl.program_id(0) == 0)
            def _():
                count_ref[...] = jnp.zeros_like(count_ref)

            u_hi, u_lo = _split_bf16(u)
            w_hi = wr_hi_ref[...]
            logits = _dot(u_hi, w_hi) + _dot(u_lo, w_hi) + _dot(u_hi, wr_lo_ref[...]) + br_ref[...]
            idx, gates = _route_block(logits, count_ref)
            idx_ref[...] = idx
            gate_ref[...] = gates
            cnt_ref[...] = count_ref[...]


def _resid_ln(h, y, mods, mods_next, ln_g, ln_b, rows_in, rows_out, *, alpha, gate_idx, next_mod=None, router=None,
              combine_gates=None):
    d = h.shape[1]
    t = rows_out.tile
    if rows_in is rows_out:
        in_row = lambda i: i
    else:
        in_row = rows_in.latent_block
    with_u = next_mod is not None
    with_router = router is not None
    combine = combine_gates is not None
    shift_idx, scale_idx = next_mod if with_u else (0, 0)
    mod_spec = pl.BlockSpec((None, None, N_MOD, d), lambda i: (rows_out.batch_of(i), rows_out.seg_of(i), 0, 0))
    in_specs = [pl.BlockSpec((t, d), lambda i: (in_row(i), 0))]
    args = [h, y]
    if combine:
        in_specs += [pl.BlockSpec((TOP_K, t, d), lambda i: (0, in_row(i), 0)),
                     pl.BlockSpec((t, LANES), lambda i: (in_row(i), 0))]
        args.append(combine_gates)
    else:
        in_specs.append(pl.BlockSpec((t, d), lambda i: (in_row(i), 0)))
    in_specs += [mod_spec, mod_spec, pl.BlockSpec((1, d), lambda i: (0, 0)), pl.BlockSpec((1, d), lambda i: (0, 0))]
    args += [mods, mods_next, ln_g.reshape(1, d), ln_b.reshape(1, d)]
    out_shape = [jax.ShapeDtypeStruct((rows_out.n_rows, d), F32)]
    out_specs = [pl.BlockSpec((t, d), lambda i: (i, 0))]
    scratch = []
    if with_router:
        wr, br = router
        w_spec = pl.BlockSpec((d, LANES), lambda i: (0, 0))
        in_specs += [w_spec, w_spec, pl.BlockSpec((1, LANES), lambda i: (0, 0))]
        args += [*_split_bf16(wr), br]
    if with_u:
        out_shape.append(jax.ShapeDtypeStruct((rows_out.n_rows, d), F32 if with_router else BF16))
        out_specs.append(pl.BlockSpec((t, d), lambda i: (i, 0)))
    if with_router:
        out_shape += [jax.ShapeDtypeStruct((rows_out.n_rows, LANES), jnp.int32),
                      jax.ShapeDtypeStruct((rows_out.n_rows, LANES), F32),
                      jax.ShapeDtypeStruct((1, LANES), F32)]
        out_specs += [pl.BlockSpec((t, LANES), lambda i: (i, 0)), pl.BlockSpec((t, LANES), lambda i: (i, 0)),
                      pl.BlockSpec((1, LANES), lambda i: (0, 0))]
        scratch.append(pltpu.VMEM((1, LANES), F32))
    return pl.pallas_call(
        functools.partial(_resid_ln_kernel, alpha=alpha, gate_idx=gate_idx, shift_idx=shift_idx,
                          scale_idx=scale_idx, with_u=with_u, with_router=with_router, combine=combine),
        out_shape=out_shape,
        grid=(rows_out.n_blocks,),
        in_specs=in_specs,
        out_specs=out_specs,
        scratch_shapes=scratch,
        compiler_params=_cparams("arbitrary" if with_router else "parallel"),
        name="resid_ln",
    )(*args)


def _gla_kernel(*refs, reverse, fuse_post, q_scale, dk, dv):
    if fuse_post:
        q_ref, k_ref, v_ref, r_ref, w2_ref, gb_ref, other_ref, g_ref, ng_ref, out_ref, state_ref = refs
    else:
        q_ref, k_ref, v_ref, r_ref, w2_ref, gb_ref, out_ref, state_ref = refs

    @pl.when(pl.program_id(1) == 0)
    def _():
        state_ref[...] = jnp.zeros_like(state_ref)

    t = q_ref.shape[0]
    c = GLA_CHUNK
    shift = c.bit_length() - 1
    r_hi, r_lo = _split_bf16(r_ref[...])
    w_hi, w_lo = _split_bf16(w2_ref[...])
    z = _dot(r_hi, w_hi) + _dot(r_lo, w_hi) + _dot(r_hi, w_lo) + gb_ref[...]
    log_a = (jnp.minimum(z, 0.0) - jnp.log(1.0 + jnp.exp(-jnp.abs(z)))) * (1.0 / GLA_GATE_NORM)
    row = lax.broadcasted_iota(jnp.int32, (t, t), 0)
    col = lax.broadcasted_iota(jnp.int32, (t, t), 1)
    same_chunk = lax.shift_right_logical(row, shift) == lax.shift_right_logical(col, shift)
    ahead = (col - row) if reverse else (row - col)
    tri = jnp.where(jnp.where(same_chunk, ahead, -1) >= 0, 1.0, 0.0).astype(BF16)
    a_hi, a_lo = _split_bf16(log_a)
    cum = _dot(tri, a_hi) + _dot(tri, a_lo)
    grow = jnp.exp(cum)
    shrink = jnp.exp(-cum)
    crow = lax.broadcasted_iota(jnp.int32, (c, c), 0)
    ccol = lax.broadcasted_iota(jnp.int32, (c, c), 1)
    causal = (ccol >= crow) if reverse else (crow >= ccol)
    n_chunks = t // c
    for h in range(GLA_HEADS):
        ks = slice(h * dk, (h + 1) * dk)
        vs = slice(h * dv, (h + 1) * dv)
        q_dec = (q_ref[:, ks].astype(F32) * q_scale * grow[:, ks]).astype(BF16)
        k_in = k_ref[:, ks].astype(F32) * shrink[:, ks]
        for step in range(n_chunks):
            j = n_chunks - 1 - step if reverse else step
            rs = slice(j * c, (j + 1) * c)
            edge = j * c if reverse else (j + 1) * c - 1
            decay = jnp.exp(cum[edge:edge + 1, ks])
            qd = q_dec[rs]
            k_intra = k_in[rs].astype(BF16)
            k_state = (k_in[rs] * decay).astype(BF16)
            v = v_ref[rs, vs]
            scores = jnp.where(causal, _dot_nt(qd, k_intra), 0.0)
            state_t = state_ref[h]
            o = _dot(scores.astype(BF16), v) + _dot_nt(qd, state_t.astype(BF16))
            kv_t = lax.dot_general(v, k_state, (((0,), (0,)), ((), ())), preferred_element_type=F32)
            state_ref[h] = state_t * decay + kv_t
            if fuse_post:
                o = o + other_ref[rs, vs]
                o = o * lax.rsqrt(jnp.mean(o * o, axis=-1, keepdims=True) + LN_EPS) * ng_ref[...]
                g = g_ref[rs, vs].astype(F32)
                out_ref[rs, vs] = (o * (g * _sigmoid(g))).astype(out_ref.dtype)
            else:
                out_ref[rs, vs] = o


def _gla_scan(qkvg, r, w2, gate_b, rows, direction, other=None, norm_g=None):
    t_rows = qkvg.shape[0]
    dk_tot = w2.shape[2]
    dk = dk_tot // GLA_HEADS
    dv_tot = (qkvg.shape[1] - 2 * dk_tot) // 2
    dv = dv_tot // GLA_HEADS
    t = rows.tile
    reverse = direction == 1
    fuse_post = other is not None

    def block(b, n):
        if reverse:
            n = jnp.where(n < rows.nb_ctx, rows.nb_ctx - 1 - n, rows.nb_seq + rows.nb_ctx - 1 - n)
        return b * rows.nb_seq + n

    in_specs = [
        pl.BlockSpec((t, dk_tot), lambda b, n: (block(b, n), 0)),
        pl.BlockSpec((t, dk_tot), lambda b, n: (block(b, n), 1)),
        pl.BlockSpec((t, dv_tot), lambda b, n: (block(b, n), (2 * dk_tot) // dv_tot)),
        pl.BlockSpec((t, LANES), lambda b, n: (block(b, n), direction)),
        pl.BlockSpec((None, LANES, dk_tot), lambda b, n: (direction, 0, 0)),
        pl.BlockSpec((None, 1, dk_tot), lambda b, n: (direction, 0, 0)),
    ]
    args = [qkvg, qkvg, qkvg, r, w2, gate_b]
    if fuse_post:
        in_specs += [
            pl.BlockSpec((t, dv_tot), lambda b, n: (block(b, n), 0)),
            pl.BlockSpec((t, dv_tot), lambda b, n: (block(b, n), (2 * dk_tot + dv_tot) // dv_tot)),
            pl.BlockSpec((1, dv), lambda b, n: (0, 0)),
        ]
        args += [other, qkvg, norm_g.reshape(1, dv).astype(F32)]
    return pl.pallas_call(
        functools.partial(_gla_kernel, reverse=reverse, fuse_post=fuse_post, q_scale=float(dk) ** -0.5, dk=dk, dv=dv),
        out_shape=jax.ShapeDtypeStruct((t_rows, dv_tot), BF16 if fuse_post else F32),
        grid=(rows.batch, rows.nb_seq),
        in_specs=in_specs,
        out_specs=pl.BlockSpec((t, dv_tot), lambda b, n: (block(b, n), 0)),
        scratch_shapes=[pltpu.VMEM((GLA_HEADS, dv, dk), F32)],
        compiler_params=_cparams("parallel", "arbitrary"),
        name="gla_scan_bwd" if reverse else "gla_scan_fwd",
    )(*args)


def _gla_mixer(u, rows, w_in, gate_w1, gate_w2, gate_b, norm_g, w_out):
    d = u.shape[1]
    dk_tot = gate_w2.shape[2]
    qkvg = _matmul(u, w_in.astype(BF16), name="gla_in")
    w1 = jnp.zeros((d, 2 * LANES), F32)
    w2 = jnp.zeros((2, LANES, dk_tot), F32)
    for dd in range(2):
        w1 = w1.at[:, dd * LANES:dd * LANES + GLA_GATE_RANK].set(gate_w1[dd])
        w2 = w2.at[dd, :GLA_GATE_RANK].set(gate_w2[dd])
    r = _matmul(u, w1.astype(BF16), out_dtype=F32, name="gla_gate_in")
    gate_b = gate_b.reshape(2, 1, dk_tot).astype(F32)
    o_fwd = _gla_scan(qkvg, r, w2, gate_b, rows, 0)
    y = _gla_scan(qkvg, r, w2, gate_b, rows, 1, other=o_fwd, norm_g=norm_g)
    return _matmul(y, w_out.astype(BF16), name="gla_out")


def _rope_kernel(x_ref, cos_ref, sin_ref, q_ref, k2_ref, v2_ref, *, q_width, kv_width):
    cos = cos_ref[...]
    sin = sin_ref[...]
    lane = lax.broadcasted_iota(jnp.int32, cos.shape, 1)
    first = (lane % 32) < 16
    low = lane < SWA_HEAD_DIM

    def rope(slab):
        partner = jnp.where(first, pltpu.roll(slab, LANES - 16, 1), pltpu.roll(slab, 16, 1))
        return slab * cos + partner * sin

    scale = SWA_HEAD_DIM ** -0.5
    for s in range(q_width // LANES):
        slab = x_ref[:, s * LANES:(s + 1) * LANES].astype(F32)
        q_ref[:, s * LANES:(s + 1) * LANES] = (rope(slab) * scale).astype(q_ref.dtype)
    for s in range(kv_width // LANES):
        k_slab = rope(x_ref[:, q_width + s * LANES:q_width + (s + 1) * LANES].astype(F32))
        v_slab = x_ref[:, q_width + kv_width + s * LANES:q_width + kv_width + (s + 1) * LANES].astype(F32)
        for slab, out_ref in ((k_slab, k2_ref), (v_slab, v2_ref)):
            swapped = pltpu.roll(slab, SWA_HEAD_DIM, 1)
            out_ref[2 * s] = jnp.where(low, slab, swapped).astype(out_ref.dtype)
            out_ref[2 * s + 1] = jnp.where(low, swapped, slab).astype(out_ref.dtype)


def _rope_tables(n_ctx, seq):
    t = jnp.arange(seq, dtype=jnp.int32)
    row = (t // GRID_W).astype(F32)
    col = (t % GRID_W).astype(F32)
    n_freq = SWA_HEAD_DIM // 4
    inv_freq = ROPE_BASE ** (-jnp.arange(n_freq, dtype=F32) / n_freq)
    lane = jnp.arange(LANES)
    within = lane % SWA_HEAD_DIM
    pos = jnp.where((within < SWA_HEAD_DIM // 2)[None, :], row[:, None], col[:, None])
    ang = pos * inv_freq[lane % n_freq][None, :]
    sign = jnp.where((lane % 32) < 16, -1.0, 1.0)[None, :]
    cos = jnp.concatenate([jnp.ones((n_ctx, LANES), F32), jnp.cos(ang)], axis=0)
    sin = jnp.concatenate([jnp.zeros((n_ctx, LANES), F32), jnp.sin(ang) * sign], axis=0)
    return cos, sin


def _rope_split(qkv, cos, sin, rows, q_width, kv_width):
    t_rows = qkv.shape[0]
    t = rows.tile
    n_kv = kv_width // SWA_HEAD_DIM
    seq_rows = rows.nb_seq * t
    kv_shape = jax.ShapeDtypeStruct((rows.batch, n_kv, seq_rows, LANES), BF16)
    kv_spec = pl.BlockSpec((None, n_kv, t, LANES), lambda i: (i // rows.nb_seq, 0, i % rows.nb_seq, 0))
    return pl.pallas_call(
        functools.partial(_rope_kernel, q_width=q_width, kv_width=kv_width),
        out_shape=[jax.ShapeDtypeStruct((t_rows, q_width), BF16), kv_shape, kv_shape],
        grid=(rows.n_blocks,),
        in_specs=[
            pl.BlockSpec((t, qkv.shape[1]), lambda i: (i, 0)),
            pl.BlockSpec((t, LANES), lambda i: (i % rows.nb_seq, 0)),
            pl.BlockSpec((t, LANES), lambda i: (i % rows.nb_seq, 0)),
        ],
        out_specs=[pl.BlockSpec((t, q_width), lambda i: (i, 0)), kv_spec, kv_spec],
        compiler_params=_cparams("parallel"),
        name="rope_split",
    )(qkv, cos, sin)


def _attend_heads(q_ref, kk, vv, mask, sink_ref, head0, o_ref):
    tq = q_ref.shape[0]
    lane = lax.broadcasted_iota(jnp.int32, (tq, LANES), 1)
    low = lane < SWA_HEAD_DIM
    for pair in range(SWA_GROUP // 2):
        q2 = q_ref[:, pair * LANES:(pair + 1) * LANES]
        outs = []
        for half in range(2):
            qh = jnp.where(low if half == 0 else lane >= SWA_HEAD_DIM, q2, jnp.zeros_like(q2))
            s = _dot_nt(qh, kk)
            if mask is not None:
                s = jnp.where(mask, s, NEG_INF)
            sink = sink_ref[head0 + 2 * pair + half]
            m = jnp.maximum(jnp.max(s, axis=-1, keepdims=True), sink)
            p = jnp.exp(s - m)
            denom = jnp.sum(p, axis=-1, keepdims=True) + jnp.exp(sink - m)
            outs.append(_dot(p.astype(BF16), vv) / denom)
        o_ref[:, pair * LANES:(pair + 1) * LANES] = jnp.where(low, outs[0], outs[1]).astype(o_ref.dtype)


def _attn_kernel(sink_ref, q_ref, kc_ref, vc_ref, kp_ref, vp_ref, km_ref, vm_ref, kn_ref, vn_ref,
                 o_ref, kk_ref, vv_ref, *, n_ctx, nq_ctx, nq_lat):
    n = pl.program_id(2)
    head0 = pl.program_id(1) * SWA_GROUP
    w = WINDOW

    @pl.when(n < nq_ctx)
    def _():
        _attend_heads(q_ref, kc_ref[...], vc_ref[...], None, sink_ref, head0, o_ref)

    @pl.when(n >= nq_ctx)
    def _():
        m = n - nq_ctx
        for ref, parts in ((kk_ref, (kc_ref, kp_ref, km_ref, kn_ref)), (vv_ref, (vc_ref, vp_ref, vm_ref, vn_ref))):
            ref[0:n_ctx] = parts[0][...]
            ref[n_ctx:n_ctx + w] = parts[1][...]
            ref[n_ctx + w:n_ctx + 2 * w] = parts[2][...]
            ref[n_ctx + 2 * w:n_ctx + 3 * w] = parts[3][...]
        nk = n_ctx + 3 * w
        qpos = lax.broadcasted_iota(jnp.int32, (w, nk), 0)
        col = lax.broadcasted_iota(jnp.int32, (w, nk), 1)
        rel = jnp.where(col < n_ctx, 0, col - (n_ctx + w) - qpos)
        far = 4 * w
        no_prev = jnp.where(m > 0, 0, far)
        no_next = jnp.where(m < nq_lat - 1, 0, far)
        missing = jnp.where(col < n_ctx, 0,
                            jnp.where(col < n_ctx + w, no_prev, jnp.where(col >= n_ctx + 2 * w, no_next, 0)))
        mask = jnp.abs(rel) + missing <= w
        _attend_heads(q_ref, kk_ref[...], vv_ref[...], mask, sink_ref, head0, o_ref)


def _window_attention(q, k2, v2, sink, rows, n_ctx):
    t_rows, q_width = q.shape
    batch, n_kv, seq_rows, _ = k2.shape
    w = WINDOW
    gw = SWA_GROUP * SWA_HEAD_DIM
    nq_seq = seq_rows // w
    nq_ctx = n_ctx // w
    nq_lat = nq_seq - nq_ctx
    ctx_kv = pl.BlockSpec((None, None, n_ctx, LANES), lambda b, g, n: (b, g, 0, 0))

    def band(shift):
        def index(b, g, n):
            return (b, g, nq_ctx + jnp.clip(n - nq_ctx + shift, 0, nq_lat - 1), 0)
        return pl.BlockSpec((None, None, w, LANES), index)

    nk = n_ctx + 3 * w
    return pl.pallas_call(
        functools.partial(_attn_kernel, n_ctx=n_ctx, nq_ctx=nq_ctx, nq_lat=nq_lat),
        out_shape=jax.ShapeDtypeStruct((t_rows, q_width), BF16),
        grid=(batch, n_kv, nq_seq),
        in_specs=[pl.BlockSpec(memory_space=pltpu.SMEM),
                  pl.BlockSpec((w, gw), lambda b, g, n: (b * nq_seq + n, g)),
                  ctx_kv, ctx_kv, band(-1), band(-1), band(0), band(0), band(1), band(1)],
        out_specs=pl.BlockSpec((w, gw), lambda b, g, n: (b * nq_seq + n, g)),
        scratch_shapes=[pltpu.VMEM((nk, LANES), BF16), pltpu.VMEM((nk, LANES), BF16)],
        compiler_params=_cparams("parallel", "parallel", "parallel"),
        name="window_attn",
    )(sink, q, k2, v2, k2, v2, k2, v2, k2, v2)


def _window_mixer(u, rows, n_ctx, seq, w_qkv, b_qkv, sink, w_out, b_out):
    q_width = w_out.shape[0]
    kv_width = (w_qkv.shape[1] - q_width) // 2
    qkv = _matmul(u, w_qkv.astype(BF16), bias=b_qkv, name="swa_qkv")
    cos, sin = _rope_tables(n_ctx, seq)
    q, k2, v2 = _rope_split(qkv, cos, sin, rows, q_width, kv_width)
    o = _window_attention(q, k2, v2, sink.astype(F32), rows, n_ctx)
    return _matmul(o, w_out.astype(BF16), bias=b_out, name="swa_out")


def _conv_kernel(gi_ref, go_ref, val_ref, w_ref, o_ref, p_ref, *, n_ctx):
    length = gi_ref.shape[0]
    pad = 8
    p_ref[0:pad] = jnp.zeros((pad, p_ref.shape[1]), F32)
    p_ref[pad + length:pad + length + pad] = jnp.zeros((pad, p_ref.shape[1]), F32)
    p_ref[pad:pad + length] = gi_ref[...].astype(F32) * val_ref[...].astype(F32)
    t = lax.broadcasted_iota(jnp.int32, (length, 1), 0)
    seg_start = t * (t - n_ctx) == 0
    seg_end = (t - (n_ctx - 1)) * (t - (length - 1)) == 0
    prev = jnp.where(seg_start, 0.0, p_ref[pad - 1:pad - 1 + length])
    nxt = jnp.where(seg_end, 0.0, p_ref[pad + 1:pad + 1 + length])
    z = prev * w_ref[0:1, :] + p_ref[pad:pad + length] * w_ref[1:2, :] + nxt * w_ref[2:3, :]
    o_ref[...] = (go_ref[...].astype(F32) * z).astype(o_ref.dtype)


def _conv_gate(proj, conv_w, rows, n_ctx):
    t_rows = proj.shape[0]
    d = proj.shape[1] // 3
    length = rows.nb_seq * rows.tile
    tc = LANES
    nc = d // tc
    return pl.pallas_call(
        functools.partial(_conv_kernel, n_ctx=n_ctx),
        out_shape=jax.ShapeDtypeStruct((t_rows, d), BF16),
        grid=(rows.batch, nc),
        in_specs=[
            pl.BlockSpec((length, tc), lambda b, c: (b, c)),
            pl.BlockSpec((length, tc), lambda b, c: (b, nc + c)),
            pl.BlockSpec((length, tc), lambda b, c: (b, 2 * nc + c)),
            pl.BlockSpec((CONV_WIDTH, tc), lambda b, c: (0, c)),
        ],
        out_specs=pl.BlockSpec((length, tc), lambda b, c: (b, c)),
        scratch_shapes=[pltpu.VMEM((length + 16, tc), F32)],
        compiler_params=_cparams("parallel", "parallel"),
        name="conv_gate",
    )(proj, proj, proj, conv_w.astype(F32))


def _conv_mixer(u, rows, n_ctx, w_in, w_conv, w_out):
    proj = _matmul(u, w_in.astype(BF16), name="conv_in")
    zg = _conv_gate(proj, w_conv, rows, n_ctx)
    return _matmul(zg, w_out.astype(BF16), name="conv_out")


def _row_gather_copy(u_hbm, x_buf, sem, slot, tok, r):
    return pltpu.make_async_copy(u_hbm.at[pl.ds(tok, 1)], x_buf.at[slot, pl.ds(r, 1)], sem.at[slot])


def _moe_kernel(blk_exp_ref, n_used_ref, row_tok_ref, u_hbm, wgu_ref, bgu_ref, wdn_ref, bdn_ref, o_ref,
                x_buf, sem, wgu_bf_ref, wdn_bf_ref):
    i = pl.program_id(0)
    tm = o_ref.shape[0]
    n_used = n_used_ref[0]
    used = i < n_used
    slot = lax.rem(i, 2)

    def start_gather(block, to_slot):
        def body(r, carry):
            _row_gather_copy(u_hbm, x_buf, sem, to_slot, row_tok_ref[block * tm + r], r).start()
            return carry
        lax.fori_loop(0, tm, body, 0, unroll=8)

    @pl.when(used & (i == 0))
    def _():
        start_gather(0, 0)

    @pl.when(i + 1 < n_used)
    def _():
        start_gather(i + 1, 1 - slot)

    new_expert = (i == 0) | (blk_exp_ref[i] != blk_exp_ref[jnp.maximum(i - 1, 0)])

    @pl.when(used & new_expert)
    def _():
        wgu_bf_ref[...] = wgu_ref[...].astype(BF16)
        wdn_bf_ref[...] = wdn_ref[...].astype(BF16)

    @pl.when(used)
    def _():
        pltpu.make_async_copy(u_hbm.at[pl.ds(0, tm)], x_buf.at[slot], sem.at[slot]).wait()
        ff = wdn_ref.shape[0]
        gu = _dot(x_buf[slot].astype(BF16), wgu_bf_ref[...]) + bgu_ref[...]
        glu = jnp.minimum(gu[:, :ff], SWIGLU_LIMIT)
        lin = jnp.clip(gu[:, ff:], -SWIGLU_LIMIT, SWIGLU_LIMIT)
        act = glu * _sigmoid(SWIGLU_ALPHA * glu) * (lin + 1.0)
        o_ref[...] = (_dot(act.astype(BF16), wdn_bf_ref[...]) + bdn_ref[...]).astype(o_ref.dtype)

    @pl.when(i >= n_used)
    def _():
        o_ref[...] = jnp.zeros_like(o_ref)


def _expert_mlp(u, row_tok, blk_exp, n_used, layer, w_gate_up, b_gate_up, w_down, b_down):
    d = u.shape[1]
    n_rows = row_tok.shape[0]
    depth, n_exp, _, ff2 = w_gate_up.shape
    ff = ff2 // 2
    tm = MOE_ROW_BLOCK
    grid_spec = pltpu.PrefetchScalarGridSpec(
        num_scalar_prefetch=3,
        grid=(n_rows // tm,),
        in_specs=[
            pl.BlockSpec(memory_space=pl.ANY),
            pl.BlockSpec((None, None, d, ff2), lambda i, be, nu, rt: (layer, be[i], 0, 0)),
            pl.BlockSpec((None, None, 1, ff2), lambda i, be, nu, rt: (layer, be[i], 0, 0)),
            pl.BlockSpec((None, None, ff, d), lambda i, be, nu, rt: (layer, be[i], 0, 0)),
            pl.BlockSpec((None, None, 1, d), lambda i, be, nu, rt: (layer, be[i], 0, 0)),
        ],
        out_specs=pl.BlockSpec((tm, d), lambda i, be, nu, rt: (i, 0)),
        scratch_shapes=[pltpu.VMEM((2, tm, d), F32), pltpu.SemaphoreType.DMA((2,)),
                        pltpu.VMEM((d, ff2), BF16), pltpu.VMEM((ff, d), BF16)],
    )
    return pl.pallas_call(
        _moe_kernel,
        out_shape=jax.ShapeDtypeStruct((n_rows, d), BF16),
        grid_spec=grid_spec,
        compiler_params=pltpu.CompilerParams(dimension_semantics=("arbitrary",), vmem_limit_bytes=MOE_VMEM_LIMIT),
        name="expert_mlp",
    )(blk_exp, n_used, row_tok, u, w_gate_up, b_gate_up.reshape(depth, n_exp, 1, ff2),
      w_down, b_down.reshape(depth, n_exp, 1, d))


def _moe(u, route_idx, counts, layer, w_gate_up, b_gate_up, w_down, b_down):
    n_tok, d = u.shape
    tm = MOE_ROW_BLOCK
    n_assign = n_tok * TOP_K
    n_blocks = n_assign // tm + N_EXPERTS
    n_rows = n_blocks * tm
    top_exp = route_idx[:, :TOP_K]
    rank = route_idx[:, TOP_K:2 * TOP_K]
    counts = counts[0, :N_EXPERTS].astype(jnp.int32)
    padded = (counts + tm - 1) // tm * tm
    start = jnp.cumsum(counts) - counts
    pad_end = jnp.cumsum(padded)
    pad_start = pad_end - padded
    blk_start = jnp.arange(n_blocks, dtype=jnp.int32) * tm
    blk_exp = jnp.minimum(jnp.sum(pad_end[None, :] <= blk_start[:, None], axis=1), N_EXPERTS - 1).astype(jnp.int32)
    n_used = (pad_end[-1:] // tm).astype(jnp.int32)
    order = jnp.argsort(top_exp.reshape(-1), stable=True).astype(jnp.int32)
    row_exp = jnp.repeat(blk_exp, tm)
    within = jnp.arange(n_rows, dtype=jnp.int32) - pad_start[row_exp]
    src = jnp.clip(start[row_exp] + within, 0, n_assign - 1)
    row_tok = jnp.where(within < counts[row_exp], order[src] // TOP_K, 0)
    y_rows = _expert_mlp(u, row_tok, blk_exp, n_used, layer, w_gate_up, b_gate_up, w_down, b_down)
    pos = pad_start[top_exp] + rank
    return y_rows[pos.T.reshape(-1)].reshape(TOP_K, n_tok, d)


def kernel(x, c, ctx, c_ctx, ada_w, ada_b, ln_g, ln_b, gla_w_in, gla_gate_w1, gla_gate_w2, gla_gate_b, gla_norm_g, gla_w_out, swa_w_qkv, swa_b_qkv, swa_sink, swa_w_out, swa_b_out, conv_w_in, conv_w, conv_w_out, moe_router_w, moe_router_b, moe_w_gate_up, moe_b_gate_up, moe_w_down, moe_b_down):
    batch, seq, d = x.shape
    n_ctx = ctx.shape[1]
    depth = ada_w.shape[0]
    alpha = (2 * depth) ** 0.25
    rows_full = _Rows(batch, n_ctx // ROW_TILE, seq // ROW_TILE)
    rows_lat = _Rows(batch, 0, seq // ROW_TILE)

    cond = jnp.zeros((8, d), F32).at[0].set(c_ctx).at[1:1 + batch].set(c)
    ada = _ada_mods(cond, ada_w, ada_b)
    mods_ctx = jnp.broadcast_to(ada[:, 0:1], (depth, batch, N_MOD * d))
    mods = jnp.stack([mods_ctx, ada[:, 1:1 + batch]], axis=2).reshape(depth, batch, 2, N_MOD, d)

    router_w = jnp.zeros((depth, d, LANES), F32).at[:, :, :N_EXPERTS].set(moe_router_w)
    router_b = jnp.zeros((depth, 1, LANES), F32).at[:, 0, :N_EXPERTS].set(moe_router_b)

    h = jnp.concatenate([ctx, x], axis=1).reshape(rows_full.n_rows, d)
    u = _modulate(h, mods[0], rows_full, 0, 1)
    rows = rows_full
    for i in range(depth):
        last = i == depth - 1
        kind, j = i % 3, i // 3
        if kind == 0:
            y = _gla_mixer(u, rows, gla_w_in[j], gla_gate_w1[j], gla_gate_w2[j], gla_gate_b[j], gla_norm_g[j], gla_w_out[j])
        elif kind == 1:
            y = _window_mixer(u, rows, n_ctx, seq, swa_w_qkv[j], swa_b_qkv[j], swa_sink[j], swa_w_out[j], swa_b_out[j])
        else:
            y = _conv_mixer(u, rows, n_ctx, conv_w_in[j], conv_w[j], conv_w_out[j])
        rows_out = rows_lat if last else rows
        h, u, route_idx, route_gate, counts = _resid_ln(
            h, y, mods[i], mods[i], ln_g[i, 0], ln_b[i, 0], rows, rows_out, alpha=alpha, gate_idx=2,
            next_mod=(3, 4), router=(router_w[i], router_b[i]))
        rows = rows_out
        y = _moe(u, route_idx, counts, i, moe_w_gate_up, moe_b_gate_up, moe_w_down, moe_b_down)
        if last:
            (h,) = _resid_ln(h, y, mods[i], mods[i], ln_g[i, 1], ln_b[i, 1], rows, rows, alpha=alpha, gate_idx=5,
                             combine_gates=route_gate)
        else:
            h, u = _resid_ln(h, y, mods[i], mods[i + 1], ln_g[i, 1], ln_b[i, 1], rows, rows, alpha=alpha, gate_idx=5,
                             next_mod=(0, 1), combine_gates=route_gate)
    return h.reshape(batch, seq, d)
```

```python
import functools

import jax
import jax.numpy as jnp
from jax import lax
from jax.experimental import pallas as pl
from jax.experimental.pallas import tpu as pltpu

F32 = jnp.float32
BF16 = jnp.bfloat16
HIGHEST = lax.Precision.HIGHEST

GRID_W = 64
GLA_HEADS = 4
GLA_GATE_RANK = 16
GLA_GATE_NORM = 16.0
GLA_CHUNK = 64
SWA_HEAD_DIM = 64
SWA_GROUP = 8
WINDOW = 128
ROPE_BASE = 10000.0
CONV_WIDTH = 3
N_EXPERTS = 32
TOP_K = 4
SWIGLU_LIMIT = 7.0
SWIGLU_ALPHA = 1.702
N_MOD = 6
LN_EPS = 1e-5
NEG_INF = -1e30

LANES = 128
ROW_TILE = 256
MOE_ROW_BLOCK = 256
VMEM_LIMIT = 48 * 1024 * 1024
MOE_VMEM_LIMIT = 60 * 1024 * 1024


def _cparams(*sem):
    return pltpu.CompilerParams(dimension_semantics=sem, vmem_limit_bytes=VMEM_LIMIT)


def _sigmoid(x):
    return 1.0 / (1.0 + jnp.exp(-x))


def _dot(a, b):
    return jnp.dot(a, b, preferred_element_type=F32)


def _dot_nt(a, b):
    return lax.dot_general(a, b, (((1,), (1,)), ((), ())), preferred_element_type=F32)


def _split_bf16(x):
    hi = x.astype(BF16)
    lo = (x - hi.astype(F32)).astype(BF16)
    return hi, lo


class _Rows:
    def __init__(self, batch, nb_ctx, nb_lat, tile=ROW_TILE):
        self.batch, self.nb_ctx, self.nb_lat, self.tile = batch, nb_ctx, nb_lat, tile
        self.nb_seq = nb_ctx + nb_lat
        self.n_blocks = batch * self.nb_seq
        self.n_rows = self.n_blocks * tile

    def batch_of(self, i):
        return i // self.nb_seq

    def seg_of(self, i):
        if self.nb_ctx == 0:
            return 1
        return jnp.where(i % self.nb_seq >= self.nb_ctx, 1, 0)

    def latent_block(self, i):
        return (i // self.nb_lat) * self.nb_seq + self.nb_ctx + i % self.nb_lat


def _ada_kernel(c_ref, w_ref, b_ref, o_ref):
    c = c_ref[...]
    s = c * _sigmoid(c)
    o_ref[...] = jnp.dot(s, w_ref[...], precision=HIGHEST, preferred_element_type=F32) + b_ref[...]


def _ada_mods(cond, ada_w, ada_b):
    depth, d, n = ada_w.shape
    tn = 1024 if n % 1024 == 0 else n
    return pl.pallas_call(
        _ada_kernel,
        out_shape=jax.ShapeDtypeStruct((depth, 8, n), F32),
        grid=(depth, n // tn),
        in_specs=[
            pl.BlockSpec((8, d), lambda l, j: (0, 0)),
            pl.BlockSpec((None, d, tn), lambda l, j: (l, 0, j)),
            pl.BlockSpec((None, 1, tn), lambda l, j: (l, 0, j)),
        ],
        out_specs=pl.BlockSpec((None, 8, tn), lambda l, j: (l, 0, j)),
        compiler_params=_cparams("parallel", "parallel"),
        name="ada_mods",
    )(cond, ada_w, ada_b.reshape(depth, 1, n))


def _modulate_kernel(h_ref, m_ref, u_ref, *, shift_idx, scale_idx):
    h = h_ref[...]
    u = h * (1.0 + m_ref[scale_idx:scale_idx + 1, :]) + m_ref[shift_idx:shift_idx + 1, :]
    u_ref[...] = u.astype(u_ref.dtype)


def _modulate(h, mods, rows, shift_idx, scale_idx):
    d = h.shape[1]
    t = rows.tile
    return pl.pallas_call(
        functools.partial(_modulate_kernel, shift_idx=shift_idx, scale_idx=scale_idx),
        out_shape=jax.ShapeDtypeStruct(h.shape, BF16),
        grid=(rows.n_blocks,),
        in_specs=[
            pl.BlockSpec((t, d), lambda i: (i, 0)),
            pl.BlockSpec((None, None, N_MOD, d), lambda i: (rows.batch_of(i), rows.seg_of(i), 0, 0)),
        ],
        out_specs=pl.BlockSpec((t, d), lambda i: (i, 0)),
        compiler_params=_cparams("parallel"),
        name="modulate",
    )(h, mods)


def _mm_kernel(x_ref, w_ref, o_ref):
    o_ref[...] = _dot(x_ref[...], w_ref[...]).astype(o_ref.dtype)


def _mm_bias_kernel(x_ref, w_ref, b_ref, o_ref):
    o_ref[...] = (_dot(x_ref[...], w_ref[...]) + b_ref[...]).astype(o_ref.dtype)


def _pick_tile(n, candidates):
    for c in candidates:
        if n % c == 0:
            return c
    return n


def _matmul(x, w, bias=None, out_dtype=BF16, name="proj"):
    m, k = x.shape
    n = w.shape[1]
    tm = _pick_tile(m, (1024, 512, 256))
    tn = _pick_tile(n, (1024, 768, 640, 512, 256, 128))
    in_specs = [
        pl.BlockSpec((tm, k), lambda i, j: (i, 0)),
        pl.BlockSpec((k, tn), lambda i, j: (0, j)),
    ]
    args = [x, w]
    body = _mm_kernel
    if bias is not None:
        in_specs.append(pl.BlockSpec((1, tn), lambda i, j: (0, j)))
        args.append(bias.reshape(1, n).astype(F32))
        body = _mm_bias_kernel
    return pl.pallas_call(
        body,
        out_shape=jax.ShapeDtypeStruct((m, n), out_dtype),
        grid=(m // tm, n // tn),
        in_specs=in_specs,
        out_specs=pl.BlockSpec((tm, tn), lambda i, j: (i, j)),
        compiler_params=_cparams("parallel", "parallel"),
        name=name,
    )(*args)


def _route_block(logits, count_ref):
    t = logits.shape[0]
    lane = lax.broadcasted_iota(jnp.int32, (t, LANES), 1)
    lane_f = lane.astype(F32)
    left = jnp.where(lane < N_EXPERTS, logits, NEG_INF)
    hot = jnp.zeros((t, LANES), F32)
    sels, vals = [], []
    for _ in range(TOP_K):
        m = jnp.max(left, axis=-1, keepdims=True)
        sel = jnp.min(jnp.where(left == m, lane_f, float(LANES)), axis=-1, keepdims=True)
        pick = lane_f == sel
        left = jnp.where(pick, NEG_INF, left)
        hot = jnp.where(pick, 1.0, hot)
        sels.append(sel)
        vals.append(m)
    row = lax.broadcasted_iota(jnp.int32, (t, t), 0)
    col = lax.broadcasted_iota(jnp.int32, (t, t), 1)
    earlier = jnp.where(row > col, 1.0, 0.0).astype(BF16)
    before = _dot(earlier, hot.astype(BF16)) + count_ref[...]
    count_ref[...] += jnp.sum(hot, axis=0, keepdims=True)
    exps = [jnp.exp(v - vals[0]) for v in vals]
    denom = exps[0]
    for e in exps[1:]:
        denom = denom + e
    idx = jnp.zeros((t, LANES), jnp.int32)
    gates = jnp.zeros((t, LANES), F32)
    for k in range(TOP_K):
        rank = jnp.sum(jnp.where(lane_f == sels[k], before, 0.0), axis=-1, keepdims=True)
        idx = jnp.where(lane == k, sels[k].astype(jnp.int32), idx)
        idx = jnp.where(lane == TOP_K + k, rank.astype(jnp.int32), idx)
        gates = jnp.where(lane == k, exps[k] / denom, gates)
    return idx, gates


def _resid_ln_kernel(*refs, alpha, gate_idx, shift_idx, scale_idx, with_u, with_router, combine):
    h_ref, y_ref = refs[:2]
    pos = 2
    if combine:
        yg_ref = refs[pos]
        pos += 1
    m_ref, mn_ref, g_ref, b_ref = refs[pos:pos + 4]
    pos += 4
    if with_router:
        wr_hi_ref, wr_lo_ref, br_ref = refs[pos:pos + 3]
        pos += 3
    hn_ref = refs[pos]
    pos += 1
    if combine:
        gates = yg_ref[...]
        y = gates[:, 0:1] * y_ref[0].astype(F32)
        for k in range(1, TOP_K):
            y = y + gates[:, k:k + 1] * y_ref[k].astype(F32)
    else:
        y = y_ref[...].astype(F32)
    z = alpha * h_ref[...] + y * m_ref[gate_idx:gate_idx + 1, :]
    zc = z - jnp.mean(z, axis=-1, keepdims=True)
    var = jnp.mean(zc * zc, axis=-1, keepdims=True)
    hn = zc * lax.rsqrt(var + LN_EPS) * g_ref[...] + b_ref[...]
    hn_ref[...] = hn
    if with_u:
        u_ref = refs[pos]
        pos += 1
        u = hn * (1.0 + mn_ref[scale_idx:scale_idx + 1, :]) + mn_ref[shift_idx:shift_idx + 1, :]
        u_ref[...] = u.astype(u_ref.dtype)
        if with_router:
            idx_ref, gate_ref, cnt_ref, count_ref = refs[pos:pos + 4]

            @pl.when(pl.program_id(0) == 0)
            def _():
                count_ref[...] = jnp.zeros_like(count_ref)

            u_hi, u_lo = _split_bf16(u)
            w_hi = wr_hi_ref[...]
            logits = _dot(u_hi, w_hi) + _dot(u_lo, w_hi) + _dot(u_hi, wr_lo_ref[...]) + br_ref[...]
            idx, gates = _route_block(logits, count_ref)
            idx_ref[...] = idx
            gate_ref[...] = gates
            cnt_ref[...] = count_ref[...]


def _resid_ln(h, y, mods, mods_next, ln_g, ln_b, rows_in, rows_out, *, alpha, gate_idx, next_mod=None, router=None,
              combine_gates=None):
    d = h.shape[1]
    t = rows_out.tile
    if rows_in is rows_out:
        in_row = lambda i: i
    else:
        in_row = rows_in.latent_block
    with_u = next_mod is not None
    with_router = router is not None
    combine = combine_gates is not None
    shift_idx, scale_idx = next_mod if with_u else (0, 0)
    mod_spec = pl.BlockSpec((None, None, N_MOD, d), lambda i: (rows_out.batch_of(i), rows_out.seg_of(i), 0, 0))
    in_specs = [pl.BlockSpec((t, d), lambda i: (in_row(i), 0))]
    args = [h, y]
    if combine:
        in_specs += [pl.BlockSpec((TOP_K, t, d), lambda i: (0, in_row(i), 0)),
                     pl.BlockSpec((t, LANES), lambda i: (in_row(i), 0))]
        args.append(combine_gates)
    else:
        in_specs.append(pl.BlockSpec((t, d), lambda i: (in_row(i), 0)))
    in_specs += [mod_spec, mod_spec, pl.BlockSpec((1, d), lambda i: (0, 0)), pl.BlockSpec((1, d), lambda i: (0, 0))]
    args += [mods, mods_next, ln_g.reshape(1, d), ln_b.reshape(1, d)]
    out_shape = [jax.ShapeDtypeStruct((rows_out.n_rows, d), F32)]
    out_specs = [pl.BlockSpec((t, d), lambda i: (i, 0))]
    scratch = []
    if with_router:
        wr, br = router
        w_spec = pl.BlockSpec((d, LANES), lambda i: (0, 0))
        in_specs += [w_spec, w_spec, pl.BlockSpec((1, LANES), lambda i: (0, 0))]
        args += [*_split_bf16(wr), br]
    if with_u:
        out_shape.append(jax.ShapeDtypeStruct((rows_out.n_rows, d), F32 if with_router else BF16))
        out_specs.append(pl.BlockSpec((t, d), lambda i: (i, 0)))
    if with_router:
        out_shape += [jax.ShapeDtypeStruct((rows_out.n_rows, LANES), jnp.int32),
                      jax.ShapeDtypeStruct((rows_out.n_rows, LANES), F32),
                      jax.ShapeDtypeStruct((1, LANES), F32)]
        out_specs += [pl.BlockSpec((t, LANES), lambda i: (i, 0)), pl.BlockSpec((t, LANES), lambda i: (i, 0)),
                      pl.BlockSpec((1, LANES), lambda i: (0, 0))]
        scratch.append(pltpu.VMEM((1, LANES), F32))
    return pl.pallas_call(
        functools.partial(_resid_ln_kernel, alpha=alpha, gate_idx=gate_idx, shift_idx=shift_idx,
                          scale_idx=scale_idx, with_u=with_u, with_router=with_router, combine=combine),
        out_shape=out_shape,
        grid=(rows_out.n_blocks,),
        in_specs=in_specs,
        out_specs=out_specs,
        scratch_shapes=scratch,
        compiler_params=_cparams("arbitrary" if with_router else "parallel"),
        name="resid_ln",
    )(*args)


def _gla_kernel(*refs, reverse, fuse_post, q_scale, dk, dv):
    if fuse_post:
        q_ref, k_ref, v_ref, r_ref, w2_ref, gb_ref, other_ref, g_ref, ng_ref, out_ref, state_ref = refs
    else:
        q_ref, k_ref, v_ref, r_ref, w2_ref, gb_ref, out_ref, state_ref = refs

    @pl.when(pl.program_id(1) == 0)
    def _():
        state_ref[...] = jnp.zeros_like(state_ref)

    t = q_ref.shape[0]
    c = GLA_CHUNK
    shift = c.bit_length() - 1
    r_hi, r_lo = _split_bf16(r_ref[...])
    w_hi, w_lo = _split_bf16(w2_ref[...])
    z = _dot(r_hi, w_hi) + _dot(r_lo, w_hi) + _dot(r_hi, w_lo) + gb_ref[...]
    log_a = (jnp.minimum(z, 0.0) - jnp.log(1.0 + jnp.exp(-jnp.abs(z)))) * (1.0 / GLA_GATE_NORM)
    row = lax.broadcasted_iota(jnp.int32, (t, t), 0)
    col = lax.broadcasted_iota(jnp.int32, (t, t), 1)
    same_chunk = lax.shift_right_logical(row, shift) == lax.shift_right_logical(col, shift)
    ahead = (col - row) if reverse else (row - col)
    tri = jnp.where(jnp.where(same_chunk, ahead, -1) >= 0, 1.0, 0.0).astype(BF16)
    a_hi, a_lo = _split_bf16(log_a)
    cum = _dot(tri, a_hi) + _dot(tri, a_lo)
    grow = jnp.exp(cum)
    shrink = jnp.exp(-cum)
    crow = lax.broadcasted_iota(jnp.int32, (c, c), 0)
    ccol = lax.broadcasted_iota(jnp.int32, (c, c), 1)
    causal = (ccol >= crow) if reverse else (crow >= ccol)
    n_chunks = t // c
    for h in range(GLA_HEADS):
        ks = slice(h * dk, (h + 1) * dk)
        vs = slice(h * dv, (h + 1) * dv)
        q_dec = (q_ref[:, ks].astype(F32) * q_scale * grow[:, ks]).astype(BF16)
        k_in = k_ref[:, ks].astype(F32) * shrink[:, ks]
        for step in range(n_chunks):
            j = n_chunks - 1 - step if reverse else step
            rs = slice(j * c, (j + 1) * c)
            edge = j * c if reverse else (j + 1) * c - 1
            decay = jnp.exp(cum[edge:edge + 1, ks])
            qd = q_dec[rs]
            k_intra = k_in[rs].astype(BF16)
            k_state = (k_in[rs] * decay).astype(BF16)
            v = v_ref[rs, vs]
            scores = jnp.where(causal, _dot_nt(qd, k_intra), 0.0)
            state_t = state_ref[h]
            o = _dot(scores.astype(BF16), v) + _dot_nt(qd, state_t.astype(BF16))
            kv_t = lax.dot_general(v, k_state, (((0,), (0,)), ((), ())), preferred_element_type=F32)
            state_ref[h] = state_t * decay + kv_t
            if fuse_post:
                o = o + other_ref[rs, vs]
                o = o * lax.rsqrt(jnp.mean(o * o, axis=-1, keepdims=True) + LN_EPS) * ng_ref[...]
                g = g_ref[rs, vs].astype(F32)
                out_ref[rs, vs] = (o * (g * _sigmoid(g))).astype(out_ref.dtype)
            else:
                out_ref[rs, vs] = o


def _gla_scan(qkvg, r, w2, gate_b, rows, direction, other=None, norm_g=None):
    t_rows = qkvg.shape[0]
    dk_tot = w2.shape[2]
    dk = dk_tot // GLA_HEADS
    dv_tot = (qkvg.shape[1] - 2 * dk_tot) // 2
    dv = dv_tot // GLA_HEADS
    t = rows.tile
    reverse = direction == 1
    fuse_post = other is not None

    def block(b, n):
        if reverse:
            n = jnp.where(n < rows.nb_ctx, rows.nb_ctx - 1 - n, rows.nb_seq + rows.nb_ctx - 1 - n)
        return b * rows.nb_seq + n

    in_specs = [
        pl.BlockSpec((t, dk_tot), lambda b, n: (block(b, n), 0)),
        pl.BlockSpec((t, dk_tot), lambda b, n: (block(b, n), 1)),
        pl.BlockSpec((t, dv_tot), lambda b, n: (block(b, n), (2 * dk_tot) // dv_tot)),
        pl.BlockSpec((t, LANES), lambda b, n: (block(b, n), direction)),
        pl.BlockSpec((None, LANES, dk_tot), lambda b, n: (direction, 0, 0)),
        pl.BlockSpec((None, 1, dk_tot), lambda b, n: (direction, 0, 0)),
    ]
    args = [qkvg, qkvg, qkvg, r, w2, gate_b]
    if fuse_post:
        in_specs += [
            pl.BlockSpec((t, dv_tot), lambda b, n: (block(b, n), 0)),
            pl.BlockSpec((t, dv_tot), lambda b, n: (block(b, n), (2 * dk_tot + dv_tot) // dv_tot)),
            pl.BlockSpec((1, dv), lambda b, n: (0, 0)),
        ]
        args += [other, qkvg, norm_g.reshape(1, dv).astype(F32)]
    return pl.pallas_call(
        functools.partial(_gla_kernel, reverse=reverse, fuse_post=fuse_post, q_scale=float(dk) ** -0.5, dk=dk, dv=dv),
        out_shape=jax.ShapeDtypeStruct((t_rows, dv_tot), BF16 if fuse_post else F32),
        grid=(rows.batch, rows.nb_seq),
        in_specs=in_specs,
        out_specs=pl.BlockSpec((t, dv_tot), lambda b, n: (block(b, n), 0)),
        scratch_shapes=[pltpu.VMEM((GLA_HEADS, dv, dk), F32)],
        compiler_params=_cparams("parallel", "arbitrary"),
        name="gla_scan_bwd" if reverse else "gla_scan_fwd",
    )(*args)


def _gla_mixer(u, rows, w_in, gate_w1, gate_w2, gate_b, norm_g, w_out):
    d = u.shape[1]
    dk_tot = gate_w2.shape[2]
    qkvg = _matmul(u, w_in.astype(BF16), name="gla_in")
    w1 = jnp.zeros((d, 2 * LANES), F32)
    w2 = jnp.zeros((2, LANES, dk_tot), F32)
    for dd in range(2):
        w1 = w1.at[:, dd * LANES:dd * LANES + GLA_GATE_RANK].set(gate_w1[dd])
        w2 = w2.at[dd, :GLA_GATE_RANK].set(gate_w2[dd])
    r = _matmul(u, w1.astype(BF16), out_dtype=F32, name="gla_gate_in")
    gate_b = gate_b.reshape(2, 1, dk_tot).astype(F32)
    o_fwd = _gla_scan(qkvg, r, w2, gate_b, rows, 0)
    y = _gla_scan(qkvg, r, w2, gate_b, rows, 1, other=o_fwd, norm_g=norm_g)
    return _matmul(y, w_out.astype(BF16), name="gla_out")


def _rope_kernel(x_ref, cos_ref, sin_ref, q_ref, k2_ref, v2_ref, *, q_width, kv_width):
    cos = cos_ref[...]
    sin = sin_ref[...]
    lane = lax.broadcasted_iota(jnp.int32, cos.shape, 1)
    first = (lane % 32) < 16
    low = lane < SWA_HEAD_DIM

    def rope(slab):
        partner = jnp.where(first, pltpu.roll(slab, LANES - 16, 1), pltpu.roll(slab, 16, 1))
        return slab * cos + partner * sin

    scale = SWA_HEAD_DIM ** -0.5
    for s in range(q_width // LANES):
        slab = x_ref[:, s * LANES:(s + 1) * LANES].astype(F32)
        q_ref[:, s * LANES:(s + 1) * LANES] = (rope(slab) * scale).astype(q_ref.dtype)
    for s in range(kv_width // LANES):
        k_slab = rope(x_ref[:, q_width + s * LANES:q_width + (s + 1) * LANES].astype(F32))
        v_slab = x_ref[:, q_width + kv_width + s * LANES:q_width + kv_width + (s + 1) * LANES].astype(F32)
        for slab, out_ref in ((k_slab, k2_ref), (v_slab, v2_ref)):
            swapped = pltpu.roll(slab, SWA_HEAD_DIM, 1)
            out_ref[2 * s] = jnp.where(low, slab, swapped).astype(out_ref.dtype)
            out_ref[2 * s + 1] = jnp.where(low, swapped, slab).astype(out_ref.dtype)


def _rope_tables(n_ctx, seq):
    t = jnp.arange(seq, dtype=jnp.int32)
    row = (t // GRID_W).astype(F32)
    col = (t % GRID_W).astype(F32)
    n_freq = SWA_HEAD_DIM // 4
    inv_freq = ROPE_BASE ** (-jnp.arange(n_freq, dtype=F32) / n_freq)
    lane = jnp.arange(LANES)
    within = lane % SWA_HEAD_DIM
    pos = jnp.where((within < SWA_HEAD_DIM // 2)[None, :], row[:, None], col[:, None])
    ang = pos * inv_freq[lane % n_freq][None, :]
    sign = jnp.where((lane % 32) < 16, -1.0, 1.0)[None, :]
    cos = jnp.concatenate([jnp.ones((n_ctx, LANES), F32), jnp.cos(ang)], axis=0)
    sin = jnp.concatenate([jnp.zeros((n_ctx, LANES), F32), jnp.sin(ang) * sign], axis=0)
    return cos, sin


def _rope_split(qkv, cos, sin, rows, q_width, kv_width):
    t_rows = qkv.shape[0]
    t = rows.tile
    n_kv = kv_width // SWA_HEAD_DIM
    seq_rows = rows.nb_seq * t
    kv_shape = jax.ShapeDtypeStruct((rows.batch, n_kv, seq_rows, LANES), BF16)
    kv_spec = pl.BlockSpec((None, n_kv, t, LANES), lambda i: (i // rows.nb_seq, 0, i % rows.nb_seq, 0))
    return pl.pallas_call(
        functools.partial(_rope_kernel, q_width=q_width, kv_width=kv_width),
        out_shape=[jax.ShapeDtypeStruct((t_rows, q_width), BF16), kv_shape, kv_shape],
        grid=(rows.n_blocks,),
        in_specs=[
            pl.BlockSpec((t, qkv.shape[1]), lambda i: (i, 0)),
            pl.BlockSpec((t, LANES), lambda i: (i % rows.nb_seq, 0)),
            pl.BlockSpec((t, LANES), lambda i: (i % rows.nb_seq, 0)),
        ],
        out_specs=[pl.BlockSpec((t, q_width), lambda i: (i, 0)), kv_spec, kv_spec],
        compiler_params=_cparams("parallel"),
        name="rope_split",
    )(qkv, cos, sin)


def _attend_heads(q_ref, kk, vv, mask, sink_ref, head0, o_ref):
    tq = q_ref.shape[0]
    lane = lax.broadcasted_iota(jnp.int32, (tq, LANES), 1)
    low = lane < SWA_HEAD_DIM
    for pair in range(SWA_GROUP // 2):
        q2 = q_ref[:, pair * LANES:(pair + 1) * LANES]
        outs = []
        for half in range(2):
            qh = jnp.where(low if half == 0 else lane >= SWA_HEAD_DIM, q2, jnp.zeros_like(q2))
            s = _dot_nt(qh, kk)
            if mask is not None:
                s = jnp.where(mask, s, NEG_INF)
            sink = sink_ref[head0 + 2 * pair + half]
            m = jnp.maximum(jnp.max(s, axis=-1, keepdims=True), sink)
            p = jnp.exp(s - m)
            denom = jnp.sum(p, axis=-1, keepdims=True) + jnp.exp(sink - m)
            outs.append(_dot(p.astype(BF16), vv) / denom)
        o_ref[:, pair * LANES:(pair + 1) * LANES] = jnp.where(low, outs[0], outs[1]).astype(o_ref.dtype)


def _attn_kernel(sink_ref, q_ref, kc_ref, vc_ref, kp_ref, vp_ref, km_ref, vm_ref, kn_ref, vn_ref,
                 o_ref, kk_ref, vv_ref, *, n_ctx, nq_ctx, nq_lat):
    n = pl.program_id(2)
    head0 = pl.program_id(1) * SWA_GROUP
    w = WINDOW

    @pl.when(n < nq_ctx)
    def _():
        _attend_heads(q_ref, kc_ref[...], vc_ref[...], None, sink_ref, head0, o_ref)

    @pl.when(n >= nq_ctx)
    def _():
        m = n - nq_ctx
        for ref, parts in ((kk_ref, (kc_ref, kp_ref, km_ref, kn_ref)), (vv_ref, (vc_ref, vp_ref, vm_ref, vn_ref))):
            ref[0:n_ctx] = parts[0][...]
            ref[n_ctx:n_ctx + w] = parts[1][...]
            ref[n_ctx + w:n_ctx + 2 * w] = parts[2][...]
            ref[n_ctx + 2 * w:n_ctx + 3 * w] = parts[3][...]
        nk = n_ctx + 3 * w
        qpos = lax.broadcasted_iota(jnp.int32, (w, nk), 0)
        col = lax.broadcasted_iota(jnp.int32, (w, nk), 1)
        rel = jnp.where(col < n_ctx, 0, col - (n_ctx + w) - qpos)
        far = 4 * w
        no_prev = jnp.where(m > 0, 0, far)
        no_next = jnp.where(m < nq_lat - 1, 0, far)
        missing = jnp.where(col < n_ctx, 0,
                            jnp.where(col < n_ctx + w, no_prev, jnp.where(col >= n_ctx + 2 * w, no_next, 0)))
        mask = jnp.abs(rel) + missing <= w
        _attend_heads(q_ref, kk_ref[...], vv_ref[...], mask, sink_ref, head0, o_ref)


def _window_attention(q, k2, v2, sink, rows, n_ctx):
    t_rows, q_width = q.shape
    batch, n_kv, seq_rows, _ = k2.shape
    w = WINDOW
    gw = SWA_GROUP * SWA_HEAD_DIM
    nq_seq = seq_rows // w
    nq_ctx = n_ctx // w
    nq_lat = nq_seq - nq_ctx
    ctx_kv = pl.BlockSpec((None, None, n_ctx, LANES), lambda b, g, n: (b, g, 0, 0))

    def band(shift):
        def index(b, g, n):
            return (b, g, nq_ctx + jnp.clip(n - nq_ctx + shift, 0, nq_lat - 1), 0)
        return pl.BlockSpec((None, None, w, LANES), index)

    nk = n_ctx + 3 * w
    return pl.pallas_call(
        functools.partial(_attn_kernel, n_ctx=n_ctx, nq_ctx=nq_ctx, nq_lat=nq_lat),
        out_shape=jax.ShapeDtypeStruct((t_rows, q_width), BF16),
        grid=(batch, n_kv, nq_seq),
        in_specs=[pl.BlockSpec(memory_space=pltpu.SMEM),
                  pl.BlockSpec((w, gw), lambda b, g, n: (b * nq_seq + n, g)),
                  ctx_kv, ctx_kv, band(-1), band(-1), band(0), band(0), band(1), band(1)],
        out_specs=pl.BlockSpec((w, gw), lambda b, g, n: (b * nq_seq + n, g)),
        scratch_shapes=[pltpu.VMEM((nk, LANES), BF16), pltpu.VMEM((nk, LANES), BF16)],
        compiler_params=_cparams("parallel", "parallel", "parallel"),
        name="window_attn",
    )(sink, q, k2, v2, k2, v2, k2, v2, k2, v2)


def _window_mixer(u, rows, n_ctx, seq, w_qkv, b_qkv, sink, w_out, b_out):
    q_width = w_out.shape[0]
    kv_width = (w_qkv.shape[1] - q_width) // 2
    qkv = _matmul(u, w_qkv.astype(BF16), bias=b_qkv, name="swa_qkv")
    cos, sin = _rope_tables(n_ctx, seq)
    q, k2, v2 = _rope_split(qkv, cos, sin, rows, q_width, kv_width)
    o = _window_attention(q, k2, v2, sink.astype(F32), rows, n_ctx)
    return _matmul(o, w_out.astype(BF16), bias=b_out, name="swa_out")


def _conv_kernel(gi_ref, go_ref, val_ref, w_ref, o_ref, p_ref, *, n_ctx):
    length = gi_ref.shape[0]
    pad = 8
    p_ref[0:pad] = jnp.zeros((pad, p_ref.shape[1]), F32)
    p_ref[pad + length:pad + length + pad] = jnp.zeros((pad, p_ref.shape[1]), F32)
    p_ref[pad:pad + length] = gi_ref[...].astype(F32) * val_ref[...].astype(F32)
    t = lax.broadcasted_iota(jnp.int32, (length, 1), 0)
    seg_start = t * (t - n_ctx) == 0
    seg_end = (t - (n_ctx - 1)) * (t - (length - 1)) == 0
    prev = jnp.where(seg_start, 0.0, p_ref[pad - 1:pad - 1 + length])
    nxt = jnp.where(seg_end, 0.0, p_ref[pad + 1:pad + 1 + length])
    z = prev * w_ref[0:1, :] + p_ref[pad:pad + length] * w_ref[1:2, :] + nxt * w_ref[2:3, :]
    o_ref[...] = (go_ref[...].astype(F32) * z).astype(o_ref.dtype)


def _conv_gate(proj, conv_w, rows, n_ctx):
    t_rows = proj.shape[0]
    d = proj.shape[1] // 3
    length = rows.nb_seq * rows.tile
    tc = LANES
    nc = d // tc
    return pl.pallas_call(
        functools.partial(_conv_kernel, n_ctx=n_ctx),
        out_shape=jax.ShapeDtypeStruct((t_rows, d), BF16),
        grid=(rows.batch, nc),
        in_specs=[
            pl.BlockSpec((length, tc), lambda b, c: (b, c)),
            pl.BlockSpec((length, tc), lambda b, c: (b, nc + c)),
            pl.BlockSpec((length, tc), lambda b, c: (b, 2 * nc + c)),
            pl.BlockSpec((CONV_WIDTH, tc), lambda b, c: (0, c)),
        ],
        out_specs=pl.BlockSpec((length, tc), lambda b, c: (b, c)),
        scratch_shapes=[pltpu.VMEM((length + 16, tc), F32)],
        compiler_params=_cparams("parallel", "parallel"),
        name="conv_gate",
    )(proj, proj, proj, conv_w.astype(F32))


def _conv_mixer(u, rows, n_ctx, w_in, w_conv, w_out):
    proj = _matmul(u, w_in.astype(BF16), name="conv_in")
    zg = _conv_gate(proj, w_conv, rows, n_ctx)
    return _matmul(zg, w_out.astype(BF16), name="conv_out")


def _row_gather_copy(u_hbm, x_buf, sem, slot, tok, r):
    return pltpu.make_async_copy(u_hbm.at[pl.ds(tok, 1)], x_buf.at[slot, pl.ds(r, 1)], sem.at[slot])


def _moe_kernel(blk_exp_ref, n_used_ref, row_tok_ref, u_hbm, wgu_ref, bgu_ref, wdn_ref, bdn_ref, o_ref,
                x_buf, sem, wgu_bf_ref, wdn_bf_ref):
    i = pl.program_id(0)
    tm = o_ref.shape[0]
    n_used = n_used_ref[0]
    used = i < n_used
    slot = lax.rem(i, 2)

    def wait_gather():
        pltpu.make_async_copy(u_hbm.at[pl.ds(0, tm)], x_buf.at[slot], sem.at[slot]).wait()

    @pl.when(used & (i == 0))
    def _():
        def body(r, carry):
            _row_gather_copy(u_hbm, x_buf, sem, 0, row_tok_ref[r], r).start()
            return carry
        lax.fori_loop(0, tm, body, 0, unroll=8)

    new_expert = (i == 0) | (blk_exp_ref[i] != blk_exp_ref[jnp.maximum(i - 1, 0)])

    @pl.when(used & new_expert)
    def _():
        wgu_bf_ref[...] = wgu_ref[...].astype(BF16)
        wdn_bf_ref[...] = wdn_ref[...].astype(BF16)

    @pl.when(used)
    def _():
        wait_gather()
        x = x_buf[slot].astype(BF16)
        base = (i + 1) * tm
        for r in range(tm):
            _row_gather_copy(u_hbm, x_buf, sem, 1 - slot, row_tok_ref[base + r], r).start()
        ff = wdn_ref.shape[0]
        gu = _dot(x, wgu_bf_ref[...]) + bgu_ref[...]
        glu = jnp.minimum(gu[:, :ff], SWIGLU_LIMIT)
        lin = jnp.clip(gu[:, ff:], -SWIGLU_LIMIT, SWIGLU_LIMIT)
        act = glu * _sigmoid(SWIGLU_ALPHA * glu) * (lin + 1.0)
        o_ref[...] = (_dot(act.astype(BF16), wdn_bf_ref[...]) + bdn_ref[...]).astype(o_ref.dtype)

    @pl.when((i == n_used) & (i > 0))
    def _():
        wait_gather()

    @pl.when(i >= n_used)
    def _():
        o_ref[...] = jnp.zeros_like(o_ref)


def _expert_mlp(u, row_tok, blk_exp, n_used, layer, w_gate_up, b_gate_up, w_down, b_down):
    d = u.shape[1]
    n_rows = row_tok.shape[0]
    depth, n_exp, _, ff2 = w_gate_up.shape
    ff = ff2 // 2
    tm = MOE_ROW_BLOCK
    grid_spec = pltpu.PrefetchScalarGridSpec(
        num_scalar_prefetch=3,
        grid=(n_rows // tm,),
        in_specs=[
            pl.BlockSpec(memory_space=pl.ANY),
            pl.BlockSpec((None, None, d, ff2), lambda i, be, nu, rt: (layer, be[i], 0, 0)),
            pl.BlockSpec((None, None, 1, ff2), lambda i, be, nu, rt: (layer, be[i], 0, 0)),
            pl.BlockSpec((None, None, ff, d), lambda i, be, nu, rt: (layer, be[i], 0, 0)),
            pl.BlockSpec((None, None, 1, d), lambda i, be, nu, rt: (layer, be[i], 0, 0)),
        ],
        out_specs=pl.BlockSpec((tm, d), lambda i, be, nu, rt: (i, 0)),
        scratch_shapes=[pltpu.VMEM((2, tm, d), F32), pltpu.SemaphoreType.DMA((2,)),
                        pltpu.VMEM((d, ff2), BF16), pltpu.VMEM((ff, d), BF16)],
    )
    return pl.pallas_call(
        _moe_kernel,
        out_shape=jax.ShapeDtypeStruct((n_rows, d), BF16),
        grid_spec=grid_spec,
        compiler_params=pltpu.CompilerParams(dimension_semantics=("arbitrary",), vmem_limit_bytes=MOE_VMEM_LIMIT),
        name="expert_mlp",
    )(blk_exp, n_used, row_tok, u, w_gate_up, b_gate_up.reshape(depth, n_exp, 1, ff2),
      w_down, b_down.reshape(depth, n_exp, 1, d))


def _moe(u, route_idx, counts, layer, w_gate_up, b_gate_up, w_down, b_down):
    n_tok, d = u.shape
    tm = MOE_ROW_BLOCK
    n_assign = n_tok * TOP_K
    n_blocks = n_assign // tm + N_EXPERTS
    n_rows = n_blocks * tm
    top_exp = route_idx[:, :TOP_K]
    rank = route_idx[:, TOP_K:2 * TOP_K]
    counts = counts[0, :N_EXPERTS].astype(jnp.int32)
    padded = (counts + tm - 1) // tm * tm
    start = jnp.cumsum(counts) - counts
    pad_end = jnp.cumsum(padded)
    pad_start = pad_end - padded
    blk_start = jnp.arange(n_blocks, dtype=jnp.int32) * tm
    blk_exp = jnp.minimum(jnp.sum(pad_end[None, :] <= blk_start[:, None], axis=1), N_EXPERTS - 1).astype(jnp.int32)
    n_used = (pad_end[-1:] // tm).astype(jnp.int32)
    order = jnp.argsort(top_exp.reshape(-1), stable=True).astype(jnp.int32)
    row_exp = jnp.repeat(blk_exp, tm)
    within = jnp.arange(n_rows, dtype=jnp.int32) - pad_start[row_exp]
    src = jnp.clip(start[row_exp] + within, 0, n_assign - 1)
    row_tok = jnp.where(within < counts[row_exp], order[src] // TOP_K, 0)
    y_rows = _expert_mlp(u, row_tok, blk_exp, n_used, layer, w_gate_up, b_gate_up, w_down, b_down)
    pos = pad_start[top_exp] + rank
    return y_rows[pos.T.reshape(-1)].reshape(TOP_K, n_tok, d)


def kernel(x, c, ctx, c_ctx, ada_w, ada_b, ln_g, ln_b, gla_w_in, gla_gate_w1, gla_gate_w2, gla_gate_b, gla_norm_g, gla_w_out, swa_w_qkv, swa_b_qkv, swa_sink, swa_w_out, swa_b_out, conv_w_in, conv_w, conv_w_out, moe_router_w, moe_router_b, moe_w_gate_up, moe_b_gate_up, moe_w_down, moe_b_down):
    batch, seq, d = x.shape
    n_ctx = ctx.shape[1]
    depth = ada_w.shape[0]
    alpha = (2 * depth) ** 0.25
    rows_full = _Rows(batch, n_ctx // ROW_TILE, seq // ROW_TILE)
    rows_lat = _Rows(batch, 0, seq // ROW_TILE)

    cond = jnp.zeros((8, d), F32).at[0].set(c_ctx).at[1:1 + batch].set(c)
    ada = _ada_mods(cond, ada_w, ada_b)
    mods_ctx = jnp.broadcast_to(ada[:, 0:1], (depth, batch, N_MOD * d))
    mods = jnp.stack([mods_ctx, ada[:, 1:1 + batch]], axis=2).reshape(depth, batch, 2, N_MOD, d)

    router_w = jnp.zeros((depth, d, LANES), F32).at[:, :, :N_EXPERTS].set(moe_router_w)
    router_b = jnp.zeros((depth, 1, LANES), F32).at[:, 0, :N_EXPERTS].set(moe_router_b)

    h = jnp.concatenate([ctx, x], axis=1).reshape(rows_full.n_rows, d)
    u = _modulate(h, mods[0], rows_full, 0, 1)
    rows = rows_full
    for i in range(depth):
        last = i == depth - 1
        kind, j = i % 3, i // 3
        if kind == 0:
            y = _gla_mixer(u, rows, gla_w_in[j], gla_gate_w1[j], gla_gate_w2[j], gla_gate_b[j], gla_norm_g[j], gla_w_out[j])
        elif kind == 1:
            y = _window_mixer(u, rows, n_ctx, seq, swa_w_qkv[j], swa_b_qkv[j], swa_sink[j], swa_w_out[j], swa_b_out[j])
        else:
            y = _conv_mixer(u, rows, n_ctx, conv_w_in[j], conv_w[j], conv_w_out[j])
        rows_out = rows_lat if last else rows
        h, u, route_idx, route_gate, counts = _resid_ln(
            h, y, mods[i], mods[i], ln_g[i, 0], ln_b[i, 0], rows, rows_out, alpha=alpha, gate_idx=2,
            next_mod=(3, 4), router=(router_w[i], router_b[i]))
        rows = rows_out
        y = _moe(u, route_idx, counts, i, moe_w_gate_up, moe_b_gate_up, moe_w_down, moe_b_down)
        if last:
            (h,) = _resid_ln(h, y, mods[i], mods[i], ln_g[i, 1], ln_b[i, 1], rows, rows, alpha=alpha, gate_idx=5,
                             combine_gates=route_gate)
        else:
            h, u = _resid_ln(h, y, mods[i], mods[i + 1], ln_g[i, 1], ln_b[i, 1], rows, rows, alpha=alpha, gate_idx=5,
                             next_mod=(0, 1), combine_gates=route_gate)
    return h.reshape(batch, seq, d)
```

```python
import functools

import jax
import jax.numpy as jnp
from jax import lax
from jax.experimental import pallas as pl
from jax.experimental.pallas import tpu as pltpu

F32 = jnp.float32
BF16 = jnp.bfloat16
HIGHEST = lax.Precision.HIGHEST

GRID_W = 64
GLA_HEADS = 4
GLA_GATE_RANK = 16
GLA_GATE_NORM = 16.0
GLA_CHUNK = 64
SWA_HEAD_DIM = 64
SWA_GROUP = 8
WINDOW = 128
ROPE_BASE = 10000.0
CONV_WIDTH = 3
N_EXPERTS = 32
TOP_K = 4
SWIGLU_LIMIT = 7.0
SWIGLU_ALPHA = 1.702
N_MOD = 6
LN_EPS = 1e-5
NEG_INF = -1e30

LANES = 128
ROW_TILE = 256
MOE_ROW_BLOCK = 256
VMEM_LIMIT = 48 * 1024 * 1024
MOE_VMEM_LIMIT = 60 * 1024 * 1024


def _cparams(*sem):
    return pltpu.CompilerParams(dimension_semantics=sem, vmem_limit_bytes=VMEM_LIMIT)


def _sigmoid(x):
    return 1.0 / (1.0 + jnp.exp(-x))


def _dot(a, b):
    return jnp.dot(a, b, preferred_element_type=F32)


def _dot_nt(a, b):
    return lax.dot_general(a, b, (((1,), (1,)), ((), ())), preferred_element_type=F32)


def _split_bf16(x):
    hi = x.astype(BF16)
    lo = (x - hi.astype(F32)).astype(BF16)
    return hi, lo


class _Rows:
    def __init__(self, batch, nb_ctx, nb_lat, tile=ROW_TILE):
        self.batch, self.nb_ctx, self.nb_lat, self.tile = batch, nb_ctx, nb_lat, tile
        self.nb_seq = nb_ctx + nb_lat
        self.n_blocks = batch * self.nb_seq
        self.n_rows = self.n_blocks * tile

    def batch_of(self, i):
        return i // self.nb_seq

    def seg_of(self, i):
        if self.nb_ctx == 0:
            return 1
        return jnp.where(i % self.nb_seq >= self.nb_ctx, 1, 0)

    def latent_block(self, i):
        return (i // self.nb_lat) * self.nb_seq + self.nb_ctx + i % self.nb_lat


def _ada_kernel(c_ref, w_ref, b_ref, o_ref):
    c = c_ref[...]
    s = c * _sigmoid(c)
    o_ref[...] = jnp.dot(s, w_ref[...], precision=HIGHEST, preferred_element_type=F32) + b_ref[...]


def _ada_mods(cond, ada_w, ada_b):
    depth, d, n = ada_w.shape
    tn = 1024 if n % 1024 == 0 else n
    return pl.pallas_call(
        _ada_kernel,
        out_shape=jax.ShapeDtypeStruct((depth, 8, n), F32),
        grid=(depth, n // tn),
        in_specs=[
            pl.BlockSpec((8, d), lambda l, j: (0, 0)),
            pl.BlockSpec((None, d, tn), lambda l, j: (l, 0, j)),
            pl.BlockSpec((None, 1, tn), lambda l, j: (l, 0, j)),
        ],
        out_specs=pl.BlockSpec((None, 8, tn), lambda l, j: (l, 0, j)),
        compiler_params=_cparams("parallel", "parallel"),
        name="ada_mods",
    )(cond, ada_w, ada_b.reshape(depth, 1, n))


def _modulate_kernel(h_ref, m_ref, u_ref, *, shift_idx, scale_idx):
    h = h_ref[...]
    u = h * (1.0 + m_ref[scale_idx:scale_idx + 1, :]) + m_ref[shift_idx:shift_idx + 1, :]
    u_ref[...] = u.astype(u_ref.dtype)


def _modulate(h, mods, rows, shift_idx, scale_idx):
    d = h.shape[1]
    t = rows.tile
    return pl.pallas_call(
        functools.partial(_modulate_kernel, shift_idx=shift_idx, scale_idx=scale_idx),
        out_shape=jax.ShapeDtypeStruct(h.shape, BF16),
        grid=(rows.n_blocks,),
        in_specs=[
            pl.BlockSpec((t, d), lambda i: (i, 0)),
            pl.BlockSpec((None, None, N_MOD, d), lambda i: (rows.batch_of(i), rows.seg_of(i), 0, 0)),
        ],
        out_specs=pl.BlockSpec((t, d), lambda i: (i, 0)),
        compiler_params=_cparams("parallel"),
        name="modulate",
    )(h, mods)


def _mm_kernel(x_ref, w_ref, o_ref):
    o_ref[...] = _dot(x_ref[...], w_ref[...]).astype(o_ref.dtype)


def _mm_bias_kernel(x_ref, w_ref, b_ref, o_ref):
    o_ref[...] = (_dot(x_ref[...], w_ref[...]) + b_ref[...]).astype(o_ref.dtype)


def _pick_tile(n, candidates):
    for c in candidates:
        if n % c == 0:
            return c
    return n


def _matmul(x, w, bias=None, out_dtype=BF16, name="proj"):
    m, k = x.shape
    n = w.shape[1]
    tm = _pick_tile(m, (1024, 512, 256))
    tn = _pick_tile(n, (1024, 768, 640, 512, 256, 128))
    in_specs = [
        pl.BlockSpec((tm, k), lambda i, j: (i, 0)),
        pl.BlockSpec((k, tn), lambda i, j: (0, j)),
    ]
    args = [x, w]
    body = _mm_kernel
    if bias is not None:
        in_specs.append(pl.BlockSpec((1, tn), lambda i, j: (0, j)))
        args.append(bias.reshape(1, n).astype(F32))
        body = _mm_bias_kernel
    return pl.pallas_call(
        body,
        out_shape=jax.ShapeDtypeStruct((m, n), out_dtype),
        grid=(m // tm, n // tn),
        in_specs=in_specs,
        out_specs=pl.BlockSpec((tm, tn), lambda i, j: (i, j)),
        compiler_params=_cparams("parallel", "parallel"),
        name=name,
    )(*args)


def _route_block(logits, count_ref):
    t = logits.shape[0]
    lane = lax.broadcasted_iota(jnp.int32, (t, LANES), 1)
    lane_f = lane.astype(F32)
    left = jnp.where(lane < N_EXPERTS, logits, NEG_INF)
    hot = jnp.zeros((t, LANES), F32)
    sels, vals = [], []
    for _ in range(TOP_K):
        m = jnp.max(left, axis=-1, keepdims=True)
        sel = jnp.min(jnp.where(left == m, lane_f, float(LANES)), axis=-1, keepdims=True)
        pick = lane_f == sel
        left = jnp.where(pick, NEG_INF, left)
        hot = jnp.where(pick, 1.0, hot)
        sels.append(sel)
        vals.append(m)
    row = lax.broadcasted_iota(jnp.int32, (t, t), 0)
    col = lax.broadcasted_iota(jnp.int32, (t, t), 1)
    earlier = jnp.where(row > col, 1.0, 0.0).astype(BF16)
    before = _dot(earlier, hot.astype(BF16)) + count_ref[...]
    count_ref[...] += jnp.sum(hot, axis=0, keepdims=True)
    exps = [jnp.exp(v - vals[0]) for v in vals]
    denom = exps[0]
    for e in exps[1:]:
        denom = denom + e
    idx = jnp.zeros((t, LANES), jnp.int32)
    gates = jnp.zeros((t, LANES), F32)
    for k in range(TOP_K):
        rank = jnp.sum(jnp.where(lane_f == sels[k], before, 0.0), axis=-1, keepdims=True)
        idx = jnp.where(lane == k, sels[k].astype(jnp.int32), idx)
        idx = jnp.where(lane == TOP_K + k, rank.astype(jnp.int32), idx)
        gates = jnp.where(lane == k, exps[k] / denom, gates)
    return idx, gates


def _resid_ln_kernel(*refs, alpha, gate_idx, shift_idx, scale_idx, with_u, with_router, combine):
    h_ref, y_ref = refs[:2]
    pos = 2
    if combine:
        yg_ref = refs[pos]
        pos += 1
    m_ref, mn_ref, g_ref, b_ref = refs[pos:pos + 4]
    pos += 4
    if with_router:
        wr_hi_ref, wr_lo_ref, br_ref = refs[pos:pos + 3]
        pos += 3
    hn_ref = refs[pos]
    pos += 1
    if combine:
        gates = yg_ref[...]
        y = gates[:, 0:1] * y_ref[0].astype(F32)
        for k in range(1, TOP_K):
            y = y + gates[:, k:k + 1] * y_ref[k].astype(F32)
    else:
        y = y_ref[...].astype(F32)
    z = alpha * h_ref[...] + y * m_ref[gate_idx:gate_idx + 1, :]
    zc = z - jnp.mean(z, axis=-1, keepdims=True)
    var = jnp.mean(zc * zc, axis=-1, keepdims=True)
    hn = zc * lax.rsqrt(var + LN_EPS) * g_ref[...] + b_ref[...]
    hn_ref[...] = hn
    if with_u:
        u_ref = refs[pos]
        pos += 1
        u = hn * (1.0 + mn_ref[scale_idx:scale_idx + 1, :]) + mn_ref[shift_idx:shift_idx + 1, :]
        if with_router:
            n_slabs = u.shape[1] // LANES
            for s in range(n_slabs):
                u_ref[pl.ds(s, u.shape[0], stride=n_slabs), :] = u[:, s * LANES:(s + 1) * LANES]
        else:
            u_ref[...] = u.astype(u_ref.dtype)
        if with_router:
            idx_ref, gate_ref, cnt_ref, count_ref = refs[pos:pos + 4]

            @pl.when(pl.program_id(0) == 0)
            def _():
                count_ref[...] = jnp.zeros_like(count_ref)

            u_hi, u_lo = _split_bf16(u)
            w_hi = wr_hi_ref[...]
            logits = _dot(u_hi, w_hi) + _dot(u_lo, w_hi) + _dot(u_hi, wr_lo_ref[...]) + br_ref[...]
            idx, gates = _route_block(logits, count_ref)
            idx_ref[...] = idx
            gate_ref[...] = gates
            cnt_ref[...] = count_ref[...]


def _resid_ln(h, y, mods, mods_next, ln_g, ln_b, rows_in, rows_out, *, alpha, gate_idx, next_mod=None, router=None,
              combine_gates=None):
    d = h.shape[1]
    t = rows_out.tile
    if rows_in is rows_out:
        in_row = lambda i: i
    else:
        in_row = rows_in.latent_block
    with_u = next_mod is not None
    with_router = router is not None
    combine = combine_gates is not None
    shift_idx, scale_idx = next_mod if with_u else (0, 0)
    mod_spec = pl.BlockSpec((None, None, N_MOD, d), lambda i: (rows_out.batch_of(i), rows_out.seg_of(i), 0, 0))
    in_specs = [pl.BlockSpec((t, d), lambda i: (in_row(i), 0))]
    args = [h, y]
    if combine:
        in_specs += [pl.BlockSpec((TOP_K, t, d), lambda i: (0, in_row(i), 0)),
                     pl.BlockSpec((t, LANES), lambda i: (in_row(i), 0))]
        args.append(combine_gates)
    else:
        in_specs.append(pl.BlockSpec((t, d), lambda i: (in_row(i), 0)))
    in_specs += [mod_spec, mod_spec, pl.BlockSpec((1, d), lambda i: (0, 0)), pl.BlockSpec((1, d), lambda i: (0, 0))]
    args += [mods, mods_next, ln_g.reshape(1, d), ln_b.reshape(1, d)]
    out_shape = [jax.ShapeDtypeStruct((rows_out.n_rows, d), F32)]
    out_specs = [pl.BlockSpec((t, d), lambda i: (i, 0))]
    scratch = []
    if with_router:
        wr, br = router
        w_spec = pl.BlockSpec((d, LANES), lambda i: (0, 0))
        in_specs += [w_spec, w_spec, pl.BlockSpec((1, LANES), lambda i: (0, 0))]
        args += [*_split_bf16(wr), br]
    if with_u:
        if with_router:
            n_slabs = d // LANES
            out_shape.append(jax.ShapeDtypeStruct((rows_out.n_rows * n_slabs, LANES), F32))
            out_specs.append(pl.BlockSpec((t * n_slabs, LANES), lambda i: (i, 0)))
        else:
            out_shape.append(jax.ShapeDtypeStruct((rows_out.n_rows, d), BF16))
            out_specs.append(pl.BlockSpec((t, d), lambda i: (i, 0)))
    if with_router:
        out_shape += [jax.ShapeDtypeStruct((rows_out.n_rows, LANES), jnp.int32),
                      jax.ShapeDtypeStruct((rows_out.n_rows, LANES), F32),
                      jax.ShapeDtypeStruct((1, LANES), F32)]
        out_specs += [pl.BlockSpec((t, LANES), lambda i: (i, 0)), pl.BlockSpec((t, LANES), lambda i: (i, 0)),
                      pl.BlockSpec((1, LANES), lambda i: (0, 0))]
        scratch.append(pltpu.VMEM((1, LANES), F32))
    return pl.pallas_call(
        functools.partial(_resid_ln_kernel, alpha=alpha, gate_idx=gate_idx, shift_idx=shift_idx,
                          scale_idx=scale_idx, with_u=with_u, with_router=with_router, combine=combine),
        out_shape=out_shape,
        grid=(rows_out.n_blocks,),
        in_specs=in_specs,
        out_specs=out_specs,
        scratch_shapes=scratch,
        compiler_params=_cparams("arbitrary" if with_router else "parallel"),
        name="resid_ln",
    )(*args)


def _gla_kernel(*refs, reverse, fuse_post, q_scale, dk, dv):
    if fuse_post:
        q_ref, k_ref, v_ref, r_ref, w2_ref, gb_ref, other_ref, g_ref, ng_ref, out_ref, state_ref = refs
    else:
        q_ref, k_ref, v_ref, r_ref, w2_ref, gb_ref, out_ref, state_ref = refs

    @pl.when(pl.program_id(1) == 0)
    def _():
        state_ref[...] = jnp.zeros_like(state_ref)

    t = q_ref.shape[0]
    c = GLA_CHUNK
    shift = c.bit_length() - 1
    r_hi, r_lo = _split_bf16(r_ref[...])
    w_hi, w_lo = _split_bf16(w2_ref[...])
    z = _dot(r_hi, w_hi) + _dot(r_lo, w_hi) + _dot(r_hi, w_lo) + gb_ref[...]
    log_a = (jnp.minimum(z, 0.0) - jnp.log(1.0 + jnp.exp(-jnp.abs(z)))) * (1.0 / GLA_GATE_NORM)
    row = lax.broadcasted_iota(jnp.int32, (t, t), 0)
    col = lax.broadcasted_iota(jnp.int32, (t, t), 1)
    same_chunk = lax.shift_right_logical(row, shift) == lax.shift_right_logical(col, shift)
    ahead = (col - row) if reverse else (row - col)
    tri = jnp.where(jnp.where(same_chunk, ahead, -1) >= 0, 1.0, 0.0).astype(BF16)
    a_hi, a_lo = _split_bf16(log_a)
    cum = _dot(tri, a_hi) + _dot(tri, a_lo)
    grow = jnp.exp(cum)
    shrink = jnp.exp(-cum)
    crow = lax.broadcasted_iota(jnp.int32, (c, c), 0)
    ccol = lax.broadcasted_iota(jnp.int32, (c, c), 1)
    causal = (ccol >= crow) if reverse else (crow >= ccol)
    n_chunks = t // c
    for h in range(GLA_HEADS):
        ks = slice(h * dk, (h + 1) * dk)
        vs = slice(h * dv, (h + 1) * dv)
        q_dec = (q_ref[:, ks].astype(F32) * q_scale * grow[:, ks]).astype(BF16)
        k_in = k_ref[:, ks].astype(F32) * shrink[:, ks]
        for step in range(n_chunks):
            j = n_chunks - 1 - step if reverse else step
            rs = slice(j * c, (j + 1) * c)
            edge = j * c if reverse else (j + 1) * c - 1
            decay = jnp.exp(cum[edge:edge + 1, ks])
            qd = q_dec[rs]
            k_intra = k_in[rs].astype(BF16)
            k_state = (k_in[rs] * decay).astype(BF16)
            v = v_ref[rs, vs]
            scores = jnp.where(causal, _dot_nt(qd, k_intra), 0.0)
            state_t = state_ref[h]
            o = _dot(scores.astype(BF16), v) + _dot_nt(qd, state_t.astype(BF16))
            kv_t = lax.dot_general(v, k_state, (((0,), (0,)), ((), ())), preferred_element_type=F32)
            state_ref[h] = state_t * decay + kv_t
            if fuse_post:
                o = o + other_ref[rs, vs]
                o = o * lax.rsqrt(jnp.mean(o * o, axis=-1, keepdims=True) + LN_EPS) * ng_ref[...]
                g = g_ref[rs, vs].astype(F32)
                out_ref[rs, vs] = (o * (g * _sigmoid(g))).astype(out_ref.dtype)
            else:
                out_ref[rs, vs] = o


def _gla_scan(qkvg, r, w2, gate_b, rows, direction, other=None, norm_g=None):
    t_rows = qkvg.shape[0]
    dk_tot = w2.shape[2]
    dk = dk_tot // GLA_HEADS
    dv_tot = (qkvg.shape[1] - 2 * dk_tot) // 2
    dv = dv_tot // GLA_HEADS
    t = rows.tile
    reverse = direction == 1
    fuse_post = other is not None

    def block(b, n):
        if reverse:
            n = jnp.where(n < rows.nb_ctx, rows.nb_ctx - 1 - n, rows.nb_seq + rows.nb_ctx - 1 - n)
        return b * rows.nb_seq + n

    in_specs = [
        pl.BlockSpec((t, dk_tot), lambda b, n: (block(b, n), 0)),
        pl.BlockSpec((t, dk_tot), lambda b, n: (block(b, n), 1)),
        pl.BlockSpec((t, dv_tot), lambda b, n: (block(b, n), (2 * dk_tot) // dv_tot)),
        pl.BlockSpec((t, LANES), lambda b, n: (block(b, n), direction)),
        pl.BlockSpec((None, LANES, dk_tot), lambda b, n: (direction, 0, 0)),
        pl.BlockSpec((None, 1, dk_tot), lambda b, n: (direction, 0, 0)),
    ]
    args = [qkvg, qkvg, qkvg, r, w2, gate_b]
    if fuse_post:
        in_specs += [
            pl.BlockSpec((t, dv_tot), lambda b, n: (block(b, n), 0)),
            pl.BlockSpec((t, dv_tot), lambda b, n: (block(b, n), (2 * dk_tot + dv_tot) // dv_tot)),
            pl.BlockSpec((1, dv), lambda b, n: (0, 0)),
        ]
        args += [other, qkvg, norm_g.reshape(1, dv).astype(F32)]
    return pl.pallas_call(
        functools.partial(_gla_kernel, reverse=reverse, fuse_post=fuse_post, q_scale=float(dk) ** -0.5, dk=dk, dv=dv),
        out_shape=jax.ShapeDtypeStruct((t_rows, dv_tot), BF16 if fuse_post else F32),
        grid=(rows.batch, rows.nb_seq),
        in_specs=in_specs,
        out_specs=pl.BlockSpec((t, dv_tot), lambda b, n: (block(b, n), 0)),
        scratch_shapes=[pltpu.VMEM((GLA_HEADS, dv, dk), F32)],
        compiler_params=_cparams("parallel", "arbitrary"),
        name="gla_scan_bwd" if reverse else "gla_scan_fwd",
    )(*args)


def _gla_mixer(u, rows, w_in, gate_w1, gate_w2, gate_b, norm_g, w_out):
    d = u.shape[1]
    dk_tot = gate_w2.shape[2]
    qkvg = _matmul(u, w_in.astype(BF16), name="gla_in")
    w1 = jnp.zeros((d, 2 * LANES), F32)
    w2 = jnp.zeros((2, LANES, dk_tot), F32)
    for dd in range(2):
        w1 = w1.at[:, dd * LANES:dd * LANES + GLA_GATE_RANK].set(gate_w1[dd])
        w2 = w2.at[dd, :GLA_GATE_RANK].set(gate_w2[dd])
    r = _matmul(u, w1.astype(BF16), out_dtype=F32, name="gla_gate_in")
    gate_b = gate_b.reshape(2, 1, dk_tot).astype(F32)
    o_fwd = _gla_scan(qkvg, r, w2, gate_b, rows, 0)
    y = _gla_scan(qkvg, r, w2, gate_b, rows, 1, other=o_fwd, norm_g=norm_g)
    return _matmul(y, w_out.astype(BF16), name="gla_out")


def _rope_kernel(x_ref, cos_ref, sin_ref, q_ref, k2_ref, v2_ref, *, q_width, kv_width):
    cos = cos_ref[...]
    sin = sin_ref[...]
    lane = lax.broadcasted_iota(jnp.int32, cos.shape, 1)
    first = (lane % 32) < 16
    low = lane < SWA_HEAD_DIM

    def rope(slab):
        partner = jnp.where(first, pltpu.roll(slab, LANES - 16, 1), pltpu.roll(slab, 16, 1))
        return slab * cos + partner * sin

    scale = SWA_HEAD_DIM ** -0.5
    for s in range(q_width // LANES):
        slab = x_ref[:, s * LANES:(s + 1) * LANES].astype(F32)
        q_ref[:, s * LANES:(s + 1) * LANES] = (rope(slab) * scale).astype(q_ref.dtype)
    for s in range(kv_width // LANES):
        k_slab = rope(x_ref[:, q_width + s * LANES:q_width + (s + 1) * LANES].astype(F32))
        v_slab = x_ref[:, q_width + kv_width + s * LANES:q_width + kv_width + (s + 1) * LANES].astype(F32)
        for slab, out_ref in ((k_slab, k2_ref), (v_slab, v2_ref)):
            swapped = pltpu.roll(slab, SWA_HEAD_DIM, 1)
            out_ref[2 * s] = jnp.where(low, slab, swapped).astype(out_ref.dtype)
            out_ref[2 * s + 1] = jnp.where(low, swapped, slab).astype(out_ref.dtype)


def _rope_tables(n_ctx, seq):
    t = jnp.arange(seq, dtype=jnp.int32)
    row = (t // GRID_W).astype(F32)
    col = (t % GRID_W).astype(F32)
    n_freq = SWA_HEAD_DIM // 4
    inv_freq = ROPE_BASE ** (-jnp.arange(n_freq, dtype=F32) / n_freq)
    lane = jnp.arange(LANES)
    within = lane % SWA_HEAD_DIM
    pos = jnp.where((within < SWA_HEAD_DIM // 2)[None, :], row[:, None], col[:, None])
    ang = pos * inv_freq[lane % n_freq][None, :]
    sign = jnp.where((lane % 32) < 16, -1.0, 1.0)[None, :]
    cos = jnp.concatenate([jnp.ones((n_ctx, LANES), F32), jnp.cos(ang)], axis=0)
    sin = jnp.concatenate([jnp.zeros((n_ctx, LANES), F32), jnp.sin(ang) * sign], axis=0)
    return cos, sin


def _rope_split(qkv, cos, sin, rows, q_width, kv_width):
    t_rows = qkv.shape[0]
    t = rows.tile
    n_kv = kv_width // SWA_HEAD_DIM
    seq_rows = rows.nb_seq * t
    kv_shape = jax.ShapeDtypeStruct((rows.batch, n_kv, seq_rows, LANES), BF16)
    kv_spec = pl.BlockSpec((None, n_kv, t, LANES), lambda i: (i // rows.nb_seq, 0, i % rows.nb_seq, 0))
    return pl.pallas_call(
        functools.partial(_rope_kernel, q_width=q_width, kv_width=kv_width),
        out_shape=[jax.ShapeDtypeStruct((t_rows, q_width), BF16), kv_shape, kv_shape],
        grid=(rows.n_blocks,),
        in_specs=[
            pl.BlockSpec((t, qkv.shape[1]), lambda i: (i, 0)),
            pl.BlockSpec((t, LANES), lambda i: (i % rows.nb_seq, 0)),
            pl.BlockSpec((t, LANES), lambda i: (i % rows.nb_seq, 0)),
        ],
        out_specs=[pl.BlockSpec((t, q_width), lambda i: (i, 0)), kv_spec, kv_spec],
        compiler_params=_cparams("parallel"),
        name="rope_split",
    )(qkv, cos, sin)


def _attend_heads(q_ref, kk, vv, mask, sink_ref, head0, o_ref):
    tq = q_ref.shape[0]
    lane = lax.broadcasted_iota(jnp.int32, (tq, LANES), 1)
    low = lane < SWA_HEAD_DIM
    for pair in range(SWA_GROUP // 2):
        q2 = q_ref[:, pair * LANES:(pair + 1) * LANES]
        outs = []
        for half in range(2):
            qh = jnp.where(low if half == 0 else lane >= SWA_HEAD_DIM, q2, jnp.zeros_like(q2))
            s = _dot_nt(qh, kk)
            if mask is not None:
                s = jnp.where(mask, s, NEG_INF)
            sink = sink_ref[head0 + 2 * pair + half]
            m = jnp.maximum(jnp.max(s, axis=-1, keepdims=True), sink)
            p = jnp.exp(s - m)
            denom = jnp.sum(p, axis=-1, keepdims=True) + jnp.exp(sink - m)
            outs.append(_dot(p.astype(BF16), vv) / denom)
        o_ref[:, pair * LANES:(pair + 1) * LANES] = jnp.where(low, outs[0], outs[1]).astype(o_ref.dtype)


def _attn_kernel(sink_ref, q_ref, kc_ref, vc_ref, kp_ref, vp_ref, km_ref, vm_ref, kn_ref, vn_ref,
                 o_ref, kk_ref, vv_ref, *, n_ctx, nq_ctx, nq_lat):
    n = pl.program_id(2)
    head0 = pl.program_id(1) * SWA_GROUP
    w = WINDOW

    @pl.when(n < nq_ctx)
    def _():
        _attend_heads(q_ref, kc_ref[...], vc_ref[...], None, sink_ref, head0, o_ref)

    @pl.when(n >= nq_ctx)
    def _():
        m = n - nq_ctx
        for ref, parts in ((kk_ref, (kc_ref, kp_ref, km_ref, kn_ref)), (vv_ref, (vc_ref, vp_ref, vm_ref, vn_ref))):
            ref[0:n_ctx] = parts[0][...]
            ref[n_ctx:n_ctx + w] = parts[1][...]
            ref[n_ctx + w:n_ctx + 2 * w] = parts[2][...]
            ref[n_ctx + 2 * w:n_ctx + 3 * w] = parts[3][...]
        nk = n_ctx + 3 * w
        qpos = lax.broadcasted_iota(jnp.int32, (w, nk), 0)
        col = lax.broadcasted_iota(jnp.int32, (w, nk), 1)
        rel = jnp.where(col < n_ctx, 0, col - (n_ctx + w) - qpos)
        far = 4 * w
        no_prev = jnp.where(m > 0, 0, far)
        no_next = jnp.where(m < nq_lat - 1, 0, far)
        missing = jnp.where(col < n_ctx, 0,
                            jnp.where(col < n_ctx + w, no_prev, jnp.where(col >= n_ctx + 2 * w, no_next, 0)))
        mask = jnp.abs(rel) + missing <= w
        _attend_heads(q_ref, kk_ref[...], vv_ref[...], mask, sink_ref, head0, o_ref)


def _window_attention(q, k2, v2, sink, rows, n_ctx):
    t_rows, q_width = q.shape
    batch, n_kv, seq_rows, _ = k2.shape
    w = WINDOW
    gw = SWA_GROUP * SWA_HEAD_DIM
    nq_seq = seq_rows // w
    nq_ctx = n_ctx // w
    nq_lat = nq_seq - nq_ctx
    ctx_kv = pl.BlockSpec((None, None, n_ctx, LANES), lambda b, g, n: (b, g, 0, 0))

    def band(shift):
        def index(b, g, n):
            return (b, g, nq_ctx + jnp.clip(n - nq_ctx + shift, 0, nq_lat - 1), 0)
        return pl.BlockSpec((None, None, w, LANES), index)

    nk = n_ctx + 3 * w
    return pl.pallas_call(
        functools.partial(_attn_kernel, n_ctx=n_ctx, nq_ctx=nq_ctx, nq_lat=nq_lat),
        out_shape=jax.ShapeDtypeStruct((t_rows, q_width), BF16),
        grid=(batch, n_kv, nq_seq),
        in_specs=[pl.BlockSpec(memory_space=pltpu.SMEM),
                  pl.BlockSpec((w, gw), lambda b, g, n: (b * nq_seq + n, g)),
                  ctx_kv, ctx_kv, band(-1), band(-1), band(0), band(0), band(1), band(1)],
        out_specs=pl.BlockSpec((w, gw), lambda b, g, n: (b * nq_seq + n, g)),
        scratch_shapes=[pltpu.VMEM((nk, LANES), BF16), pltpu.VMEM((nk, LANES), BF16)],
        compiler_params=_cparams("parallel", "parallel", "parallel"),
        name="window_attn",
    )(sink, q, k2, v2, k2, v2, k2, v2, k2, v2)


def _window_mixer(u, rows, n_ctx, seq, w_qkv, b_qkv, sink, w_out, b_out):
    q_width = w_out.shape[0]
    kv_width = (w_qkv.shape[1] - q_width) // 2
    qkv = _matmul(u, w_qkv.astype(BF16), bias=b_qkv, name="swa_qkv")
    cos, sin = _rope_tables(n_ctx, seq)
    q, k2, v2 = _rope_split(qkv, cos, sin, rows, q_width, kv_width)
    o = _window_attention(q, k2, v2, sink.astype(F32), rows, n_ctx)
    return _matmul(o, w_out.astype(BF16), bias=b_out, name="swa_out")


def _conv_kernel(gi_ref, go_ref, val_ref, w_ref, o_ref, p_ref, *, n_ctx):
    length = gi_ref.shape[0]
    pad = 8
    p_ref[0:pad] = jnp.zeros((pad, p_ref.shape[1]), F32)
    p_ref[pad + length:pad + length + pad] = jnp.zeros((pad, p_ref.shape[1]), F32)
    p_ref[pad:pad + length] = gi_ref[...].astype(F32) * val_ref[...].astype(F32)
    t = lax.broadcasted_iota(jnp.int32, (length, 1), 0)
    seg_start = t * (t - n_ctx) == 0
    seg_end = (t - (n_ctx - 1)) * (t - (length - 1)) == 0
    prev = jnp.where(seg_start, 0.0, p_ref[pad - 1:pad - 1 + length])
    nxt = jnp.where(seg_end, 0.0, p_ref[pad + 1:pad + 1 + length])
    z = prev * w_ref[0:1, :] + p_ref[pad:pad + length] * w_ref[1:2, :] + nxt * w_ref[2:3, :]
    o_ref[...] = (go_ref[...].astype(F32) * z).astype(o_ref.dtype)


def _conv_gate(proj, conv_w, rows, n_ctx):
    t_rows = proj.shape[0]
    d = proj.shape[1] // 3
    length = rows.nb_seq * rows.tile
    tc = LANES
    nc = d // tc
    return pl.pallas_call(
        functools.partial(_conv_kernel, n_ctx=n_ctx),
        out_shape=jax.ShapeDtypeStruct((t_rows, d), BF16),
        grid=(rows.batch, nc),
        in_specs=[
            pl.BlockSpec((length, tc), lambda b, c: (b, c)),
            pl.BlockSpec((length, tc), lambda b, c: (b, nc + c)),
            pl.BlockSpec((length, tc), lambda b, c: (b, 2 * nc + c)),
            pl.BlockSpec((CONV_WIDTH, tc), lambda b, c: (0, c)),
        ],
        out_specs=pl.BlockSpec((length, tc), lambda b, c: (b, c)),
        scratch_shapes=[pltpu.VMEM((length + 16, tc), F32)],
        compiler_params=_cparams("parallel", "parallel"),
        name="conv_gate",
    )(proj, proj, proj, conv_w.astype(F32))


def _conv_mixer(u, rows, n_ctx, w_in, w_conv, w_out):
    proj = _matmul(u, w_in.astype(BF16), name="conv_in")
    zg = _conv_gate(proj, w_conv, rows, n_ctx)
    return _matmul(zg, w_out.astype(BF16), name="conv_out")


def _row_gather_copy(u_hbm, x_buf, sem, slot, tok, r):
    s = x_buf.shape[1] // MOE_ROW_BLOCK
    src = u_hbm.at[pl.ds(pl.multiple_of(tok * s, s), s)]
    dst_row = r * s if isinstance(r, int) else pl.multiple_of(r * s, s)
    return pltpu.make_async_copy(src, x_buf.at[slot, pl.ds(dst_row, s)], sem.at[slot])


def _moe_kernel(blk_exp_ref, n_used_ref, row_tok_ref, u_hbm, wgu_ref, bgu_ref, wdn_ref, bdn_ref, o_ref,
                x_buf, sem, wgu_bf_ref, wdn_bf_ref):
    i = pl.program_id(0)
    tm = o_ref.shape[0]
    n_used = n_used_ref[0]
    used = i < n_used
    slot = lax.rem(i, 2)

    n_slabs = x_buf.shape[1] // tm

    def wait_gather():
        pltpu.make_async_copy(u_hbm.at[pl.ds(0, tm * n_slabs)], x_buf.at[slot], sem.at[slot]).wait()

    @pl.when(used & (i == 0))
    def _():
        def body(r, carry):
            _row_gather_copy(u_hbm, x_buf, sem, 0, row_tok_ref[r], r).start()
            return carry
        lax.fori_loop(0, tm, body, 0, unroll=8)

    new_expert = (i == 0) | (blk_exp_ref[i] != blk_exp_ref[jnp.maximum(i - 1, 0)])

    @pl.when(used & new_expert)
    def _():
        wgu_bf_ref[...] = wgu_ref[...].astype(BF16)
        wdn_bf_ref[...] = wdn_ref[...].astype(BF16)

    @pl.when(used)
    def _():
        wait_gather()
        x = jnp.concatenate([x_buf[slot, pl.ds(s, tm, stride=n_slabs), :].astype(BF16) for s in range(n_slabs)],
                            axis=1)
        base = (i + 1) * tm
        for r in range(tm):
            _row_gather_copy(u_hbm, x_buf, sem, 1 - slot, row_tok_ref[base + r], r).start()
        ff = wdn_ref.shape[0]
        gu = _dot(x, wgu_bf_ref[...]) + bgu_ref[...]
        glu = jnp.minimum(gu[:, :ff], SWIGLU_LIMIT)
        lin = jnp.clip(gu[:, ff:], -SWIGLU_LIMIT, SWIGLU_LIMIT)
        act = glu * _sigmoid(SWIGLU_ALPHA * glu) * (lin + 1.0)
        o_ref[...] = (_dot(act.astype(BF16), wdn_bf_ref[...]) + bdn_ref[...]).astype(o_ref.dtype)

    @pl.when((i == n_used) & (i > 0))
    def _():
        wait_gather()

    @pl.when(i >= n_used)
    def _():
        o_ref[...] = jnp.zeros_like(o_ref)


def _expert_mlp(u, row_tok, blk_exp, n_used, layer, w_gate_up, b_gate_up, w_down, b_down):
    n_rows = row_tok.shape[0]
    depth, n_exp, d, ff2 = w_gate_up.shape
    ff = ff2 // 2
    tm = MOE_ROW_BLOCK
    grid_spec = pltpu.PrefetchScalarGridSpec(
        num_scalar_prefetch=3,
        grid=(n_rows // tm,),
        in_specs=[
            pl.BlockSpec(memory_space=pl.ANY),
            pl.BlockSpec((None, None, d, ff2), lambda i, be, nu, rt: (layer, be[i], 0, 0)),
            pl.BlockSpec((None, None, 1, ff2), lambda i, be, nu, rt: (layer, be[i], 0, 0)),
            pl.BlockSpec((None, None, ff, d), lambda i, be, nu, rt: (layer, be[i], 0, 0)),
            pl.BlockSpec((None, None, 1, d), lambda i, be, nu, rt: (layer, be[i], 0, 0)),
        ],
        out_specs=pl.BlockSpec((tm, d), lambda i, be, nu, rt: (i, 0)),
        scratch_shapes=[pltpu.VMEM((2, tm * (d // LANES), LANES), F32), pltpu.SemaphoreType.DMA((2,)),
                        pltpu.VMEM((d, ff2), BF16), pltpu.VMEM((ff, d), BF16)],
    )
    return pl.pallas_call(
        _moe_kernel,
        out_shape=jax.ShapeDtypeStruct((n_rows, d), BF16),
        grid_spec=grid_spec,
        compiler_params=pltpu.CompilerParams(dimension_semantics=("arbitrary",), vmem_limit_bytes=MOE_VMEM_LIMIT),
        name="expert_mlp",
    )(blk_exp, n_used, row_tok, u, w_gate_up, b_gate_up.reshape(depth, n_exp, 1, ff2),
      w_down, b_down.reshape(depth, n_exp, 1, d))


def _moe(u, route_idx, counts, layer, w_gate_up, b_gate_up, w_down, b_down):
    n_tok = route_idx.shape[0]
    d = w_down.shape[-1]
    tm = MOE_ROW_BLOCK
    n_assign = n_tok * TOP_K
    n_blocks = n_assign // tm + N_EXPERTS
    n_rows = n_blocks * tm
    top_exp = route_idx[:, :TOP_K]
    rank = route_idx[:, TOP_K:2 * TOP_K]
    counts = counts[0, :N_EXPERTS].astype(jnp.int32)
    padded = (counts + tm - 1) // tm * tm
    start = jnp.cumsum(counts) - counts
    pad_end = jnp.cumsum(padded)
    pad_start = pad_end - padded
    blk_start = jnp.arange(n_blocks, dtype=jnp.int32) * tm
    blk_exp = jnp.minimum(jnp.sum(pad_end[None, :] <= blk_start[:, None], axis=1), N_EXPERTS - 1).astype(jnp.int32)
    n_used = (pad_end[-1:] // tm).astype(jnp.int32)
    order = jnp.argsort(top_exp.reshape(-1), stable=True).astype(jnp.int32)
    row_exp = jnp.repeat(blk_exp, tm)
    within = jnp.arange(n_rows, dtype=jnp.int32) - pad_start[row_exp]
    src = jnp.clip(start[row_exp] + within, 0, n_assign - 1)
    row_tok = jnp.where(within < counts[row_exp], order[src] // TOP_K, 0)
    y_rows = _expert_mlp(u, row_tok, blk_exp, n_used, layer, w_gate_up, b_gate_up, w_down, b_down)
    pos = pad_start[top_exp] + rank
    return y_rows[pos.T.reshape(-1)].reshape(TOP_K, n_tok, d)


def kernel(x, c, ctx, c_ctx, ada_w, ada_b, ln_g, ln_b, gla_w_in, gla_gate_w1, gla_gate_w2, gla_gate_b, gla_norm_g, gla_w_out, swa_w_qkv, swa_b_qkv, swa_sink, swa_w_out, swa_b_out, conv_w_in, conv_w, conv_w_out, moe_router_w, moe_router_b, moe_w_gate_up, moe_b_gate_up, moe_w_down, moe_b_down):
    batch, seq, d = x.shape
    n_ctx = ctx.shape[1]
    depth = ada_w.shape[0]
    alpha = (2 * depth) ** 0.25
    rows_full = _Rows(batch, n_ctx // ROW_TILE, seq // ROW_TILE)
    rows_lat = _Rows(batch, 0, seq // ROW_TILE)

    cond = jnp.zeros((8, d), F32).at[0].set(c_ctx).at[1:1 + batch].set(c)
    ada = _ada_mods(cond, ada_w, ada_b)
    mods_ctx = jnp.broadcast_to(ada[:, 0:1], (depth, batch, N_MOD * d))
    mods = jnp.stack([mods_ctx, ada[:, 1:1 + batch]], axis=2).reshape(depth, batch, 2, N_MOD, d)

    router_w = jnp.zeros((depth, d, LANES), F32).at[:, :, :N_EXPERTS].set(moe_router_w)
    router_b = jnp.zeros((depth, 1, LANES), F32).at[:, 0, :N_EXPERTS].set(moe_router_b)

    h = jnp.concatenate([ctx, x], axis=1).reshape(rows_full.n_rows, d)
    u = _modulate(h, mods[0], rows_full, 0, 1)
    rows = rows_full
    for i in range(depth):
        last = i == depth - 1
        kind, j = i % 3, i // 3
        if kind == 0:
            y = _gla_mixer(u, rows, gla_w_in[j], gla_gate_w1[j], gla_gate_w2[j], gla_gate_b[j], gla_norm_g[j], gla_w_out[j])
        elif kind == 1:
            y = _window_mixer(u, rows, n_ctx, seq, swa_w_qkv[j], swa_b_qkv[j], swa_sink[j], swa_w_out[j], swa_b_out[j])
        else:
            y = _conv_mixer(u, rows, n_ctx, conv_w_in[j], conv_w[j], conv_w_out[j])
        rows_out = rows_lat if last else rows
        h, u, route_idx, route_gate, counts = _resid_ln(
            h, y, mods[i], mods[i], ln_g[i, 0], ln_b[i, 0], rows, rows_out, alpha=alpha, gate_idx=2,
            next_mod=(3, 4), router=(router_w[i], router_b[i]))
        rows = rows_out
        y = _moe(u, route_idx, counts, i, moe_w_gate_up, moe_b_gate_up, moe_w_down, moe_b_down)
        if last:
            (h,) = _resid_ln(h, y, mods[i], mods[i], ln_g[i, 1], ln_b[i, 1], rows, rows, alpha=alpha, gate_idx=5,
                             combine_gates=route_gate)
        else:
            h, u = _resid_ln(h, y, mods[i], mods[i + 1], ln_g[i, 1], ln_b[i, 1], rows, rows, alpha=alpha, gate_idx=5,
                             next_mod=(0, 1), combine_gates=route_gate)
    return h.reshape(batch, seq, d)
```

```python
import functools

import jax
import jax.numpy as jnp
from jax import lax
from jax.experimental import pallas as pl
from jax.experimental.pallas import tpu as pltpu

F32 = jnp.float32
BF16 = jnp.bfloat16
HIGHEST = lax.Precision.HIGHEST

GRID_W = 64
GLA_HEADS = 4
GLA_GATE_RANK = 16
GLA_GATE_NORM = 16.0
GLA_CHUNK = 64
SWA_HEAD_DIM = 64
SWA_GROUP = 8
WINDOW = 128
ROPE_BASE = 10000.0
CONV_WIDTH = 3
N_EXPERTS = 32
TOP_K = 4
SWIGLU_LIMIT = 7.0
SWIGLU_ALPHA = 1.702
N_MOD = 6
LN_EPS = 1e-5
NEG_INF = -1e30

LANES = 128
ROW_TILE = 256
MOE_ROW_BLOCK = 256
VMEM_LIMIT = 48 * 1024 * 1024
MOE_VMEM_LIMIT = 60 * 1024 * 1024


def _cparams(*sem):
    return pltpu.CompilerParams(dimension_semantics=sem, vmem_limit_bytes=VMEM_LIMIT)


def _sigmoid(x):
    return 1.0 / (1.0 + jnp.exp(-x))


def _dot(a, b):
    return jnp.dot(a, b, preferred_element_type=F32)


def _dot_nt(a, b):
    return lax.dot_general(a, b, (((1,), (1,)), ((), ())), preferred_element_type=F32)


def _split_bf16(x):
    hi = x.astype(BF16)
    lo = (x - hi.astype(F32)).astype(BF16)
    return hi, lo


class _Rows:
    def __init__(self, batch, nb_ctx, nb_lat, tile=ROW_TILE):
        self.batch, self.nb_ctx, self.nb_lat, self.tile = batch, nb_ctx, nb_lat, tile
        self.nb_seq = nb_ctx + nb_lat
        self.n_blocks = batch * self.nb_seq
        self.n_rows = self.n_blocks * tile

    def batch_of(self, i):
        return i // self.nb_seq

    def seg_of(self, i):
        if self.nb_ctx == 0:
            return 1
        return jnp.where(i % self.nb_seq >= self.nb_ctx, 1, 0)

    def latent_block(self, i):
        return (i // self.nb_lat) * self.nb_seq + self.nb_ctx + i % self.nb_lat


def _ada_kernel(c_ref, w_ref, b_ref, o_ref):
    c = c_ref[...]
    s = c * _sigmoid(c)
    o_ref[...] = jnp.dot(s, w_ref[...], precision=HIGHEST, preferred_element_type=F32) + b_ref[...]


def _ada_mods(cond, ada_w, ada_b):
    depth, d, n = ada_w.shape
    tn = 1024 if n % 1024 == 0 else n
    return pl.pallas_call(
        _ada_kernel,
        out_shape=jax.ShapeDtypeStruct((depth, 8, n), F32),
        grid=(depth, n // tn),
        in_specs=[
            pl.BlockSpec((8, d), lambda l, j: (0, 0)),
            pl.BlockSpec((None, d, tn), lambda l, j: (l, 0, j)),
            pl.BlockSpec((None, 1, tn), lambda l, j: (l, 0, j)),
        ],
        out_specs=pl.BlockSpec((None, 8, tn), lambda l, j: (l, 0, j)),
        compiler_params=_cparams("parallel", "parallel"),
        name="ada_mods",
    )(cond, ada_w, ada_b.reshape(depth, 1, n))


def _modulate_kernel(h_ref, m_ref, u_ref, *, shift_idx, scale_idx):
    h = h_ref[...]
    u = h * (1.0 + m_ref[scale_idx:scale_idx + 1, :]) + m_ref[shift_idx:shift_idx + 1, :]
    u_ref[...] = u.astype(u_ref.dtype)


def _modulate(h, mods, rows, shift_idx, scale_idx):
    d = h.shape[1]
    t = rows.tile
    return pl.pallas_call(
        functools.partial(_modulate_kernel, shift_idx=shift_idx, scale_idx=scale_idx),
        out_shape=jax.ShapeDtypeStruct(h.shape, BF16),
        grid=(rows.n_blocks,),
        in_specs=[
            pl.BlockSpec((t, d), lambda i: (i, 0)),
            pl.BlockSpec((None, None, N_MOD, d), lambda i: (rows.batch_of(i), rows.seg_of(i), 0, 0)),
        ],
        out_specs=pl.BlockSpec((t, d), lambda i: (i, 0)),
        compiler_params=_cparams("parallel"),
        name="modulate",
    )(h, mods)


def _mm_kernel(x_ref, w_ref, o_ref):
    o_ref[...] = _dot(x_ref[...], w_ref[...]).astype(o_ref.dtype)


def _mm_bias_kernel(x_ref, w_ref, b_ref, o_ref):
    o_ref[...] = (_dot(x_ref[...], w_ref[...]) + b_ref[...]).astype(o_ref.dtype)


def _pick_tile(n, candidates):
    for c in candidates:
        if n % c == 0:
            return c
    return n


def _matmul(x, w, bias=None, out_dtype=BF16, name="proj"):
    m, k = x.shape
    n = w.shape[1]
    tm = _pick_tile(m, (1024, 512, 256))
    tn = _pick_tile(n, (1024, 768, 640, 512, 256, 128))
    in_specs = [
        pl.BlockSpec((tm, k), lambda i, j: (i, 0)),
        pl.BlockSpec((k, tn), lambda i, j: (0, j)),
    ]
    args = [x, w]
    body = _mm_kernel
    if bias is not None:
        in_specs.append(pl.BlockSpec((1, tn), lambda i, j: (0, j)))
        args.append(bias.reshape(1, n).astype(F32))
        body = _mm_bias_kernel
    return pl.pallas_call(
        body,
        out_shape=jax.ShapeDtypeStruct((m, n), out_dtype),
        grid=(m // tm, n // tn),
        in_specs=in_specs,
        out_specs=pl.BlockSpec((tm, tn), lambda i, j: (i, j)),
        compiler_params=_cparams("parallel", "parallel"),
        name=name,
    )(*args)


def _route_block(logits, count_ref):
    t = logits.shape[0]
    lane = lax.broadcasted_iota(jnp.int32, (t, LANES), 1)
    lane_f = lane.astype(F32)
    left = jnp.where(lane < N_EXPERTS, logits, NEG_INF)
    hot = jnp.zeros((t, LANES), F32)
    sels, vals = [], []
    for _ in range(TOP_K):
        m = jnp.max(left, axis=-1, keepdims=True)
        sel = jnp.min(jnp.where(left == m, lane_f, float(LANES)), axis=-1, keepdims=True)
        pick = lane_f == sel
        left = jnp.where(pick, NEG_INF, left)
        hot = jnp.where(pick, 1.0, hot)
        sels.append(sel)
        vals.append(m)
    row = lax.broadcasted_iota(jnp.int32, (t, t), 0)
    col = lax.broadcasted_iota(jnp.int32, (t, t), 1)
    earlier = jnp.where(row > col, 1.0, 0.0).astype(BF16)
    before = _dot(earlier, hot.astype(BF16)) + count_ref[...]
    count_ref[...] += jnp.sum(hot, axis=0, keepdims=True)
    exps = [jnp.exp(v - vals[0]) for v in vals]
    denom = exps[0]
    for e in exps[1:]:
        denom = denom + e
    idx = jnp.zeros((t, LANES), jnp.int32)
    gates = jnp.zeros((t, LANES), F32)
    for k in range(TOP_K):
        rank = jnp.sum(jnp.where(lane_f == sels[k], before, 0.0), axis=-1, keepdims=True)
        idx = jnp.where(lane == k, sels[k].astype(jnp.int32), idx)
        idx = jnp.where(lane == TOP_K + k, rank.astype(jnp.int32), idx)
        gates = jnp.where(lane == k, exps[k] / denom, gates)
    return idx, gates


def _resid_ln_kernel(*refs, alpha, gate_idx, shift_idx, scale_idx, with_u, with_router, combine):
    h_ref, y_ref = refs[:2]
    pos = 2
    if combine:
        yg_ref = refs[pos]
        pos += 1
    m_ref, mn_ref, g_ref, b_ref = refs[pos:pos + 4]
    pos += 4
    if with_router:
        wr_hi_ref, wr_lo_ref, br_ref = refs[pos:pos + 3]
        pos += 3
    hn_ref = refs[pos]
    pos += 1
    if combine:
        gates = yg_ref[...]
        y = gates[:, 0:1] * y_ref[0].astype(F32)
        for k in range(1, TOP_K):
            y = y + gates[:, k:k + 1] * y_ref[k].astype(F32)
    else:
        y = y_ref[...].astype(F32)
    z = alpha * h_ref[...] + y * m_ref[gate_idx:gate_idx + 1, :]
    zc = z - jnp.mean(z, axis=-1, keepdims=True)
    var = jnp.mean(zc * zc, axis=-1, keepdims=True)
    hn = zc * lax.rsqrt(var + LN_EPS) * g_ref[...] + b_ref[...]
    hn_ref[...] = hn
    if with_u:
        u_ref = refs[pos]
        pos += 1
        u = hn * (1.0 + mn_ref[scale_idx:scale_idx + 1, :]) + mn_ref[shift_idx:shift_idx + 1, :]
        u_ref[...] = u.astype(u_ref.dtype)
        if with_router:
            idx_ref, gate_ref, cnt_ref, count_ref = refs[pos:pos + 4]

            @pl.when(pl.program_id(0) == 0)
            def _():
                count_ref[...] = jnp.zeros_like(count_ref)

            u_hi, u_lo = _split_bf16(u)
            w_hi = wr_hi_ref[...]
            logits = _dot(u_hi, w_hi) + _dot(u_lo, w_hi) + _dot(u_hi, wr_lo_ref[...]) + br_ref[...]
            idx, gates = _route_block(logits, count_ref)
            idx_ref[...] = idx
            gate_ref[...] = gates
            cnt_ref[...] = count_ref[...]


def _resid_ln(h, y, mods, mods_next, ln_g, ln_b, rows_in, rows_out, *, alpha, gate_idx, next_mod=None, router=None,
              combine_gates=None):
    d = h.shape[1]
    t = rows_out.tile
    if rows_in is rows_out:
        in_row = lambda i: i
    else:
        in_row = rows_in.latent_block
    with_u = next_mod is not None
    with_router = router is not None
    combine = combine_gates is not None
    shift_idx, scale_idx = next_mod if with_u else (0, 0)
    mod_spec = pl.BlockSpec((None, None, N_MOD, d), lambda i: (rows_out.batch_of(i), rows_out.seg_of(i), 0, 0))
    in_specs = [pl.BlockSpec((t, d), lambda i: (in_row(i), 0))]
    args = [h, y]
    if combine:
        in_specs += [pl.BlockSpec((TOP_K, t, d), lambda i: (0, in_row(i), 0)),
                     pl.BlockSpec((t, LANES), lambda i: (in_row(i), 0))]
        args.append(combine_gates)
    else:
        in_specs.append(pl.BlockSpec((t, d), lambda i: (in_row(i), 0)))
    in_specs += [mod_spec, mod_spec, pl.BlockSpec((1, d), lambda i: (0, 0)), pl.BlockSpec((1, d), lambda i: (0, 0))]
    args += [mods, mods_next, ln_g.reshape(1, d), ln_b.reshape(1, d)]
    out_shape = [jax.ShapeDtypeStruct((rows_out.n_rows, d), F32)]
    out_specs = [pl.BlockSpec((t, d), lambda i: (i, 0))]
    scratch = []
    if with_router:
        wr, br = router
        w_spec = pl.BlockSpec((d, LANES), lambda i: (0, 0))
        in_specs += [w_spec, w_spec, pl.BlockSpec((1, LANES), lambda i: (0, 0))]
        args += [*_split_bf16(wr), br]
    if with_u:
        out_shape.append(jax.ShapeDtypeStruct((rows_out.n_rows, d), F32 if with_router else BF16))
        out_specs.append(pl.BlockSpec((t, d), lambda i: (i, 0)))
    if with_router:
        out_shape += [jax.ShapeDtypeStruct((rows_out.n_rows, LANES), jnp.int32),
                      jax.ShapeDtypeStruct((rows_out.n_rows, LANES), F32),
                      jax.ShapeDtypeStruct((1, LANES), F32)]
        out_specs += [pl.BlockSpec((t, LANES), lambda i: (i, 0)), pl.BlockSpec((t, LANES), lambda i: (i, 0)),
                      pl.BlockSpec((1, LANES), lambda i: (0, 0))]
        scratch.append(pltpu.VMEM((1, LANES), F32))
    return pl.pallas_call(
        functools.partial(_resid_ln_kernel, alpha=alpha, gate_idx=gate_idx, shift_idx=shift_idx,
                          scale_idx=scale_idx, with_u=with_u, with_router=with_router, combine=combine),
        out_shape=out_shape,
        grid=(rows_out.n_blocks,),
        in_specs=in_specs,
        out_specs=out_specs,
        scratch_shapes=scratch,
        compiler_params=_cparams("arbitrary" if with_router else "parallel"),
        name="resid_ln",
    )(*args)


def _gla_kernel(*refs, reverse, fuse_post, q_scale, dk, dv):
    if fuse_post:
        q_ref, k_ref, v_ref, r_ref, w2_ref, gb_ref, other_ref, g_ref, ng_ref, out_ref, state_ref = refs
    else:
        q_ref, k_ref, v_ref, r_ref, w2_ref, gb_ref, out_ref, state_ref = refs

    @pl.when(pl.program_id(1) == 0)
    def _():
        state_ref[...] = jnp.zeros_like(state_ref)

    t = q_ref.shape[0]
    c = GLA_CHUNK
    shift = c.bit_length() - 1
    r_hi, r_lo = _split_bf16(r_ref[...])
    w_hi, w_lo = _split_bf16(w2_ref[...])
    z = _dot(r_hi, w_hi) + _dot(r_lo, w_hi) + _dot(r_hi, w_lo) + gb_ref[...]
    log_a = (jnp.minimum(z, 0.0) - jnp.log(1.0 + jnp.exp(-jnp.abs(z)))) * (1.0 / GLA_GATE_NORM)
    row = lax.broadcasted_iota(jnp.int32, (t, t), 0)
    col = lax.broadcasted_iota(jnp.int32, (t, t), 1)
    same_chunk = lax.shift_right_logical(row, shift) == lax.shift_right_logical(col, shift)
    ahead = (col - row) if reverse else (row - col)
    tri = jnp.where(jnp.where(same_chunk, ahead, -1) >= 0, 1.0, 0.0).astype(BF16)
    a_hi, a_lo = _split_bf16(log_a)
    cum = _dot(tri, a_hi) + _dot(tri, a_lo)
    grow = jnp.exp(cum)
    shrink = jnp.exp(-cum)
    crow = lax.broadcasted_iota(jnp.int32, (c, c), 0)
    ccol = lax.broadcasted_iota(jnp.int32, (c, c), 1)
    causal = (ccol >= crow) if reverse else (crow >= ccol)
    n_chunks = t // c
    for h in range(GLA_HEADS):
        ks = slice(h * dk, (h + 1) * dk)
        vs = slice(h * dv, (h + 1) * dv)
        q_dec = (q_ref[:, ks].astype(F32) * q_scale * grow[:, ks]).astype(BF16)
        k_in = k_ref[:, ks].astype(F32) * shrink[:, ks]
        for step in range(n_chunks):
            j = n_chunks - 1 - step if reverse else step
            rs = slice(j * c, (j + 1) * c)
            edge = j * c if reverse else (j + 1) * c - 1
            decay = jnp.exp(cum[edge:edge + 1, ks])
            qd = q_dec[rs]
            k_intra = k_in[rs].astype(BF16)
            k_state = (k_in[rs] * decay).astype(BF16)
            v = v_ref[rs, vs]
            scores = jnp.where(causal, _dot_nt(qd, k_intra), 0.0)
            state_t = state_ref[h]
            o = _dot(scores.astype(BF16), v) + _dot_nt(qd, state_t.astype(BF16))
            kv_t = lax.dot_general(v, k_state, (((0,), (0,)), ((), ())), preferred_element_type=F32)
            state_ref[h] = state_t * decay + kv_t
            if fuse_post:
                o = o + other_ref[rs, vs]
                o = o * lax.rsqrt(jnp.mean(o * o, axis=-1, keepdims=True) + LN_EPS) * ng_ref[...]
                g = g_ref[rs, vs].astype(F32)
                out_ref[rs, vs] = (o * (g * _sigmoid(g))).astype(out_ref.dtype)
            else:
                out_ref[rs, vs] = o


def _gla_scan(qkvg, r, w2, gate_b, rows, direction, other=None, norm_g=None):
    t_rows = qkvg.shape[0]
    dk_tot = w2.shape[2]
    dk = dk_tot // GLA_HEADS
    dv_tot = (qkvg.shape[1] - 2 * dk_tot) // 2
    dv = dv_tot // GLA_HEADS
    t = rows.tile
    reverse = direction == 1
    fuse_post = other is not None

    def block(b, n):
        if reverse:
            n = jnp.where(n < rows.nb_ctx, rows.nb_ctx - 1 - n, rows.nb_seq + rows.nb_ctx - 1 - n)
        return b * rows.nb_seq + n

    in_specs = [
        pl.BlockSpec((t, dk_tot), lambda b, n: (block(b, n), 0)),
        pl.BlockSpec((t, dk_tot), lambda b, n: (block(b, n), 1)),
        pl.BlockSpec((t, dv_tot), lambda b, n: (block(b, n), (2 * dk_tot) // dv_tot)),
        pl.BlockSpec((t, LANES), lambda b, n: (block(b, n), direction)),
        pl.BlockSpec((None, LANES, dk_tot), lambda b, n: (direction, 0, 0)),
        pl.BlockSpec((None, 1, dk_tot), lambda b, n: (direction, 0, 0)),
    ]
    args = [qkvg, qkvg, qkvg, r, w2, gate_b]
    if fuse_post:
        in_specs += [
            pl.BlockSpec((t, dv_tot), lambda b, n: (block(b, n), 0)),
            pl.BlockSpec((t, dv_tot), lambda b, n: (block(b, n), (2 * dk_tot + dv_tot) // dv_tot)),
            pl.BlockSpec((1, dv), lambda b, n: (0, 0)),
        ]
        args += [other, qkvg, norm_g.reshape(1, dv).astype(F32)]
    return pl.pallas_call(
        functools.partial(_gla_kernel, reverse=reverse, fuse_post=fuse_post, q_scale=float(dk) ** -0.5, dk=dk, dv=dv),
        out_shape=jax.ShapeDtypeStruct((t_rows, dv_tot), BF16 if fuse_post else F32),
        grid=(rows.batch, rows.nb_seq),
        in_specs=in_specs,
        out_specs=pl.BlockSpec((t, dv_tot), lambda b, n: (block(b, n), 0)),
        scratch_shapes=[pltpu.VMEM((GLA_HEADS, dv, dk), F32)],
        compiler_params=_cparams("parallel", "arbitrary"),
        name="gla_scan_bwd" if reverse else "gla_scan_fwd",
    )(*args)


def _gla_mixer(u, rows, w_in, gate_w1, gate_w2, gate_b, norm_g, w_out):
    d = u.shape[1]
    dk_tot = gate_w2.shape[2]
    qkvg = _matmul(u, w_in.astype(BF16), name="gla_in")
    w1 = jnp.zeros((d, 2 * LANES), F32)
    w2 = jnp.zeros((2, LANES, dk_tot), F32)
    for dd in range(2):
        w1 = w1.at[:, dd * LANES:dd * LANES + GLA_GATE_RANK].set(gate_w1[dd])
        w2 = w2.at[dd, :GLA_GATE_RANK].set(gate_w2[dd])
    r = _matmul(u, w1.astype(BF16), out_dtype=F32, name="gla_gate_in")
    gate_b = gate_b.reshape(2, 1, dk_tot).astype(F32)
    o_fwd = _gla_scan(qkvg, r, w2, gate_b, rows, 0)
    y = _gla_scan(qkvg, r, w2, gate_b, rows, 1, other=o_fwd, norm_g=norm_g)
    return _matmul(y, w_out.astype(BF16), name="gla_out")


def _rope_kernel(x_ref, cos_ref, sin_ref, q_ref, k2_ref, v2_ref, *, q_width, kv_width):
    cos = cos_ref[...]
    sin = sin_ref[...]
    lane = lax.broadcasted_iota(jnp.int32, cos.shape, 1)
    first = (lane % 32) < 16
    low = lane < SWA_HEAD_DIM

    def rope(slab):
        partner = jnp.where(first, pltpu.roll(slab, LANES - 16, 1), pltpu.roll(slab, 16, 1))
        return slab * cos + partner * sin

    scale = SWA_HEAD_DIM ** -0.5
    for s in range(q_width // LANES):
        slab = x_ref[:, s * LANES:(s + 1) * LANES].astype(F32)
        q_ref[:, s * LANES:(s + 1) * LANES] = (rope(slab) * scale).astype(q_ref.dtype)
    for s in range(kv_width // LANES):
        k_slab = rope(x_ref[:, q_width + s * LANES:q_width + (s + 1) * LANES].astype(F32))
        v_slab = x_ref[:, q_width + kv_width + s * LANES:q_width + kv_width + (s + 1) * LANES].astype(F32)
        for slab, out_ref in ((k_slab, k2_ref), (v_slab, v2_ref)):
            swapped = pltpu.roll(slab, SWA_HEAD_DIM, 1)
            out_ref[2 * s] = jnp.where(low, slab, swapped).astype(out_ref.dtype)
            out_ref[2 * s + 1] = jnp.where(low, swapped, slab).astype(out_ref.dtype)


def _rope_tables(n_ctx, seq):
    t = jnp.arange(seq, dtype=jnp.int32)
    row = (t // GRID_W).astype(F32)
    col = (t % GRID_W).astype(F32)
    n_freq = SWA_HEAD_DIM // 4
    inv_freq = ROPE_BASE ** (-jnp.arange(n_freq, dtype=F32) / n_freq)
    lane = jnp.arange(LANES)
    within = lane % SWA_HEAD_DIM
    pos = jnp.where((within < SWA_HEAD_DIM // 2)[None, :], row[:, None], col[:, None])
    ang = pos * inv_freq[lane % n_freq][None, :]
    sign = jnp.where((lane % 32) < 16, -1.0, 1.0)[None, :]
    cos = jnp.concatenate([jnp.ones((n_ctx, LANES), F32), jnp.cos(ang)], axis=0)
    sin = jnp.concatenate([jnp.zeros((n_ctx, LANES), F32), jnp.sin(ang) * sign], axis=0)
    return cos, sin


def _rope_split(qkv, cos, sin, rows, q_width, kv_width):
    t_rows = qkv.shape[0]
    t = rows.tile
    n_kv = kv_width // SWA_HEAD_DIM
    seq_rows = rows.nb_seq * t
    kv_shape = jax.ShapeDtypeStruct((rows.batch, n_kv, seq_rows, LANES), BF16)
    kv_spec = pl.BlockSpec((None, n_kv, t, LANES), lambda i: (i // rows.nb_seq, 0, i % rows.nb_seq, 0))
    return pl.pallas_call(
        functools.partial(_rope_kernel, q_width=q_width, kv_width=kv_width),
        out_shape=[jax.ShapeDtypeStruct((t_rows, q_width), BF16), kv_shape, kv_shape],
        grid=(rows.n_blocks,),
        in_specs=[
            pl.BlockSpec((t, qkv.shape[1]), lambda i: (i, 0)),
            pl.BlockSpec((t, LANES), lambda i: (i % rows.nb_seq, 0)),
            pl.BlockSpec((t, LANES), lambda i: (i % rows.nb_seq, 0)),
        ],
        out_specs=[pl.BlockSpec((t, q_width), lambda i: (i, 0)), kv_spec, kv_spec],
        compiler_params=_cparams("parallel"),
        name="rope_split",
    )(qkv, cos, sin)


def _attend_heads(q_ref, kk, vv, mask, sink_ref, head0, o_ref):
    tq = q_ref.shape[0]
    lane = lax.broadcasted_iota(jnp.int32, (tq, LANES), 1)
    low = lane < SWA_HEAD_DIM
    for pair in range(SWA_GROUP // 2):
        q2 = q_ref[:, pair * LANES:(pair + 1) * LANES]
        outs = []
        for half in range(2):
            qh = jnp.where(low if half == 0 else lane >= SWA_HEAD_DIM, q2, jnp.zeros_like(q2))
            s = _dot_nt(qh, kk)
            if mask is not None:
                s = jnp.where(mask, s, NEG_INF)
            sink = sink_ref[head0 + 2 * pair + half]
            m = jnp.maximum(jnp.max(s, axis=-1, keepdims=True), sink)
            p = jnp.exp(s - m)
            denom = jnp.sum(p, axis=-1, keepdims=True) + jnp.exp(sink - m)
            outs.append(_dot(p.astype(BF16), vv) / denom)
        o_ref[:, pair * LANES:(pair + 1) * LANES] = jnp.where(low, outs[0], outs[1]).astype(o_ref.dtype)


def _attn_kernel(sink_ref, q_ref, kc_ref, vc_ref, kp_ref, vp_ref, km_ref, vm_ref, kn_ref, vn_ref,
                 o_ref, kk_ref, vv_ref, *, n_ctx, nq_ctx, nq_lat):
    n = pl.program_id(2)
    head0 = pl.program_id(1) * SWA_GROUP
    w = WINDOW

    @pl.when(n < nq_ctx)
    def _():
        _attend_heads(q_ref, kc_ref[...], vc_ref[...], None, sink_ref, head0, o_ref)

    @pl.when(n >= nq_ctx)
    def _():
        m = n - nq_ctx
        for ref, parts in ((kk_ref, (kc_ref, kp_ref, km_ref, kn_ref)), (vv_ref, (vc_ref, vp_ref, vm_ref, vn_ref))):
            ref[0:n_ctx] = parts[0][...]
            ref[n_ctx:n_ctx + w] = parts[1][...]
            ref[n_ctx + w:n_ctx + 2 * w] = parts[2][...]
            ref[n_ctx + 2 * w:n_ctx + 3 * w] = parts[3][...]
        nk = n_ctx + 3 * w
        qpos = lax.broadcasted_iota(jnp.int32, (w, nk), 0)
        col = lax.broadcasted_iota(jnp.int32, (w, nk), 1)
        rel = jnp.where(col < n_ctx, 0, col - (n_ctx + w) - qpos)
        far = 4 * w
        no_prev = jnp.where(m > 0, 0, far)
        no_next = jnp.where(m < nq_lat - 1, 0, far)
        missing = jnp.where(col < n_ctx, 0,
                            jnp.where(col < n_ctx + w, no_prev, jnp.where(col >= n_ctx + 2 * w, no_next, 0)))
        mask = jnp.abs(rel) + missing <= w
        _attend_heads(q_ref, kk_ref[...], vv_ref[...], mask, sink_ref, head0, o_ref)


def _window_attention(q, k2, v2, sink, rows, n_ctx):
    t_rows, q_width = q.shape
    batch, n_kv, seq_rows, _ = k2.shape
    w = WINDOW
    gw = SWA_GROUP * SWA_HEAD_DIM
    nq_seq = seq_rows // w
    nq_ctx = n_ctx // w
    nq_lat = nq_seq - nq_ctx
    ctx_kv = pl.BlockSpec((None, None, n_ctx, LANES), lambda b, g, n: (b, g, 0, 0))

    def band(shift):
        def index(b, g, n):
            return (b, g, nq_ctx + jnp.clip(n - nq_ctx + shift, 0, nq_lat - 1), 0)
        return pl.BlockSpec((None, None, w, LANES), index)

    nk = n_ctx + 3 * w
    return pl.pallas_call(
        functools.partial(_attn_kernel, n_ctx=n_ctx, nq_ctx=nq_ctx, nq_lat=nq_lat),
        out_shape=jax.ShapeDtypeStruct((t_rows, q_width), BF16),
        grid=(batch, n_kv, nq_seq),
        in_specs=[pl.BlockSpec(memory_space=pltpu.SMEM),
                  pl.BlockSpec((w, gw), lambda b, g, n: (b * nq_seq + n, g)),
                  ctx_kv, ctx_kv, band(-1), band(-1), band(0), band(0), band(1), band(1)],
        out_specs=pl.BlockSpec((w, gw), lambda b, g, n: (b * nq_seq + n, g)),
        scratch_shapes=[pltpu.VMEM((nk, LANES), BF16), pltpu.VMEM((nk, LANES), BF16)],
        compiler_params=_cparams("parallel", "parallel", "parallel"),
        name="window_attn",
    )(sink, q, k2, v2, k2, v2, k2, v2, k2, v2)


def _window_mixer(u, rows, n_ctx, seq, w_qkv, b_qkv, sink, w_out, b_out):
    q_width = w_out.shape[0]
    kv_width = (w_qkv.shape[1] - q_width) // 2
    qkv = _matmul(u, w_qkv.astype(BF16), bias=b_qkv, name="swa_qkv")
    cos, sin = _rope_tables(n_ctx, seq)
    q, k2, v2 = _rope_split(qkv, cos, sin, rows, q_width, kv_width)
    o = _window_attention(q, k2, v2, sink.astype(F32), rows, n_ctx)
    return _matmul(o, w_out.astype(BF16), bias=b_out, name="swa_out")


def _conv_kernel(gi_ref, go_ref, val_ref, w_ref, o_ref, p_ref, *, n_ctx):
    length = gi_ref.shape[0]
    pad = 8
    p_ref[0:pad] = jnp.zeros((pad, p_ref.shape[1]), F32)
    p_ref[pad + length:pad + length + pad] = jnp.zeros((pad, p_ref.shape[1]), F32)
    p_ref[pad:pad + length] = gi_ref[...].astype(F32) * val_ref[...].astype(F32)
    t = lax.broadcasted_iota(jnp.int32, (length, 1), 0)
    seg_start = t * (t - n_ctx) == 0
    seg_end = (t - (n_ctx - 1)) * (t - (length - 1)) == 0
    prev = jnp.where(seg_start, 0.0, p_ref[pad - 1:pad - 1 + length])
    nxt = jnp.where(seg_end, 0.0, p_ref[pad + 1:pad + 1 + length])
    z = prev * w_ref[0:1, :] + p_ref[pad:pad + length] * w_ref[1:2, :] + nxt * w_ref[2:3, :]
    o_ref[...] = (go_ref[...].astype(F32) * z).astype(o_ref.dtype)


def _conv_gate(proj, conv_w, rows, n_ctx):
    t_rows = proj.shape[0]
    d = proj.shape[1] // 3
    length = rows.nb_seq * rows.tile
    tc = LANES
    nc = d // tc
    return pl.pallas_call(
        functools.partial(_conv_kernel, n_ctx=n_ctx),
        out_shape=jax.ShapeDtypeStruct((t_rows, d), BF16),
        grid=(rows.batch, nc),
        in_specs=[
            pl.BlockSpec((length, tc), lambda b, c: (b, c)),
            pl.BlockSpec((length, tc), lambda b, c: (b, nc + c)),
            pl.BlockSpec((length, tc), lambda b, c: (b, 2 * nc + c)),
            pl.BlockSpec((CONV_WIDTH, tc), lambda b, c: (0, c)),
        ],
        out_specs=pl.BlockSpec((length, tc), lambda b, c: (b, c)),
        scratch_shapes=[pltpu.VMEM((length + 16, tc), F32)],
        compiler_params=_cparams("parallel", "parallel"),
        name="conv_gate",
    )(proj, proj, proj, conv_w.astype(F32))


def _conv_mixer(u, rows, n_ctx, w_in, w_conv, w_out):
    proj = _matmul(u, w_in.astype(BF16), name="conv_in")
    zg = _conv_gate(proj, w_conv, rows, n_ctx)
    return _matmul(zg, w_out.astype(BF16), name="conv_out")


def _row_gather_copy(u_hbm, x_buf, sem, slot, tok, r):
    return pltpu.make_async_copy(u_hbm.at[pl.ds(tok, 1)], x_buf.at[slot, pl.ds(r, 1)], sem.at[slot])


GATHER_DMA_PRIORITY = 1


def _moe_kernel(blk_exp_ref, n_used_ref, row_tok_ref, u_hbm, wgu_ref, bgu_ref, wdn_ref, bdn_ref, o_ref,
                x_buf, sem, wgu_bf_ref, wdn_bf_ref, x_bf_ref, gu_ref, act_ref):
    i = pl.program_id(0)
    tm = o_ref.shape[0]
    n1, _, cw1 = wgu_bf_ref.shape
    n2, _, cw2 = wdn_bf_ref.shape
    rows1 = (2 * tm // 3) // n1
    rows2 = (tm - rows1 * n1) // n2
    assert rows1 * n1 + rows2 * n2 == tm
    n_used = n_used_ref[0]
    used = i < n_used
    slot = lax.rem(i, 2)

    def wait_gather():
        pltpu.make_async_copy(u_hbm.at[pl.ds(0, tm)], x_buf.at[slot], sem.at[slot]).wait()

    @pl.when(used & (i == 0))
    def _():
        def body(r, carry):
            _row_gather_copy(u_hbm, x_buf, sem, 0, row_tok_ref[r], r).start(priority=GATHER_DMA_PRIORITY)
            return carry
        lax.fori_loop(0, tm, body, 0, unroll=8)

    new_expert = (i == 0) | (blk_exp_ref[i] != blk_exp_ref[jnp.maximum(i - 1, 0)])

    @pl.when(used & new_expert)
    def _():
        for c in range(n1):
            wgu_bf_ref[c] = wgu_ref[:, c * cw1:(c + 1) * cw1].astype(BF16)
        for c in range(n2):
            wdn_bf_ref[c] = wdn_ref[:, c * cw2:(c + 1) * cw2].astype(BF16)

    @pl.when(used)
    def _():
        wait_gather()
        x_bf_ref[...] = x_buf[slot].astype(BF16)
        base = (i + 1) * tm

        def start_rows(first, count):
            for j in range(count):
                r = first + j
                _row_gather_copy(u_hbm, x_buf, sem, 1 - slot, row_tok_ref[base + r], r).start(
                    priority=GATHER_DMA_PRIORITY)

        def gate_up(c, carry):
            gu_ref[c] = _dot(x_bf_ref[...], wgu_bf_ref[c]) + bgu_ref[pl.ds(c, 1), :]
            start_rows(c * rows1, rows1)
            return carry

        lax.fori_loop(0, n1, gate_up, 0)
        half = n1 // 2
        for c in range(half):
            glu = jnp.minimum(gu_ref[c], SWIGLU_LIMIT)
            lin = jnp.clip(gu_ref[half + c], -SWIGLU_LIMIT, SWIGLU_LIMIT)
            act = glu * _sigmoid(SWIGLU_ALPHA * glu) * (lin + 1.0)
            act_ref[:, c * cw1:(c + 1) * cw1] = act.astype(BF16)

        def down(c, carry):
            col = pl.multiple_of(c * cw2, cw2)
            y = _dot(act_ref[...], wdn_bf_ref[c]) + bdn_ref[pl.ds(c, 1), :]
            o_ref[:, pl.ds(col, cw2)] = y.astype(o_ref.dtype)
            start_rows(n1 * rows1 + c * rows2, rows2)
            return carry

        lax.fori_loop(0, n2, down, 0)

    @pl.when((i == n_used) & (i > 0))
    def _():
        wait_gather()

    @pl.when(i >= n_used)
    def _():
        o_ref[...] = jnp.zeros_like(o_ref)


def _expert_mlp(u, row_tok, blk_exp, n_used, layer, w_gate_up, b_gate_up, w_down, b_down):
    d = u.shape[1]
    n_rows = row_tok.shape[0]
    depth, n_exp, _, ff2 = w_gate_up.shape
    ff = ff2 // 2
    tm = MOE_ROW_BLOCK
    cw1 = 2 * LANES if ff % (2 * LANES) == 0 else LANES
    cw2 = 4 * LANES
    n1, n2 = ff2 // cw1, d // cw2
    grid_spec = pltpu.PrefetchScalarGridSpec(
        num_scalar_prefetch=3,
        grid=(n_rows // tm,),
        in_specs=[
            pl.BlockSpec(memory_space=pl.ANY),
            pl.BlockSpec((None, None, d, ff2), lambda i, be, nu, rt: (layer, be[i], 0, 0)),
            pl.BlockSpec((None, None, n1, cw1), lambda i, be, nu, rt: (layer, be[i], 0, 0)),
            pl.BlockSpec((None, None, ff, d), lambda i, be, nu, rt: (layer, be[i], 0, 0)),
            pl.BlockSpec((None, None, n2, cw2), lambda i, be, nu, rt: (layer, be[i], 0, 0)),
        ],
        out_specs=pl.BlockSpec((tm, d), lambda i, be, nu, rt: (i, 0)),
        scratch_shapes=[pltpu.VMEM((2, tm, d), F32), pltpu.SemaphoreType.DMA((2,)),
                        pltpu.VMEM((n1, d, cw1), BF16), pltpu.VMEM((n2, ff, cw2), BF16),
                        pltpu.VMEM((tm, d), BF16), pltpu.VMEM((n1, tm, cw1), F32), pltpu.VMEM((tm, ff), BF16)],
    )
    return pl.pallas_call(
        _moe_kernel,
        out_shape=jax.ShapeDtypeStruct((n_rows, d), BF16),
        grid_spec=grid_spec,
        compiler_params=pltpu.CompilerParams(dimension_semantics=("arbitrary",), vmem_limit_bytes=MOE_VMEM_LIMIT),
        name="expert_mlp",
    )(blk_exp, n_used, row_tok, u, w_gate_up, b_gate_up.reshape(depth, n_exp, n1, cw1),
      w_down, b_down.reshape(depth, n_exp, n2, cw2))


def _moe(u, route_idx, counts, layer, w_gate_up, b_gate_up, w_down, b_down):
    n_tok, d = u.shape
    tm = MOE_ROW_BLOCK
    n_assign = n_tok * TOP_K
    n_blocks = n_assign // tm + N_EXPERTS
    n_rows = n_blocks * tm
    top_exp = route_idx[:, :TOP_K]
    rank = route_idx[:, TOP_K:2 * TOP_K]
    counts = counts[0, :N_EXPERTS].astype(jnp.int32)
    padded = (counts + tm - 1) // tm * tm
    start = jnp.cumsum(counts) - counts
    pad_end = jnp.cumsum(padded)
    pad_start = pad_end - padded
    blk_start = jnp.arange(n_blocks, dtype=jnp.int32) * tm
    blk_exp = jnp.minimum(jnp.sum(pad_end[None, :] <= blk_start[:, None], axis=1), N_EXPERTS - 1).astype(jnp.int32)
    n_used = (pad_end[-1:] // tm).astype(jnp.int32)
    order = jnp.argsort(top_exp.reshape(-1), stable=True).astype(jnp.int32)
    row_exp = jnp.repeat(blk_exp, tm)
    within = jnp.arange(n_rows, dtype=jnp.int32) - pad_start[row_exp]
    src = jnp.clip(start[row_exp] + within, 0, n_assign - 1)
    row_tok = jnp.where(within < counts[row_exp], order[src] // TOP_K, 0)
    y_rows = _expert_mlp(u, row_tok, blk_exp, n_used, layer, w_gate_up, b_gate_up, w_down, b_down)
    pos = pad_start[top_exp] + rank
    return y_rows[pos.T.reshape(-1)].reshape(TOP_K, n_tok, d)


def kernel(x, c, ctx, c_ctx, ada_w, ada_b, ln_g, ln_b, gla_w_in, gla_gate_w1, gla_gate_w2, gla_gate_b, gla_norm_g, gla_w_out, swa_w_qkv, swa_b_qkv, swa_sink, swa_w_out, swa_b_out, conv_w_in, conv_w, conv_w_out, moe_router_w, moe_router_b, moe_w_gate_up, moe_b_gate_up, moe_w_down, moe_b_down):
    batch, seq, d = x.shape
    n_ctx = ctx.shape[1]
    depth = ada_w.shape[0]
    alpha = (2 * depth) ** 0.25
    rows_full = _Rows(batch, n_ctx // ROW_TILE, seq // ROW_TILE)
    rows_lat = _Rows(batch, 0, seq // ROW_TILE)

    cond = jnp.zeros((8, d), F32).at[0].set(c_ctx).at[1:1 + batch].set(c)
    ada = _ada_mods(cond, ada_w, ada_b)
    mods_ctx = jnp.broadcast_to(ada[:, 0:1], (depth, batch, N_MOD * d))
    mods = jnp.stack([mods_ctx, ada[:, 1:1 + batch]], axis=2).reshape(depth, batch, 2, N_MOD, d)

    router_w = jnp.zeros((depth, d, LANES), F32).at[:, :, :N_EXPERTS].set(moe_router_w)
    router_b = jnp.zeros((depth, 1, LANES), F32).at[:, 0, :N_EXPERTS].set(moe_router_b)

    h = jnp.concatenate([ctx, x], axis=1).reshape(rows_full.n_rows, d)
    u = _modulate(h, mods[0], rows_full, 0, 1)
    rows = rows_full
    for i in range(depth):
        last = i == depth - 1
        kind, j = i % 3, i // 3
        if kind == 0:
            y = _gla_mixer(u, rows, gla_w_in[j], gla_gate_w1[j], gla_gate_w2[j], gla_gate_b[j], gla_norm_g[j], gla_w_out[j])
        elif kind == 1:
            y = _window_mixer(u, rows, n_ctx, seq, swa_w_qkv[j], swa_b_qkv[j], swa_sink[j], swa_w_out[j], swa_b_out[j])
        else:
            y = _conv_mixer(u, rows, n_ctx, conv_w_in[j], conv_w[j], conv_w_out[j])
        rows_out = rows_lat if last else rows
        h, u, route_idx, route_gate, counts = _resid_ln(
            h, y, mods[i], mods[i], ln_g[i, 0], ln_b[i, 0], rows, rows_out, alpha=alpha, gate_idx=2,
            next_mod=(3, 4), router=(router_w[i], router_b[i]))
        rows = rows_out
        y = _moe(u, route_idx, counts, i, moe_w_gate_up, moe_b_gate_up, moe_w_down, moe_b_down)
        if last:
            (h,) = _resid_ln(h, y, mods[i], mods[i], ln_g[i, 1], ln_b[i, 1], rows, rows, alpha=alpha, gate_idx=5,
                             combine_gates=route_gate)
        else:
            h, u = _resid_ln(h, y, mods[i], mods[i + 1], ln_g[i, 1], ln_b[i, 1], rows, rows, alpha=alpha, gate_idx=5,
                             next_mod=(0, 1), combine_gates=route_gate)
    return h.reshape(batch, seq, d)
```

```python
import functools

import jax
import jax.numpy as jnp
from jax import lax
from jax.experimental import pallas as pl
from jax.experimental.pallas import tpu as pltpu

F32 = jnp.float32
BF16 = jnp.bfloat16
HIGHEST = lax.Precision.HIGHEST

GRID_W = 64
GLA_HEADS = 4
GLA_GATE_RANK = 16
GLA_GATE_NORM = 16.0
GLA_CHUNK = 64
SWA_HEAD_DIM = 64
SWA_GROUP = 8
WINDOW = 128
ROPE_BASE = 10000.0
CONV_WIDTH = 3
N_EXPERTS = 32
TOP_K = 4
SWIGLU_LIMIT = 7.0
SWIGLU_ALPHA = 1.702
N_MOD = 6
LN_EPS = 1e-5
NEG_INF = -1e30

LANES = 128
ROW_TILE = 256
MOE_ROW_BLOCK = 256
VMEM_LIMIT = 48 * 1024 * 1024
MOE_VMEM_LIMIT = 60 * 1024 * 1024


def _cparams(*sem):
    return pltpu.CompilerParams(dimension_semantics=sem, vmem_limit_bytes=VMEM_LIMIT)


def _sigmoid(x):
    return 1.0 / (1.0 + jnp.exp(-x))


def _dot(a, b):
    return jnp.dot(a, b, preferred_element_type=F32)


def _dot_nt(a, b):
    return lax.dot_general(a, b, (((1,), (1,)), ((), ())), preferred_element_type=F32)


def _split_bf16(x):
    hi = x.astype(BF16)
    lo = (x - hi.astype(F32)).astype(BF16)
    return hi, lo


class _Rows:
    def __init__(self, batch, nb_ctx, nb_lat, tile=ROW_TILE):
        self.batch, self.nb_ctx, self.nb_lat, self.tile = batch, nb_ctx, nb_lat, tile
        self.nb_seq = nb_ctx + nb_lat
        self.n_blocks = batch * self.nb_seq
        self.n_rows = self.n_blocks * tile

    def batch_of(self, i):
        return i // self.nb_seq

    def seg_of(self, i):
        if self.nb_ctx == 0:
            return 1
        return jnp.where(i % self.nb_seq >= self.nb_ctx, 1, 0)

    def latent_block(self, i):
        return (i // self.nb_lat) * self.nb_seq + self.nb_ctx + i % self.nb_lat


def _ada_kernel(c_ref, w_ref, b_ref, o_ref):
    c = c_ref[...]
    s = c * _sigmoid(c)
    o_ref[...] = jnp.dot(s, w_ref[...], precision=HIGHEST, preferred_element_type=F32) + b_ref[...]


def _ada_mods(cond, ada_w, ada_b):
    depth, d, n = ada_w.shape
    tn = 1024 if n % 1024 == 0 else n
    return pl.pallas_call(
        _ada_kernel,
        out_shape=jax.ShapeDtypeStruct((depth, 8, n), F32),
        grid=(depth, n // tn),
        in_specs=[
            pl.BlockSpec((8, d), lambda l, j: (0, 0)),
            pl.BlockSpec((None, d, tn), lambda l, j: (l, 0, j)),
            pl.BlockSpec((None, 1, tn), lambda l, j: (l, 0, j)),
        ],
        out_specs=pl.BlockSpec((None, 8, tn), lambda l, j: (l, 0, j)),
        compiler_params=_cparams("parallel", "parallel"),
        name="ada_mods",
    )(cond, ada_w, ada_b.reshape(depth, 1, n))


def _modulate_kernel(h_ref, m_ref, u_ref, *, shift_idx, scale_idx):
    h = h_ref[...]
    u = h * (1.0 + m_ref[scale_idx:scale_idx + 1, :]) + m_ref[shift_idx:shift_idx + 1, :]
    u_ref[...] = u.astype(u_ref.dtype)


def _modulate(h, mods, rows, shift_idx, scale_idx):
    d = h.shape[1]
    t = rows.tile
    return pl.pallas_call(
        functools.partial(_modulate_kernel, shift_idx=shift_idx, scale_idx=scale_idx),
        out_shape=jax.ShapeDtypeStruct(h.shape, BF16),
        grid=(rows.n_blocks,),
        in_specs=[
            pl.BlockSpec((t, d), lambda i: (i, 0)),
            pl.BlockSpec((None, None, N_MOD, d), lambda i: (rows.batch_of(i), rows.seg_of(i), 0, 0)),
        ],
        out_specs=pl.BlockSpec((t, d), lambda i: (i, 0)),
        compiler_params=_cparams("parallel"),
        name="modulate",
    )(h, mods)


def _mm_kernel(x_ref, w_ref, o_ref):
    o_ref[...] = _dot(x_ref[...], w_ref[...]).astype(o_ref.dtype)


def _mm_bias_kernel(x_ref, w_ref, b_ref, o_ref):
    o_ref[...] = (_dot(x_ref[...], w_ref[...]) + b_ref[...]).astype(o_ref.dtype)


def _pick_tile(n, candidates):
    for c in candidates:
        if n % c == 0:
            return c
    return n


def _matmul(x, w, bias=None, out_dtype=BF16, name="proj"):
    m, k = x.shape
    n = w.shape[1]
    tm = _pick_tile(m, (1024, 512, 256))
    tn = _pick_tile(n, (1024, 768, 640, 512, 256, 128))
    in_specs = [
        pl.BlockSpec((tm, k), lambda i, j: (i, 0)),
        pl.BlockSpec((k, tn), lambda i, j: (0, j)),
    ]
    args = [x, w]
    body = _mm_kernel
    if bias is not None:
        in_specs.append(pl.BlockSpec((1, tn), lambda i, j: (0, j)))
        args.append(bias.reshape(1, n).astype(F32))
        body = _mm_bias_kernel
    return pl.pallas_call(
        body,
        out_shape=jax.ShapeDtypeStruct((m, n), out_dtype),
        grid=(m // tm, n // tn),
        in_specs=in_specs,
        out_specs=pl.BlockSpec((tm, tn), lambda i, j: (i, j)),
        compiler_params=_cparams("parallel", "parallel"),
        name=name,
    )(*args)


def _route_block(logits, count_ref):
    t = logits.shape[0]
    lane = lax.broadcasted_iota(jnp.int32, (t, LANES), 1)
    lane_f = lane.astype(F32)
    left = jnp.where(lane < N_EXPERTS, logits, NEG_INF)
    hot = jnp.zeros((t, LANES), F32)
    sels, vals = [], []
    for _ in range(TOP_K):
        m = jnp.max(left, axis=-1, keepdims=True)
        sel = jnp.min(jnp.where(left == m, lane_f, float(LANES)), axis=-1, keepdims=True)
        pick = lane_f == sel
        left = jnp.where(pick, NEG_INF, left)
        hot = jnp.where(pick, 1.0, hot)
        sels.append(sel)
        vals.append(m)
    row = lax.broadcasted_iota(jnp.int32, (t, t), 0)
    col = lax.broadcasted_iota(jnp.int32, (t, t), 1)
    earlier = jnp.where(row > col, 1.0, 0.0).astype(BF16)
    before = _dot(earlier, hot.astype(BF16)) + count_ref[...]
    count_ref[...] += jnp.sum(hot, axis=0, keepdims=True)
    exps = [jnp.exp(v - vals[0]) for v in vals]
    denom = exps[0]
    for e in exps[1:]:
        denom = denom + e
    idx = jnp.zeros((t, LANES), jnp.int32)
    gates = jnp.zeros((t, LANES), F32)
    for k in range(TOP_K):
        rank = jnp.sum(jnp.where(lane_f == sels[k], before, 0.0), axis=-1, keepdims=True)
        idx = jnp.where(lane == k, sels[k].astype(jnp.int32), idx)
        idx = jnp.where(lane == TOP_K + k, rank.astype(jnp.int32), idx)
        gates = jnp.where(lane == k, exps[k] / denom, gates)
    return idx, gates


def _resid_ln_kernel(*refs, alpha, gate_idx, shift_idx, scale_idx, with_u, with_router, combine):
    h_ref, y_ref = refs[:2]
    pos = 2
    if combine:
        yg_ref = refs[pos]
        pos += 1
    m_ref, mn_ref, g_ref, b_ref = refs[pos:pos + 4]
    pos += 4
    if with_router:
        wr_hi_ref, wr_lo_ref, br_ref = refs[pos:pos + 3]
        pos += 3
    hn_ref = refs[pos]
    pos += 1
    if combine:
        gates = yg_ref[...]
        y = gates[:, 0:1] * y_ref[0].astype(F32)
        for k in range(1, TOP_K):
            y = y + gates[:, k:k + 1] * y_ref[k].astype(F32)
    else:
        y = y_ref[...].astype(F32)
    z = alpha * h_ref[...] + y * m_ref[gate_idx:gate_idx + 1, :]
    zc = z - jnp.mean(z, axis=-1, keepdims=True)
    var = jnp.mean(zc * zc, axis=-1, keepdims=True)
    hn = zc * lax.rsqrt(var + LN_EPS) * g_ref[...] + b_ref[...]
    hn_ref[...] = hn
    if with_u:
        u_ref = refs[pos]
        pos += 1
        u = hn * (1.0 + mn_ref[scale_idx:scale_idx + 1, :]) + mn_ref[shift_idx:shift_idx + 1, :]
        u_ref[...] = u.astype(u_ref.dtype)
        if with_router:
            idx_ref, gate_ref, cnt_ref, count_ref = refs[pos:pos + 4]

            @pl.when(pl.program_id(0) == 0)
            def _():
                count_ref[...] = jnp.zeros_like(count_ref)

            u_hi, u_lo = _split_bf16(u)
            w_hi = wr_hi_ref[...]
            logits = _dot(u_hi, w_hi) + _dot(u_lo, w_hi) + _dot(u_hi, wr_lo_ref[...]) + br_ref[...]
            idx, gates = _route_block(logits, count_ref)
            idx_ref[...] = idx
            gate_ref[...] = gates
            cnt_ref[...] = count_ref[...]


def _resid_ln(h, y, mods, mods_next, ln_g, ln_b, rows_in, rows_out, *, alpha, gate_idx, next_mod=None, router=None,
              combine_gates=None):
    d = h.shape[1]
    t = rows_out.tile
    if rows_in is rows_out:
        in_row = lambda i: i
    else:
        in_row = rows_in.latent_block
    with_u = next_mod is not None
    with_router = router is not None
    combine = combine_gates is not None
    shift_idx, scale_idx = next_mod if with_u else (0, 0)
    mod_spec = pl.BlockSpec((None, None, N_MOD, d), lambda i: (rows_out.batch_of(i), rows_out.seg_of(i), 0, 0))
    in_specs = [pl.BlockSpec((t, d), lambda i: (in_row(i), 0))]
    args = [h, y]
    if combine:
        in_specs += [pl.BlockSpec((TOP_K, t, d), lambda i: (0, in_row(i), 0)),
                     pl.BlockSpec((t, LANES), lambda i: (in_row(i), 0))]
        args.append(combine_gates)
    else:
        in_specs.append(pl.BlockSpec((t, d), lambda i: (in_row(i), 0)))
    in_specs += [mod_spec, mod_spec, pl.BlockSpec((1, d), lambda i: (0, 0)), pl.BlockSpec((1, d), lambda i: (0, 0))]
    args += [mods, mods_next, ln_g.reshape(1, d), ln_b.reshape(1, d)]
    out_shape = [jax.ShapeDtypeStruct((rows_out.n_rows, d), F32)]
    out_specs = [pl.BlockSpec((t, d), lambda i: (i, 0))]
    scratch = []
    if with_router:
        wr, br = router
        w_spec = pl.BlockSpec((d, LANES), lambda i: (0, 0))
        in_specs += [w_spec, w_spec, pl.BlockSpec((1, LANES), lambda i: (0, 0))]
        args += [*_split_bf16(wr), br]
    if with_u:
        out_shape.append(jax.ShapeDtypeStruct((rows_out.n_rows, d), F32 if with_router else BF16))
        out_specs.append(pl.BlockSpec((t, d), lambda i: (i, 0)))
    if with_router:
        out_shape += [jax.ShapeDtypeStruct((rows_out.n_rows, LANES), jnp.int32),
                      jax.ShapeDtypeStruct((rows_out.n_rows, LANES), F32),
                      jax.ShapeDtypeStruct((1, LANES), F32)]
        out_specs += [pl.BlockSpec((t, LANES), lambda i: (i, 0)), pl.BlockSpec((t, LANES), lambda i: (i, 0)),
                      pl.BlockSpec((1, LANES), lambda i: (0, 0))]
        scratch.append(pltpu.VMEM((1, LANES), F32))
    return pl.pallas_call(
        functools.partial(_resid_ln_kernel, alpha=alpha, gate_idx=gate_idx, shift_idx=shift_idx,
                          scale_idx=scale_idx, with_u=with_u, with_router=with_router, combine=combine),
        out_shape=out_shape,
        grid=(rows_out.n_blocks,),
        in_specs=in_specs,
        out_specs=out_specs,
        scratch_shapes=scratch,
        compiler_params=_cparams("arbitrary" if with_router else "parallel"),
        name="resid_ln",
    )(*args)


def _gla_kernel(*refs, reverse, fuse_post, q_scale, dk, dv):
    if fuse_post:
        q_ref, k_ref, v_ref, r_ref, w2_ref, gb_ref, other_ref, g_ref, ng_ref, out_ref, state_ref = refs
    else:
        q_ref, k_ref, v_ref, r_ref, w2_ref, gb_ref, out_ref, state_ref = refs

    @pl.when(pl.program_id(1) == 0)
    def _():
        state_ref[...] = jnp.zeros_like(state_ref)

    t = q_ref.shape[0]
    c = GLA_CHUNK
    shift = c.bit_length() - 1
    r_hi, r_lo = _split_bf16(r_ref[...])
    w_hi, w_lo = _split_bf16(w2_ref[...])
    z = _dot(r_hi, w_hi) + _dot(r_lo, w_hi) + _dot(r_hi, w_lo) + gb_ref[...]
    log_a = (jnp.minimum(z, 0.0) - jnp.log(1.0 + jnp.exp(-jnp.abs(z)))) * (1.0 / GLA_GATE_NORM)
    row = lax.broadcasted_iota(jnp.int32, (t, t), 0)
    col = lax.broadcasted_iota(jnp.int32, (t, t), 1)
    same_chunk = lax.shift_right_logical(row, shift) == lax.shift_right_logical(col, shift)
    ahead = (col - row) if reverse else (row - col)
    tri = jnp.where(jnp.where(same_chunk, ahead, -1) >= 0, 1.0, 0.0).astype(BF16)
    a_hi, a_lo = _split_bf16(log_a)
    cum = _dot(tri, a_hi) + _dot(tri, a_lo)
    grow = jnp.exp(cum)
    shrink = jnp.exp(-cum)
    crow = lax.broadcasted_iota(jnp.int32, (c, c), 0)
    ccol = lax.broadcasted_iota(jnp.int32, (c, c), 1)
    causal = (ccol >= crow) if reverse else (crow >= ccol)
    n_chunks = t // c
    for h in range(GLA_HEADS):
        ks = slice(h * dk, (h + 1) * dk)
        vs = slice(h * dv, (h + 1) * dv)
        q_dec = (q_ref[:, ks].astype(F32) * q_scale * grow[:, ks]).astype(BF16)
        k_in = k_ref[:, ks].astype(F32) * shrink[:, ks]
        for step in range(n_chunks):
            j = n_chunks - 1 - step if reverse else step
            rs = slice(j * c, (j + 1) * c)
            edge = j * c if reverse else (j + 1) * c - 1
            decay = jnp.exp(cum[edge:edge + 1, ks])
            qd = q_dec[rs]
            k_intra = k_in[rs].astype(BF16)
            k_state = (k_in[rs] * decay).astype(BF16)
            v = v_ref[rs, vs]
            scores = jnp.where(causal, _dot_nt(qd, k_intra), 0.0)
            state_t = state_ref[h]
            o = _dot(scores.astype(BF16), v) + _dot_nt(qd, state_t.astype(BF16))
            kv_t = lax.dot_general(v, k_state, (((0,), (0,)), ((), ())), preferred_element_type=F32)
            state_ref[h] = state_t * decay + kv_t
            if fuse_post:
                o = o + other_ref[rs, vs]
                o = o * lax.rsqrt(jnp.mean(o * o, axis=-1, keepdims=True) + LN_EPS) * ng_ref[...]
                g = g_ref[rs, vs].astype(F32)
                out_ref[rs, vs] = (o * (g * _sigmoid(g))).astype(out_ref.dtype)
            else:
                out_ref[rs, vs] = o


def _gla_scan(qkvg, r, w2, gate_b, rows, direction, other=None, norm_g=None):
    t_rows = qkvg.shape[0]
    dk_tot = w2.shape[2]
    dk = dk_tot // GLA_HEADS
    dv_tot = (qkvg.shape[1] - 2 * dk_tot) // 2
    dv = dv_tot // GLA_HEADS
    t = rows.tile
    reverse = direction == 1
    fuse_post = other is not None

    def block(b, n):
        if reverse:
            n = jnp.where(n < rows.nb_ctx, rows.nb_ctx - 1 - n, rows.nb_seq + rows.nb_ctx - 1 - n)
        return b * rows.nb_seq + n

    in_specs = [
        pl.BlockSpec((t, dk_tot), lambda b, n: (block(b, n), 0)),
        pl.BlockSpec((t, dk_tot), lambda b, n: (block(b, n), 1)),
        pl.BlockSpec((t, dv_tot), lambda b, n: (block(b, n), (2 * dk_tot) // dv_tot)),
        pl.BlockSpec((t, LANES), lambda b, n: (block(b, n), direction)),
        pl.BlockSpec((None, LANES, dk_tot), lambda b, n: (direction, 0, 0)),
        pl.BlockSpec((None, 1, dk_tot), lambda b, n: (direction, 0, 0)),
    ]
    args = [qkvg, qkvg, qkvg, r, w2, gate_b]
    if fuse_post:
        in_specs += [
            pl.BlockSpec((t, dv_tot), lambda b, n: (block(b, n), 0)),
            pl.BlockSpec((t, dv_tot), lambda b, n: (block(b, n), (2 * dk_tot + dv_tot) // dv_tot)),
            pl.BlockSpec((1, dv), lambda b, n: (0, 0)),
        ]
        args += [other, qkvg, norm_g.reshape(1, dv).astype(F32)]
    return pl.pallas_call(
        functools.partial(_gla_kernel, reverse=reverse, fuse_post=fuse_post, q_scale=float(dk) ** -0.5, dk=dk, dv=dv),
        out_shape=jax.ShapeDtypeStruct((t_rows, dv_tot), BF16 if fuse_post else F32),
        grid=(rows.batch, rows.nb_seq),
        in_specs=in_specs,
        out_specs=pl.BlockSpec((t, dv_tot), lambda b, n: (block(b, n), 0)),
        scratch_shapes=[pltpu.VMEM((GLA_HEADS, dv, dk), F32)],
        compiler_params=_cparams("parallel", "arbitrary"),
        name="gla_scan_bwd" if reverse else "gla_scan_fwd",
    )(*args)


def _gla_mixer(u, rows, w_in, gate_w1, gate_w2, gate_b, norm_g, w_out):
    d = u.shape[1]
    dk_tot = gate_w2.shape[2]
    qkvg = _matmul(u, w_in.astype(BF16), name="gla_in")
    w1 = jnp.zeros((d, 2 * LANES), F32)
    w2 = jnp.zeros((2, LANES, dk_tot), F32)
    for dd in range(2):
        w1 = w1.at[:, dd * LANES:dd * LANES + GLA_GATE_RANK].set(gate_w1[dd])
        w2 = w2.at[dd, :GLA_GATE_RANK].set(gate_w2[dd])
    r = _matmul(u, w1.astype(BF16), out_dtype=F32, name="gla_gate_in")
    gate_b = gate_b.reshape(2, 1, dk_tot).astype(F32)
    o_fwd = _gla_scan(qkvg, r, w2, gate_b, rows, 0)
    y = _gla_scan(qkvg, r, w2, gate_b, rows, 1, other=o_fwd, norm_g=norm_g)
    return _matmul(y, w_out.astype(BF16), name="gla_out")


def _rope_kernel(x_ref, cos_ref, sin_ref, q_ref, k2_ref, v2_ref, *, q_width, kv_width):
    cos = cos_ref[...]
    sin = sin_ref[...]
    lane = lax.broadcasted_iota(jnp.int32, cos.shape, 1)
    first = (lane % 32) < 16
    low = lane < SWA_HEAD_DIM

    def rope(slab):
        partner = jnp.where(first, pltpu.roll(slab, LANES - 16, 1), pltpu.roll(slab, 16, 1))
        return slab * cos + partner * sin

    scale = SWA_HEAD_DIM ** -0.5
    for s in range(q_width // LANES):
        slab = x_ref[:, s * LANES:(s + 1) * LANES].astype(F32)
        q_ref[:, s * LANES:(s + 1) * LANES] = (rope(slab) * scale).astype(q_ref.dtype)
    for s in range(kv_width // LANES):
        k_slab = rope(x_ref[:, q_width + s * LANES:q_width + (s + 1) * LANES].astype(F32))
        v_slab = x_ref[:, q_width + kv_width + s * LANES:q_width + kv_width + (s + 1) * LANES].astype(F32)
        for slab, out_ref in ((k_slab, k2_ref), (v_slab, v2_ref)):
            swapped = pltpu.roll(slab, SWA_HEAD_DIM, 1)
            out_ref[2 * s] = jnp.where(low, slab, swapped).astype(out_ref.dtype)
            out_ref[2 * s + 1] = jnp.where(low, swapped, slab).astype(out_ref.dtype)


def _rope_tables(n_ctx, seq):
    t = jnp.arange(seq, dtype=jnp.int32)
    row = (t // GRID_W).astype(F32)
    col = (t % GRID_W).astype(F32)
    n_freq = SWA_HEAD_DIM // 4
    inv_freq = ROPE_BASE ** (-jnp.arange(n_freq, dtype=F32) / n_freq)
    lane = jnp.arange(LANES)
    within = lane % SWA_HEAD_DIM
    pos = jnp.where((within < SWA_HEAD_DIM // 2)[None, :], row[:, None], col[:, None])
    ang = pos * inv_freq[lane % n_freq][None, :]
    sign = jnp.where((lane % 32) < 16, -1.0, 1.0)[None, :]
    cos = jnp.concatenate([jnp.ones((n_ctx, LANES), F32), jnp.cos(ang)], axis=0)
    sin = jnp.concatenate([jnp.zeros((n_ctx, LANES), F32), jnp.sin(ang) * sign], axis=0)
    return cos, sin


def _rope_split(qkv, cos, sin, rows, q_width, kv_width):
    t_rows = qkv.shape[0]
    t = rows.tile
    n_kv = kv_width // SWA_HEAD_DIM
    seq_rows = rows.nb_seq * t
    kv_shape = jax.ShapeDtypeStruct((rows.batch, n_kv, seq_rows, LANES), BF16)
    kv_spec = pl.BlockSpec((None, n_kv, t, LANES), lambda i: (i // rows.nb_seq, 0, i % rows.nb_seq, 0))
    return pl.pallas_call(
        functools.partial(_rope_kernel, q_width=q_width, kv_width=kv_width),
        out_shape=[jax.ShapeDtypeStruct((t_rows, q_width), BF16), kv_shape, kv_shape],
        grid=(rows.n_blocks,),
        in_specs=[
            pl.BlockSpec((t, qkv.shape[1]), lambda i: (i, 0)),
            pl.BlockSpec((t, LANES), lambda i: (i % rows.nb_seq, 0)),
            pl.BlockSpec((t, LANES), lambda i: (i % rows.nb_seq, 0)),
        ],
        out_specs=[pl.BlockSpec((t, q_width), lambda i: (i, 0)), kv_spec, kv_spec],
        compiler_params=_cparams("parallel"),
        name="rope_split",
    )(qkv, cos, sin)


def _attend_heads(q_ref, kk, vv, mask, sink_ref, head0, o_ref):
    tq = q_ref.shape[0]
    lane = lax.broadcasted_iota(jnp.int32, (tq, LANES), 1)
    low = lane < SWA_HEAD_DIM
    q_all = q_ref[...]
    sinks = [sink_ref[head0 + h] for h in range(SWA_GROUP)]
    pair_outs = []
    for pair in range(SWA_GROUP // 2):
        q2 = q_all[:, pair * LANES:(pair + 1) * LANES]
        outs = []
        for half in range(2):
            qh = jnp.where(low if half == 0 else lane >= SWA_HEAD_DIM, q2, jnp.zeros_like(q2))
            s = _dot_nt(qh, kk)
            if mask is not None:
                s = jnp.where(mask, s, NEG_INF)
            sink = sinks[2 * pair + half]
            m = jnp.maximum(jnp.max(s, axis=-1, keepdims=True), sink)
            p = jnp.exp(s - m)
            denom = jnp.sum(p, axis=-1, keepdims=True) + jnp.exp(sink - m)
            outs.append(_dot(p.astype(BF16), vv) / denom)
        pair_outs.append(jnp.where(low, outs[0], outs[1]).astype(o_ref.dtype))
    o_ref[...] = jnp.concatenate(pair_outs, axis=1)


def _attn_kernel(sink_ref, q_ref, kc_ref, vc_ref, kp_ref, vp_ref, km_ref, vm_ref, kn_ref, vn_ref,
                 o_ref, kk_ref, vv_ref, *, n_ctx, nq_ctx, nq_lat):
    n = pl.program_id(2)
    head0 = pl.program_id(1) * SWA_GROUP
    w = WINDOW

    @pl.when(n < nq_ctx)
    def _():
        _attend_heads(q_ref, kc_ref[...], vc_ref[...], None, sink_ref, head0, o_ref)

    @pl.when(n >= nq_ctx)
    def _():
        m = n - nq_ctx
        for ref, parts in ((kk_ref, (kc_ref, kp_ref, km_ref, kn_ref)), (vv_ref, (vc_ref, vp_ref, vm_ref, vn_ref))):
            ref[0:n_ctx] = parts[0][...]
            ref[n_ctx:n_ctx + w] = parts[1][...]
            ref[n_ctx + w:n_ctx + 2 * w] = parts[2][...]
            ref[n_ctx + 2 * w:n_ctx + 3 * w] = parts[3][...]
        nk = n_ctx + 3 * w
        qpos = lax.broadcasted_iota(jnp.int32, (w, nk), 0)
        col = lax.broadcasted_iota(jnp.int32, (w, nk), 1)
        rel = jnp.where(col < n_ctx, 0, col - (n_ctx + w) - qpos)
        far = 4 * w
        no_prev = jnp.where(m > 0, 0, far)
        no_next = jnp.where(m < nq_lat - 1, 0, far)
        missing = jnp.where(col < n_ctx, 0,
                            jnp.where(col < n_ctx + w, no_prev, jnp.where(col >= n_ctx + 2 * w, no_next, 0)))
        mask = jnp.abs(rel) + missing <= w
        _attend_heads(q_ref, kk_ref[...], vv_ref[...], mask, sink_ref, head0, o_ref)


def _window_attention(q, k2, v2, sink, rows, n_ctx):
    t_rows, q_width = q.shape
    batch, n_kv, seq_rows, _ = k2.shape
    w = WINDOW
    gw = SWA_GROUP * SWA_HEAD_DIM
    nq_seq = seq_rows // w
    nq_ctx = n_ctx // w
    nq_lat = nq_seq - nq_ctx
    ctx_kv = pl.BlockSpec((None, None, n_ctx, LANES), lambda b, g, n: (b, g, 0, 0))

    def band(shift):
        def index(b, g, n):
            return (b, g, nq_ctx + jnp.clip(n - nq_ctx + shift, 0, nq_lat - 1), 0)
        return pl.BlockSpec((None, None, w, LANES), index)

    nk = n_ctx + 3 * w
    return pl.pallas_call(
        functools.partial(_attn_kernel, n_ctx=n_ctx, nq_ctx=nq_ctx, nq_lat=nq_lat),
        out_shape=jax.ShapeDtypeStruct((t_rows, q_width), BF16),
        grid=(batch, n_kv, nq_seq),
        in_specs=[pl.BlockSpec(memory_space=pltpu.SMEM),
                  pl.BlockSpec((w, gw), lambda b, g, n: (b * nq_seq + n, g)),
                  ctx_kv, ctx_kv, band(-1), band(-1), band(0), band(0), band(1), band(1)],
        out_specs=pl.BlockSpec((w, gw), lambda b, g, n: (b * nq_seq + n, g)),
        scratch_shapes=[pltpu.VMEM((nk, LANES), BF16), pltpu.VMEM((nk, LANES), BF16)],
        compiler_params=_cparams("parallel", "parallel", "parallel"),
        name="window_attn",
    )(sink, q, k2, v2, k2, v2, k2, v2, k2, v2)


def _window_mixer(u, rows, n_ctx, seq, w_qkv, b_qkv, sink, w_out, b_out):
    q_width = w_out.shape[0]
    kv_width = (w_qkv.shape[1] - q_width) // 2
    qkv = _matmul(u, w_qkv.astype(BF16), bias=b_qkv, name="swa_qkv")
    cos, sin = _rope_tables(n_ctx, seq)
    q, k2, v2 = _rope_split(qkv, cos, sin, rows, q_width, kv_width)
    o = _window_attention(q, k2, v2, sink.astype(F32), rows, n_ctx)
    return _matmul(o, w_out.astype(BF16), bias=b_out, name="swa_out")


def _conv_kernel(gi_ref, go_ref, val_ref, w_ref, o_ref, p_ref, *, n_ctx):
    length = gi_ref.shape[0]
    pad = 8
    p_ref[0:pad] = jnp.zeros((pad, p_ref.shape[1]), F32)
    p_ref[pad + length:pad + length + pad] = jnp.zeros((pad, p_ref.shape[1]), F32)
    p_ref[pad:pad + length] = gi_ref[...].astype(F32) * val_ref[...].astype(F32)
    t = lax.broadcasted_iota(jnp.int32, (length, 1), 0)
    seg_start = t * (t - n_ctx) == 0
    seg_end = (t - (n_ctx - 1)) * (t - (length - 1)) == 0
    prev = jnp.where(seg_start, 0.0, p_ref[pad - 1:pad - 1 + length])
    nxt = jnp.where(seg_end, 0.0, p_ref[pad + 1:pad + 1 + length])
    z = prev * w_ref[0:1, :] + p_ref[pad:pad + length] * w_ref[1:2, :] + nxt * w_ref[2:3, :]
    o_ref[...] = (go_ref[...].astype(F32) * z).astype(o_ref.dtype)


def _conv_gate(proj, conv_w, rows, n_ctx):
    t_rows = proj.shape[0]
    d = proj.shape[1] // 3
    length = rows.nb_seq * rows.tile
    tc = LANES
    nc = d // tc
    return pl.pallas_call(
        functools.partial(_conv_kernel, n_ctx=n_ctx),
        out_shape=jax.ShapeDtypeStruct((t_rows, d), BF16),
        grid=(rows.batch, nc),
        in_specs=[
            pl.BlockSpec((length, tc), lambda b, c: (b, c)),
            pl.BlockSpec((length, tc), lambda b, c: (b, nc + c)),
            pl.BlockSpec((length, tc), lambda b, c: (b, 2 * nc + c)),
            pl.BlockSpec((CONV_WIDTH, tc), lambda b, c: (0, c)),
        ],
        out_specs=pl.BlockSpec((length, tc), lambda b, c: (b, c)),
        scratch_shapes=[pltpu.VMEM((length + 16, tc), F32)],
        compiler_params=_cparams("parallel", "parallel"),
        name="conv_gate",
    )(proj, proj, proj, conv_w.astype(F32))


def _conv_mixer(u, rows, n_ctx, w_in, w_conv, w_out):
    proj = _matmul(u, w_in.astype(BF16), name="conv_in")
    zg = _conv_gate(proj, w_conv, rows, n_ctx)
    return _matmul(zg, w_out.astype(BF16), name="conv_out")


def _row_gather_copy(u_hbm, x_buf, sem, slot, tok, r):
    return pltpu.make_async_copy(u_hbm.at[pl.ds(tok, 1)], x_buf.at[slot, pl.ds(r, 1)], sem.at[slot])


def _moe_kernel(blk_exp_ref, n_used_ref, row_tok_ref, u_hbm, wgu_ref, bgu_ref, wdn_ref, bdn_ref, o_ref,
                x_buf, sem, wgu_bf_ref, wdn_bf_ref):
    i = pl.program_id(0)
    tm = o_ref.shape[0]
    n_used = n_used_ref[0]
    used = i < n_used
    slot = lax.rem(i, 2)

    @pl.when(used & (i == 0))
    def _():
        def body(r, carry):
            _row_gather_copy(u_hbm, x_buf, sem, 0, row_tok_ref[r], r).start()
            return carry
        lax.fori_loop(0, tm, body, 0, unroll=8)

    @pl.when(i + 1 < n_used)
    def _():
        base = (i + 1) * tm
        for r in range(tm):
            _row_gather_copy(u_hbm, x_buf, sem, 1 - slot, row_tok_ref[base + r], r).start(priority=r % 2)

    new_expert = (i == 0) | (blk_exp_ref[i] != blk_exp_ref[jnp.maximum(i - 1, 0)])

    @pl.when(used & new_expert)
    def _():
        wgu_bf_ref[...] = wgu_ref[...].astype(BF16)
        wdn_bf_ref[...] = wdn_ref[...].astype(BF16)

    @pl.when(used)
    def _():
        pltpu.make_async_copy(u_hbm.at[pl.ds(0, tm)], x_buf.at[slot], sem.at[slot]).wait()
        ff = wdn_ref.shape[0]
        gu = _dot(x_buf[slot].astype(BF16), wgu_bf_ref[...]) + bgu_ref[...]
        glu = jnp.minimum(gu[:, :ff], SWIGLU_LIMIT)
        lin = jnp.clip(gu[:, ff:], -SWIGLU_LIMIT, SWIGLU_LIMIT)
        act = glu * _sigmoid(SWIGLU_ALPHA * glu) * (lin + 1.0)
        o_ref[...] = (_dot(act.astype(BF16), wdn_bf_ref[...]) + bdn_ref[...]).astype(o_ref.dtype)

    @pl.when(i >= n_used)
    def _():
        o_ref[...] = jnp.zeros_like(o_ref)


def _expert_mlp(u, row_tok, blk_exp, n_used, layer, w_gate_up, b_gate_up, w_down, b_down):
    d = u.shape[1]
    n_rows = row_tok.shape[0]
    depth, n_exp, _, ff2 = w_gate_up.shape
    ff = ff2 // 2
    tm = MOE_ROW_BLOCK
    grid_spec = pltpu.PrefetchScalarGridSpec(
        num_scalar_prefetch=3,
        grid=(n_rows // tm,),
        in_specs=[
            pl.BlockSpec(memory_space=pl.ANY),
            pl.BlockSpec((None, None, d, ff2), lambda i, be, nu, rt: (layer, be[i], 0, 0)),
            pl.BlockSpec((None, None, 1, ff2), lambda i, be, nu, rt: (layer, be[i], 0, 0)),
            pl.BlockSpec((None, None, ff, d), lambda i, be, nu, rt: (layer, be[i], 0, 0)),
            pl.BlockSpec((None, None, 1, d), lambda i, be, nu, rt: (layer, be[i], 0, 0)),
        ],
        out_specs=pl.BlockSpec((tm, d), lambda i, be, nu, rt: (i, 0)),
        scratch_shapes=[pltpu.VMEM((2, tm, d), F32), pltpu.SemaphoreType.DMA((2,)),
                        pltpu.VMEM((d, ff2), BF16), pltpu.VMEM((ff, d), BF16)],
    )
    return pl.pallas_call(
        _moe_kernel,
        out_shape=jax.ShapeDtypeStruct((n_rows, d), BF16),
        grid_spec=grid_spec,
        compiler_params=pltpu.CompilerParams(dimension_semantics=("arbitrary",), vmem_limit_bytes=MOE_VMEM_LIMIT),
        name="expert_mlp",
    )(blk_exp, n_used, row_tok, u, w_gate_up, b_gate_up.reshape(depth, n_exp, 1, ff2),
      w_down, b_down.reshape(depth, n_exp, 1, d))


def _moe(u, route_idx, counts, layer, w_gate_up, b_gate_up, w_down, b_down):
    n_tok, d = u.shape
    tm = MOE_ROW_BLOCK
    n_assign = n_tok * TOP_K
    n_blocks = n_assign // tm + N_EXPERTS
    n_rows = n_blocks * tm
    top_exp = route_idx[:, :TOP_K]
    rank = route_idx[:, TOP_K:2 * TOP_K]
    counts = counts[0, :N_EXPERTS].astype(jnp.int32)
    padded = (counts + tm - 1) // tm * tm
    start = jnp.cumsum(counts) - counts
    pad_end = jnp.cumsum(padded)
    pad_start = pad_end - padded
    blk_start = jnp.arange(n_blocks, dtype=jnp.int32) * tm
    blk_exp = jnp.minimum(jnp.sum(pad_end[None, :] <= blk_start[:, None], axis=1), N_EXPERTS - 1).astype(jnp.int32)
    n_used = (pad_end[-1:] // tm).astype(jnp.int32)
    order = jnp.argsort(top_exp.reshape(-1), stable=True).astype(jnp.int32)
    row_exp = jnp.repeat(blk_exp, tm)
    within = jnp.arange(n_rows, dtype=jnp.int32) - pad_start[row_exp]
    src = jnp.clip(start[row_exp] + within, 0, n_assign - 1)
    row_tok = jnp.where(within < counts[row_exp], order[src] // TOP_K, 0)
    y_rows = _expert_mlp(u, row_tok, blk_exp, n_used, layer, w_gate_up, b_gate_up, w_down, b_down)
    pos = pad_start[top_exp] + rank
    return y_rows[pos.T.reshape(-1)].reshape(TOP_K, n_tok, d)


def kernel(x, c, ctx, c_ctx, ada_w, ada_b, ln_g, ln_b, gla_w_in, gla_gate_w1, gla_gate_w2, gla_gate_b, gla_norm_g, gla_w_out, swa_w_qkv, swa_b_qkv, swa_sink, swa_w_out, swa_b_out, conv_w_in, conv_w, conv_w_out, moe_router_w, moe_router_b, moe_w_gate_up, moe_b_gate_up, moe_w_down, moe_b_down):
    batch, seq, d = x.shape
    n_ctx = ctx.shape[1]
    depth = ada_w.shape[0]
    alpha = (2 * depth) ** 0.25
    rows_full = _Rows(batch, n_ctx // ROW_TILE, seq // ROW_TILE)
    rows_lat = _Rows(batch, 0, seq // ROW_TILE)

    cond = jnp.zeros((8, d), F32).at[0].set(c_ctx).at[1:1 + batch].set(c)
    ada = _ada_mods(cond, ada_w, ada_b)
    mods_ctx = jnp.broadcast_to(ada[:, 0:1], (depth, batch, N_MOD * d))
    mods = jnp.stack([mods_ctx, ada[:, 1:1 + batch]], axis=2).reshape(depth, batch, 2, N_MOD, d)

    router_w = jnp.zeros((depth, d, LANES), F32).at[:, :, :N_EXPERTS].set(moe_router_w)
    router_b = jnp.zeros((depth, 1, LANES), F32).at[:, 0, :N_EXPERTS].set(moe_router_b)

    h = jnp.concatenate([ctx, x], axis=1).reshape(rows_full.n_rows, d)
    u = _modulate(h, mods[0], rows_full, 0, 1)
    rows = rows_full
    for i in range(depth):
        last = i == depth - 1
        kind, j = i % 3, i // 3
        if kind == 0:
            y = _gla_mixer(u, rows, gla_w_in[j], gla_gate_w1[j], gla_gate_w2[j], gla_gate_b[j], gla_norm_g[j], gla_w_out[j])
        elif kind == 1:
            y = _window_mixer(u, rows, n_ctx, seq, swa_w_qkv[j], swa_b_qkv[j], swa_sink[j], swa_w_out[j], swa_b_out[j])
        else:
            y = _conv_mixer(u, rows, n_ctx, conv_w_in[j], conv_w[j], conv_w_out[j])
        rows_out = rows_lat if last else rows
        h, u, route_idx, route_gate, counts = _resid_ln(
            h, y, mods[i], mods[i], ln_g[i, 0], ln_b[i, 0], rows, rows_out, alpha=alpha, gate_idx=2,
            next_mod=(3, 4), router=(router_w[i], router_b[i]))
        rows = rows_out
        y = _moe(u, route_idx, counts, i, moe_w_gate_up, moe_b_gate_up, moe_w_down, moe_b_down)
        if last:
            (h,) = _resid_ln(h, y, mods[i], mods[i], ln_g[i, 1], ln_b[i, 1], rows, rows, alpha=alpha, gate_idx=5,
                             combine_gates=route_gate)
        else:
            h, u = _resid_ln(h, y, mods[i], mods[i + 1], ln_g[i, 1], ln_b[i, 1], rows, rows, alpha=alpha, gate_idx=5,
                             next_mod=(0, 1), combine_gates=route_gate)
    return h.reshape(batch, seq, d)
```

```python
import functools

import jax
import jax.numpy as jnp
from jax import lax
from jax.experimental import pallas as pl
from jax.experimental.pallas import tpu as pltpu

F32 = jnp.float32
BF16 = jnp.bfloat16
HIGHEST = lax.Precision.HIGHEST

GRID_W = 64
GLA_HEADS = 4
GLA_GATE_RANK = 16
GLA_GATE_NORM = 16.0
GLA_CHUNK = 64
SWA_HEAD_DIM = 64
SWA_GROUP = 8
WINDOW = 128
ROPE_BASE = 10000.0
CONV_WIDTH = 3
N_EXPERTS = 32
TOP_K = 4
SWIGLU_LIMIT = 7.0
SWIGLU_ALPHA = 1.702
N_MOD = 6
LN_EPS = 1e-5
NEG_INF = -1e30

LANES = 128
ROW_TILE = 256
MOE_ROW_BLOCK = 256
VMEM_LIMIT = 48 * 1024 * 1024
MOE_VMEM_LIMIT = 60 * 1024 * 1024


def _cparams(*sem):
    return pltpu.CompilerParams(dimension_semantics=sem, vmem_limit_bytes=VMEM_LIMIT)


def _sigmoid(x):
    return 1.0 / (1.0 + jnp.exp(-x))


def _dot(a, b):
    return jnp.dot(a, b, preferred_element_type=F32)


def _dot_nt(a, b):
    return lax.dot_general(a, b, (((1,), (1,)), ((), ())), preferred_element_type=F32)


def _split_bf16(x):
    hi = x.astype(BF16)
    lo = (x - hi.astype(F32)).astype(BF16)
    return hi, lo


class _Rows:
    def __init__(self, batch, nb_ctx, nb_lat, tile=ROW_TILE):
        self.batch, self.nb_ctx, self.nb_lat, self.tile = batch, nb_ctx, nb_lat, tile
        self.nb_seq = nb_ctx + nb_lat
        self.n_blocks = batch * self.nb_seq
        self.n_rows = self.n_blocks * tile

    def batch_of(self, i):
        return i // self.nb_seq

    def seg_of(self, i):
        if self.nb_ctx == 0:
            return 1
        return jnp.where(i % self.nb_seq >= self.nb_ctx, 1, 0)

    def latent_block(self, i):
        return (i // self.nb_lat) * self.nb_seq + self.nb_ctx + i % self.nb_lat


def _ada_kernel(c_ref, w_ref, b_ref, o_ref):
    c = c_ref[...]
    s = c * _sigmoid(c)
    o_ref[...] = jnp.dot(s, w_ref[...], precision=HIGHEST, preferred_element_type=F32) + b_ref[...]


def _ada_mods(cond, ada_w, ada_b):
    depth, d, n = ada_w.shape
    tn = 1024 if n % 1024 == 0 else n
    return pl.pallas_call(
        _ada_kernel,
        out_shape=jax.ShapeDtypeStruct((depth, 8, n), F32),
        grid=(depth, n // tn),
        in_specs=[
            pl.BlockSpec((8, d), lambda l, j: (0, 0)),
            pl.BlockSpec((None, d, tn), lambda l, j: (l, 0, j)),
            pl.BlockSpec((None, 1, tn), lambda l, j: (l, 0, j)),
        ],
        out_specs=pl.BlockSpec((None, 8, tn), lambda l, j: (l, 0, j)),
        compiler_params=_cparams("parallel", "parallel"),
        name="ada_mods",
    )(cond, ada_w, ada_b.reshape(depth, 1, n))


def _modulate_kernel(h_ref, m_ref, u_ref, *, shift_idx, scale_idx):
    h = h_ref[...]
    u = h * (1.0 + m_ref[scale_idx:scale_idx + 1, :]) + m_ref[shift_idx:shift_idx + 1, :]
    u_ref[...] = u.astype(u_ref.dtype)


def _modulate(h, mods, rows, shift_idx, scale_idx):
    d = h.shape[1]
    t = rows.tile
    return pl.pallas_call(
        functools.partial(_modulate_kernel, shift_idx=shift_idx, scale_idx=scale_idx),
        out_shape=jax.ShapeDtypeStruct(h.shape, BF16),
        grid=(rows.n_blocks,),
        in_specs=[
            pl.BlockSpec((t, d), lambda i: (i, 0)),
            pl.BlockSpec((None, None, N_MOD, d), lambda i: (rows.batch_of(i), rows.seg_of(i), 0, 0)),
        ],
        out_specs=pl.BlockSpec((t, d), lambda i: (i, 0)),
        compiler_params=_cparams("parallel"),
        name="modulate",
    )(h, mods)


def _mm_kernel(x_ref, w_ref, o_ref):
    o_ref[...] = _dot(x_ref[...], w_ref[...]).astype(o_ref.dtype)


def _mm_bias_kernel(x_ref, w_ref, b_ref, o_ref):
    o_ref[...] = (_dot(x_ref[...], w_ref[...]) + b_ref[...]).astype(o_ref.dtype)


def _pick_tile(n, candidates):
    for c in candidates:
        if n % c == 0:
            return c
    return n


def _matmul(x, w, bias=None, out_dtype=BF16, name="proj"):
    m, k = x.shape
    n = w.shape[1]
    tm = _pick_tile(m, (1024, 512, 256))
    tn = _pick_tile(n, (1024, 768, 640, 512, 256, 128))
    in_specs = [
        pl.BlockSpec((tm, k), lambda i, j: (i, 0)),
        pl.BlockSpec((k, tn), lambda i, j: (0, j)),
    ]
    args = [x, w]
    body = _mm_kernel
    if bias is not None:
        in_specs.append(pl.BlockSpec((1, tn), lambda i, j: (0, j)))
        args.append(bias.reshape(1, n).astype(F32))
        body = _mm_bias_kernel
    return pl.pallas_call(
        body,
        out_shape=jax.ShapeDtypeStruct((m, n), out_dtype),
        grid=(m // tm, n // tn),
        in_specs=in_specs,
        out_specs=pl.BlockSpec((tm, tn), lambda i, j: (i, j)),
        compiler_params=_cparams("parallel", "parallel"),
        name=name,
    )(*args)


def _route_block(logits, count_ref):
    t = logits.shape[0]
    lane = lax.broadcasted_iota(jnp.int32, (t, LANES), 1)
    lane_f = lane.astype(F32)
    left = jnp.where(lane < N_EXPERTS, logits, NEG_INF)
    hot = jnp.zeros((t, LANES), F32)
    sels, vals = [], []
    for _ in range(TOP_K):
        m = jnp.max(left, axis=-1, keepdims=True)
        sel = jnp.min(jnp.where(left == m, lane_f, float(LANES)), axis=-1, keepdims=True)
        pick = lane_f == sel
        left = jnp.where(pick, NEG_INF, left)
        hot = jnp.where(pick, 1.0, hot)
        sels.append(sel)
        vals.append(m)
    row = lax.broadcasted_iota(jnp.int32, (t, t), 0)
    col = lax.broadcasted_iota(jnp.int32, (t, t), 1)
    earlier = jnp.where(row > col, 1.0, 0.0).astype(BF16)
    before = _dot(earlier, hot.astype(BF16)) + count_ref[...]
    count_ref[...] += jnp.sum(hot, axis=0, keepdims=True)
    exps = [jnp.exp(v - vals[0]) for v in vals]
    denom = exps[0]
    for e in exps[1:]:
        denom = denom + e
    idx = jnp.zeros((t, LANES), jnp.int32)
    gates = jnp.zeros((t, LANES), F32)
    for k in range(TOP_K):
        rank = jnp.sum(jnp.where(lane_f == sels[k], before, 0.0), axis=-1, keepdims=True)
        idx = jnp.where(lane == k, sels[k].astype(jnp.int32), idx)
        idx = jnp.where(lane == TOP_K + k, rank.astype(jnp.int32), idx)
        gates = jnp.where(lane == k, exps[k] / denom, gates)
    return idx, gates


def _resid_ln_kernel(*refs, alpha, gate_idx, shift_idx, scale_idx, with_u, with_router, combine):
    h_ref, y_ref = refs[:2]
    pos = 2
    if combine:
        yg_ref = refs[pos]
        pos += 1
    m_ref, mn_ref, g_ref, b_ref = refs[pos:pos + 4]
    pos += 4
    if with_router:
        wr_hi_ref, wr_lo_ref, br_ref = refs[pos:pos + 3]
        pos += 3
    hn_ref = refs[pos]
    pos += 1
    if combine:
        gates = yg_ref[...]
        y = gates[:, 0:1] * y_ref[0].astype(F32)
        for k in range(1, TOP_K):
            y = y + gates[:, k:k + 1] * y_ref[k].astype(F32)
    else:
        y = y_ref[...].astype(F32)
    z = alpha * h_ref[...] + y * m_ref[gate_idx:gate_idx + 1, :]
    zc = z - jnp.mean(z, axis=-1, keepdims=True)
    var = jnp.mean(zc * zc, axis=-1, keepdims=True)
    hn = zc * lax.rsqrt(var + LN_EPS) * g_ref[...] + b_ref[...]
    hn_ref[...] = hn
    if with_u:
        u_ref = refs[pos]
        pos += 1
        u = hn * (1.0 + mn_ref[scale_idx:scale_idx + 1, :]) + mn_ref[shift_idx:shift_idx + 1, :]
        u_ref[...] = u.astype(u_ref.dtype)
        if with_router:
            idx_ref, gate_ref, cnt_ref, count_ref = refs[pos:pos + 4]

            @pl.when(pl.program_id(0) == 0)
            def _():
                count_ref[...] = jnp.zeros_like(count_ref)

            u_hi, u_lo = _split_bf16(u)
            w_hi = wr_hi_ref[...]
            logits = _dot(u_hi, w_hi) + _dot(u_lo, w_hi) + _dot(u_hi, wr_lo_ref[...]) + br_ref[...]
            idx, gates = _route_block(logits, count_ref)
            idx_ref[...] = idx
            gate_ref[...] = gates
            cnt_ref[...] = count_ref[...]


def _resid_ln(h, y, mods, mods_next, ln_g, ln_b, rows_in, rows_out, *, alpha, gate_idx, next_mod=None, router=None,
              combine_gates=None):
    d = h.shape[1]
    t = rows_out.tile
    if rows_in is rows_out:
        in_row = lambda i: i
    else:
        in_row = rows_in.latent_block
    with_u = next_mod is not None
    with_router = router is not None
    combine = combine_gates is not None
    shift_idx, scale_idx = next_mod if with_u else (0, 0)
    mod_spec = pl.BlockSpec((None, None, N_MOD, d), lambda i: (rows_out.batch_of(i), rows_out.seg_of(i), 0, 0))
    in_specs = [pl.BlockSpec((t, d), lambda i: (in_row(i), 0))]
    args = [h, y]
    if combine:
        in_specs += [pl.BlockSpec((TOP_K, t, d), lambda i: (0, in_row(i), 0)),
                     pl.BlockSpec((t, LANES), lambda i: (in_row(i), 0))]
        args.append(combine_gates)
    else:
        in_specs.append(pl.BlockSpec((t, d), lambda i: (in_row(i), 0)))
    in_specs += [mod_spec, mod_spec, pl.BlockSpec((1, d), lambda i: (0, 0)), pl.BlockSpec((1, d), lambda i: (0, 0))]
    args += [mods, mods_next, ln_g.reshape(1, d), ln_b.reshape(1, d)]
    out_shape = [jax.ShapeDtypeStruct((rows_out.n_rows, d), F32)]
    out_specs = [pl.BlockSpec((t, d), lambda i: (i, 0))]
    scratch = []
    if with_router:
        wr, br = router
        w_spec = pl.BlockSpec((d, LANES), lambda i: (0, 0))
        in_specs += [w_spec, w_spec, pl.BlockSpec((1, LANES), lambda i: (0, 0))]
        args += [*_split_bf16(wr), br]
    if with_u:
        out_shape.append(jax.ShapeDtypeStruct((rows_out.n_rows, d), F32 if with_router else BF16))
        out_specs.append(pl.BlockSpec((t, d), lambda i: (i, 0)))
    if with_router:
        out_shape += [jax.ShapeDtypeStruct((rows_out.n_rows, LANES), jnp.int32),
                      jax.ShapeDtypeStruct((rows_out.n_rows, LANES), F32),
                      jax.ShapeDtypeStruct((1, LANES), F32)]
        out_specs += [pl.BlockSpec((t, LANES), lambda i: (i, 0)), pl.BlockSpec((t, LANES), lambda i: (i, 0)),
                      pl.BlockSpec((1, LANES), lambda i: (0, 0))]
        scratch.append(pltpu.VMEM((1, LANES), F32))
    return pl.pallas_call(
        functools.partial(_resid_ln_kernel, alpha=alpha, gate_idx=gate_idx, shift_idx=shift_idx,
                          scale_idx=scale_idx, with_u=with_u, with_router=with_router, combine=combine),
        out_shape=out_shape,
        grid=(rows_out.n_blocks,),
        in_specs=in_specs,
        out_specs=out_specs,
        scratch_shapes=scratch,
        compiler_params=_cparams("arbitrary" if with_router else "parallel"),
        name="resid_ln",
    )(*args)


def _gla_kernel(*refs, reverse, fuse_post, q_scale, dk, dv):
    if fuse_post:
        q_ref, k_ref, v_ref, r_ref, w2_ref, gb_ref, other_ref, g_ref, ng_ref, out_ref, state_ref = refs
    else:
        q_ref, k_ref, v_ref, r_ref, w2_ref, gb_ref, out_ref, state_ref = refs

    @pl.when(pl.program_id(1) == 0)
    def _():
        state_ref[...] = jnp.zeros_like(state_ref)

    t = q_ref.shape[0]
    c = GLA_CHUNK
    shift = c.bit_length() - 1
    r_hi, r_lo = _split_bf16(r_ref[...])
    w_hi, w_lo = _split_bf16(w2_ref[...])
    z = _dot(r_hi, w_hi) + _dot(r_lo, w_hi) + _dot(r_hi, w_lo) + gb_ref[...]
    log_a = (jnp.minimum(z, 0.0) - jnp.log(1.0 + jnp.exp(-jnp.abs(z)))) * (1.0 / GLA_GATE_NORM)
    row = lax.broadcasted_iota(jnp.int32, (t, t), 0)
    col = lax.broadcasted_iota(jnp.int32, (t, t), 1)
    same_chunk = lax.shift_right_logical(row, shift) == lax.shift_right_logical(col, shift)
    ahead = (col - row) if reverse else (row - col)
    tri = jnp.where(jnp.where(same_chunk, ahead, -1) >= 0, 1.0, 0.0).astype(BF16)
    a_hi, a_lo = _split_bf16(log_a)
    cum = _dot(tri, a_hi) + _dot(tri, a_lo)
    grow = jnp.exp(cum)
    shrink = jnp.exp(-cum)
    crow = lax.broadcasted_iota(jnp.int32, (c, c), 0)
    ccol = lax.broadcasted_iota(jnp.int32, (c, c), 1)
    causal = (ccol >= crow) if reverse else (crow >= ccol)
    n_chunks = t // c
    for h in range(GLA_HEADS):
        ks = slice(h * dk, (h + 1) * dk)
        vs = slice(h * dv, (h + 1) * dv)
        q_dec = (q_ref[:, ks].astype(F32) * q_scale * grow[:, ks]).astype(BF16)
        k_in = k_ref[:, ks].astype(F32) * shrink[:, ks]
        for step in range(n_chunks):
            j = n_chunks - 1 - step if reverse else step
            rs = slice(j * c, (j + 1) * c)
            edge = j * c if reverse else (j + 1) * c - 1
            decay = jnp.exp(cum[edge:edge + 1, ks])
            qd = q_dec[rs]
            k_intra = k_in[rs].astype(BF16)
            k_state = (k_in[rs] * decay).astype(BF16)
            v = v_ref[rs, vs]
            scores = jnp.where(causal, _dot_nt(qd, k_intra), 0.0)
            state_t = state_ref[h]
            o = _dot(scores.astype(BF16), v) + _dot_nt(qd, state_t.astype(BF16))
            kv_t = lax.dot_general(v, k_state, (((0,), (0,)), ((), ())), preferred_element_type=F32)
            state_ref[h] = state_t * decay + kv_t
            if fuse_post:
                o = o + other_ref[rs, vs]
                o = o * lax.rsqrt(jnp.mean(o * o, axis=-1, keepdims=True) + LN_EPS) * ng_ref[...]
                g = g_ref[rs, vs].astype(F32)
                out_ref[rs, vs] = (o * (g * _sigmoid(g))).astype(out_ref.dtype)
            else:
                out_ref[rs, vs] = o


def _gla_scan(qkvg, r, w2, gate_b, rows, direction, other=None, norm_g=None):
    t_rows = qkvg.shape[0]
    dk_tot = w2.shape[2]
    dk = dk_tot // GLA_HEADS
    dv_tot = (qkvg.shape[1] - 2 * dk_tot) // 2
    dv = dv_tot // GLA_HEADS
    t = rows.tile
    reverse = direction == 1
    fuse_post = other is not None

    def block(b, n):
        if reverse:
            n = jnp.where(n < rows.nb_ctx, rows.nb_ctx - 1 - n, rows.nb_seq + rows.nb_ctx - 1 - n)
        return b * rows.nb_seq + n

    in_specs = [
        pl.BlockSpec((t, dk_tot), lambda b, n: (block(b, n), 0)),
        pl.BlockSpec((t, dk_tot), lambda b, n: (block(b, n), 1)),
        pl.BlockSpec((t, dv_tot), lambda b, n: (block(b, n), (2 * dk_tot) // dv_tot)),
        pl.BlockSpec((t, LANES), lambda b, n: (block(b, n), direction)),
        pl.BlockSpec((None, LANES, dk_tot), lambda b, n: (direction, 0, 0)),
        pl.BlockSpec((None, 1, dk_tot), lambda b, n: (direction, 0, 0)),
    ]
    args = [qkvg, qkvg, qkvg, r, w2, gate_b]
    if fuse_post:
        in_specs += [
            pl.BlockSpec((t, dv_tot), lambda b, n: (block(b, n), 0)),
            pl.BlockSpec((t, dv_tot), lambda b, n: (block(b, n), (2 * dk_tot + dv_tot) // dv_tot)),
            pl.BlockSpec((1, dv), lambda b, n: (0, 0)),
        ]
        args += [other, qkvg, norm_g.reshape(1, dv).astype(F32)]
    return pl.pallas_call(
        functools.partial(_gla_kernel, reverse=reverse, fuse_post=fuse_post, q_scale=float(dk) ** -0.5, dk=dk, dv=dv),
        out_shape=jax.ShapeDtypeStruct((t_rows, dv_tot), BF16 if fuse_post else F32),
        grid=(rows.batch, rows.nb_seq),
        in_specs=in_specs,
        out_specs=pl.BlockSpec((t, dv_tot), lambda b, n: (block(b, n), 0)),
        scratch_shapes=[pltpu.VMEM((GLA_HEADS, dv, dk), F32)],
        compiler_params=_cparams("parallel", "arbitrary"),
        name="gla_scan_bwd" if reverse else "gla_scan_fwd",
    )(*args)


def _gla_mixer(u, rows, w_in, gate_w1, gate_w2, gate_b, norm_g, w_out):
    d = u.shape[1]
    dk_tot = gate_w2.shape[2]
    qkvg = _matmul(u, w_in.astype(BF16), name="gla_in")
    w1 = jnp.zeros((d, 2 * LANES), F32)
    w2 = jnp.zeros((2, LANES, dk_tot), F32)
    for dd in range(2):
        w1 = w1.at[:, dd * LANES:dd * LANES + GLA_GATE_RANK].set(gate_w1[dd])
        w2 = w2.at[dd, :GLA_GATE_RANK].set(gate_w2[dd])
    r = _matmul(u, w1.astype(BF16), out_dtype=F32, name="gla_gate_in")
    gate_b = gate_b.reshape(2, 1, dk_tot).astype(F32)
    o_fwd = _gla_scan(qkvg, r, w2, gate_b, rows, 0)
    y = _gla_scan(qkvg, r, w2, gate_b, rows, 1, other=o_fwd, norm_g=norm_g)
    return _matmul(y, w_out.astype(BF16), name="gla_out")


def _rope_kernel(x_ref, cos_ref, sin_ref, q_ref, k2_ref, v2_ref, *, q_width, kv_width):
    cos = cos_ref[...]
    sin = sin_ref[...]
    lane = lax.broadcasted_iota(jnp.int32, cos.shape, 1)
    first = (lane % 32) < 16
    low = lane < SWA_HEAD_DIM

    def rope(slab):
        partner = jnp.where(first, pltpu.roll(slab, LANES - 16, 1), pltpu.roll(slab, 16, 1))
        return slab * cos + partner * sin

    scale = SWA_HEAD_DIM ** -0.5
    for s in range(q_width // LANES):
        slab = x_ref[:, s * LANES:(s + 1) * LANES].astype(F32)
        q_ref[:, s * LANES:(s + 1) * LANES] = (rope(slab) * scale).astype(q_ref.dtype)
    for s in range(kv_width // LANES):
        k_slab = rope(x_ref[:, q_width + s * LANES:q_width + (s + 1) * LANES].astype(F32))
        v_slab = x_ref[:, q_width + kv_width + s * LANES:q_width + kv_width + (s + 1) * LANES].astype(F32)
        for slab, out_ref in ((k_slab, k2_ref), (v_slab, v2_ref)):
            swapped = pltpu.roll(slab, SWA_HEAD_DIM, 1)
            out_ref[2 * s] = jnp.where(low, slab, swapped).astype(out_ref.dtype)
            out_ref[2 * s + 1] = jnp.where(low, swapped, slab).astype(out_ref.dtype)


def _rope_tables(n_ctx, seq):
    t = jnp.arange(seq, dtype=jnp.int32)
    row = (t // GRID_W).astype(F32)
    col = (t % GRID_W).astype(F32)
    n_freq = SWA_HEAD_DIM // 4
    inv_freq = ROPE_BASE ** (-jnp.arange(n_freq, dtype=F32) / n_freq)
    lane = jnp.arange(LANES)
    within = lane % SWA_HEAD_DIM
    pos = jnp.where((within < SWA_HEAD_DIM // 2)[None, :], row[:, None], col[:, None])
    ang = pos * inv_freq[lane % n_freq][None, :]
    sign = jnp.where((lane % 32) < 16, -1.0, 1.0)[None, :]
    cos = jnp.concatenate([jnp.ones((n_ctx, LANES), F32), jnp.cos(ang)], axis=0)
    sin = jnp.concatenate([jnp.zeros((n_ctx, LANES), F32), jnp.sin(ang) * sign], axis=0)
    return cos, sin


def _rope_split(qkv, cos, sin, rows, q_width, kv_width):
    t_rows = qkv.shape[0]
    t = rows.tile
    n_kv = kv_width // SWA_HEAD_DIM
    seq_rows = rows.nb_seq * t
    kv_shape = jax.ShapeDtypeStruct((rows.batch, n_kv, seq_rows, LANES), BF16)
    kv_spec = pl.BlockSpec((None, n_kv, t, LANES), lambda i: (i // rows.nb_seq, 0, i % rows.nb_seq, 0))
    return pl.pallas_call(
        functools.partial(_rope_kernel, q_width=q_width, kv_width=kv_width),
        out_shape=[jax.ShapeDtypeStruct((t_rows, q_width), BF16), kv_shape, kv_shape],
        grid=(rows.n_blocks,),
        in_specs=[
            pl.BlockSpec((t, qkv.shape[1]), lambda i: (i, 0)),
            pl.BlockSpec((t, LANES), lambda i: (i % rows.nb_seq, 0)),
            pl.BlockSpec((t, LANES), lambda i: (i % rows.nb_seq, 0)),
        ],
        out_specs=[pl.BlockSpec((t, q_width), lambda i: (i, 0)), kv_spec, kv_spec],
        compiler_params=_cparams("parallel"),
        name="rope_split",
    )(qkv, cos, sin)


def _attend_heads(q_ref, kk, vv, mask, sink_ref, head0, o_ref):
    tq = q_ref.shape[0]
    lane = lax.broadcasted_iota(jnp.int32, (tq, LANES), 1)
    low = lane < SWA_HEAD_DIM
    q_all = q_ref[...]
    sinks = [sink_ref[head0 + h] for h in range(SWA_GROUP)]
    pair_outs = []
    for pair in range(SWA_GROUP // 2):
        q2 = q_all[:, pair * LANES:(pair + 1) * LANES]
        outs = []
        for half in range(2):
            qh = jnp.where(low if half == 0 else lane >= SWA_HEAD_DIM, q2, jnp.zeros_like(q2))
            s = _dot_nt(qh, kk)
            if mask is not None:
                s = jnp.where(mask, s, NEG_INF)
            sink = sinks[2 * pair + half]
            m = jnp.maximum(jnp.max(s, axis=-1, keepdims=True), sink)
            p = jnp.exp(s - m)
            denom = jnp.sum(p, axis=-1, keepdims=True) + jnp.exp(sink - m)
            outs.append(_dot(p.astype(BF16), vv) / denom)
        pair_outs.append(jnp.where(low, outs[0], outs[1]).astype(o_ref.dtype))
    o_ref[...] = jnp.concatenate(pair_outs, axis=1)


def _attn_kernel(sink_ref, q_ref, kc_ref, vc_ref, kp_ref, vp_ref, km_ref, vm_ref, kn_ref, vn_ref,
                 o_ref, kk_ref, vv_ref, *, n_ctx, nq_ctx, nq_lat):
    n = pl.program_id(2)
    head0 = pl.program_id(1) * SWA_GROUP
    w = WINDOW

    @pl.when(n < nq_ctx)
    def _():
        _attend_heads(q_ref, kc_ref[...], vc_ref[...], None, sink_ref, head0, o_ref)

    @pl.when(n >= nq_ctx)
    def _():
        m = n - nq_ctx
        for ref, parts in ((kk_ref, (kc_ref, kp_ref, km_ref, kn_ref)), (vv_ref, (vc_ref, vp_ref, vm_ref, vn_ref))):
            ref[0:n_ctx] = parts[0][...]
            ref[n_ctx:n_ctx + w] = parts[1][...]
            ref[n_ctx + w:n_ctx + 2 * w] = parts[2][...]
            ref[n_ctx + 2 * w:n_ctx + 3 * w] = parts[3][...]
        nk = n_ctx + 3 * w
        qpos = lax.broadcasted_iota(jnp.int32, (w, nk), 0)
        col = lax.broadcasted_iota(jnp.int32, (w, nk), 1)
        rel = jnp.where(col < n_ctx, 0, col - (n_ctx + w) - qpos)
        far = 4 * w
        no_prev = jnp.where(m > 0, 0, far)
        no_next = jnp.where(m < nq_lat - 1, 0, far)
        missing = jnp.where(col < n_ctx, 0,
                            jnp.where(col < n_ctx + w, no_prev, jnp.where(col >= n_ctx + 2 * w, no_next, 0)))
        mask = jnp.abs(rel) + missing <= w
        _attend_heads(q_ref, kk_ref[...], vv_ref[...], mask, sink_ref, head0, o_ref)


def _window_attention(q, k2, v2, sink, rows, n_ctx):
    t_rows, q_width = q.shape
    batch, n_kv, seq_rows, _ = k2.shape
    w = WINDOW
    gw = SWA_GROUP * SWA_HEAD_DIM
    nq_seq = seq_rows // w
    nq_ctx = n_ctx // w
    nq_lat = nq_seq - nq_ctx
    ctx_kv = pl.BlockSpec((None, None, n_ctx, LANES), lambda b, g, n: (b, g, 0, 0))

    def band(shift):
        def index(b, g, n):
            return (b, g, nq_ctx + jnp.clip(n - nq_ctx + shift, 0, nq_lat - 1), 0)
        return pl.BlockSpec((None, None, w, LANES), index)

    nk = n_ctx + 3 * w
    return pl.pallas_call(
        functools.partial(_attn_kernel, n_ctx=n_ctx, nq_ctx=nq_ctx, nq_lat=nq_lat),
        out_shape=jax.ShapeDtypeStruct((t_rows, q_width), BF16),
        grid=(batch, n_kv, nq_seq),
        in_specs=[pl.BlockSpec(memory_space=pltpu.SMEM),
                  pl.BlockSpec((w, gw), lambda b, g, n: (b * nq_seq + n, g)),
                  ctx_kv, ctx_kv, band(-1), band(-1), band(0), band(0), band(1), band(1)],
        out_specs=pl.BlockSpec((w, gw), lambda b, g, n: (b * nq_seq + n, g)),
        scratch_shapes=[pltpu.VMEM((nk, LANES), BF16), pltpu.VMEM((nk, LANES), BF16)],
        compiler_params=_cparams("parallel", "parallel", "parallel"),
        name="window_attn",
    )(sink, q, k2, v2, k2, v2, k2, v2, k2, v2)


def _window_mixer(u, rows, n_ctx, seq, w_qkv, b_qkv, sink, w_out, b_out):
    q_width = w_out.shape[0]
    kv_width = (w_qkv.shape[1] - q_width) // 2
    qkv = _matmul(u, w_qkv.astype(BF16), bias=b_qkv, name="swa_qkv")
    cos, sin = _rope_tables(n_ctx, seq)
    q, k2, v2 = _rope_split(qkv, cos, sin, rows, q_width, kv_width)
    o = _window_attention(q, k2, v2, sink.astype(F32), rows, n_ctx)
    return _matmul(o, w_out.astype(BF16), bias=b_out, name="swa_out")


def _conv_kernel(gi_ref, go_ref, val_ref, w_ref, o_ref, p_ref, *, n_ctx):
    length = gi_ref.shape[0]
    pad = 8
    p_ref[0:pad] = jnp.zeros((pad, p_ref.shape[1]), F32)
    p_ref[pad + length:pad + length + pad] = jnp.zeros((pad, p_ref.shape[1]), F32)
    p_ref[pad:pad + length] = gi_ref[...].astype(F32) * val_ref[...].astype(F32)
    t = lax.broadcasted_iota(jnp.int32, (length, 1), 0)
    seg_start = t * (t - n_ctx) == 0
    seg_end = (t - (n_ctx - 1)) * (t - (length - 1)) == 0
    prev = jnp.where(seg_start, 0.0, p_ref[pad - 1:pad - 1 + length])
    nxt = jnp.where(seg_end, 0.0, p_ref[pad + 1:pad + 1 + length])
    z = prev * w_ref[0:1, :] + p_ref[pad:pad + length] * w_ref[1:2, :] + nxt * w_ref[2:3, :]
    o_ref[...] = (go_ref[...].astype(F32) * z).astype(o_ref.dtype)


def _conv_gate(proj, conv_w, rows, n_ctx):
    t_rows = proj.shape[0]
    d = proj.shape[1] // 3
    length = rows.nb_seq * rows.tile
    tc = LANES
    nc = d // tc
    return pl.pallas_call(
        functools.partial(_conv_kernel, n_ctx=n_ctx),
        out_shape=jax.ShapeDtypeStruct((t_rows, d), BF16),
        grid=(rows.batch, nc),
        in_specs=[
            pl.BlockSpec((length, tc), lambda b, c: (b, c)),
            pl.BlockSpec((length, tc), lambda b, c: (b, nc + c)),
            pl.BlockSpec((length, tc), lambda b, c: (b, 2 * nc + c)),
            pl.BlockSpec((CONV_WIDTH, tc), lambda b, c: (0, c)),
        ],
        out_specs=pl.BlockSpec((length, tc), lambda b, c: (b, c)),
        scratch_shapes=[pltpu.VMEM((length + 16, tc), F32)],
        compiler_params=_cparams("parallel", "parallel"),
        name="conv_gate",
    )(proj, proj, proj, conv_w.astype(F32))


def _conv_mixer(u, rows, n_ctx, w_in, w_conv, w_out):
    proj = _matmul(u, w_in.astype(BF16), name="conv_in")
    zg = _conv_gate(proj, w_conv, rows, n_ctx)
    return _matmul(zg, w_out.astype(BF16), name="conv_out")


def _row_gather_copy(u_hbm, x_buf, sem, slot, tok, r):
    return pltpu.make_async_copy(u_hbm.at[pl.ds(tok, 1)], x_buf.at[slot, pl.ds(r, 1)], sem.at[slot])


def _moe_kernel(blk_exp_ref, n_used_ref, row_tok_ref, u_hbm, wgu_ref, bgu_ref, wdn_ref, bdn_ref, o_ref,
                x_buf, sem, wgu_bf_ref, wdn_bf_ref):
    i = pl.program_id(0)
    tm = o_ref.shape[0]
    n_used = n_used_ref[0]
    used = i < n_used
    slot = lax.rem(i, 3)

    def wait_gather():
        pltpu.make_async_copy(u_hbm.at[pl.ds(0, tm)], x_buf.at[slot], sem.at[slot]).wait()

    @pl.when(used & (i == 0))
    def _():
        def body(r, carry):
            _row_gather_copy(u_hbm, x_buf, sem, 0, row_tok_ref[r], r).start()
            _row_gather_copy(u_hbm, x_buf, sem, 1, row_tok_ref[tm + r], r).start(priority=1)
            return carry
        lax.fori_loop(0, tm, body, 0, unroll=4)

    new_expert = (i == 0) | (blk_exp_ref[i] != blk_exp_ref[jnp.maximum(i - 1, 0)])

    @pl.when(used & new_expert)
    def _():
        wgu_bf_ref[...] = wgu_ref[...].astype(BF16)
        wdn_bf_ref[...] = wdn_ref[...].astype(BF16)

    @pl.when(used)
    def _():
        wait_gather()
        x = x_buf[slot].astype(BF16)
        base = (i + 2) * tm
        ahead = lax.rem(i + 2, 3)
        for r in range(tm):
            _row_gather_copy(u_hbm, x_buf, sem, ahead, row_tok_ref[base + r], r).start(priority=r % 2)
        ff = wdn_ref.shape[0]
        gu = _dot(x, wgu_bf_ref[...]) + bgu_ref[...]
        glu = jnp.minimum(gu[:, :ff], SWIGLU_LIMIT)
        lin = jnp.clip(gu[:, ff:], -SWIGLU_LIMIT, SWIGLU_LIMIT)
        act = glu * _sigmoid(SWIGLU_ALPHA * glu) * (lin + 1.0)
        o_ref[...] = (_dot(act.astype(BF16), wdn_bf_ref[...]) + bdn_ref[...]).astype(o_ref.dtype)

    @pl.when((i >= n_used) & (i <= n_used + 1) & (n_used > 0))
    def _():
        wait_gather()

    @pl.when(i >= n_used)
    def _():
        o_ref[...] = jnp.zeros_like(o_ref)


def _expert_mlp(u, row_tok, blk_exp, n_used, layer, w_gate_up, b_gate_up, w_down, b_down):
    d = u.shape[1]
    n_rows = row_tok.shape[0]
    depth, n_exp, _, ff2 = w_gate_up.shape
    ff = ff2 // 2
    tm = MOE_ROW_BLOCK
    grid_spec = pltpu.PrefetchScalarGridSpec(
        num_scalar_prefetch=3,
        grid=(n_rows // tm,),
        in_specs=[
            pl.BlockSpec(memory_space=pl.ANY),
            pl.BlockSpec((None, None, d, ff2), lambda i, be, nu, rt: (layer, be[i], 0, 0)),
            pl.BlockSpec((None, None, 1, ff2), lambda i, be, nu, rt: (layer, be[i], 0, 0)),
            pl.BlockSpec((None, None, ff, d), lambda i, be, nu, rt: (layer, be[i], 0, 0)),
            pl.BlockSpec((None, None, 1, d), lambda i, be, nu, rt: (layer, be[i], 0, 0)),
        ],
        out_specs=pl.BlockSpec((tm, d), lambda i, be, nu, rt: (i, 0)),
        scratch_shapes=[pltpu.VMEM((3, tm, d), F32), pltpu.SemaphoreType.DMA((3,)),
                        pltpu.VMEM((d, ff2), BF16), pltpu.VMEM((ff, d), BF16)],
    )
    return pl.pallas_call(
        _moe_kernel,
        out_shape=jax.ShapeDtypeStruct((n_rows, d), BF16),
        grid_spec=grid_spec,
        compiler_params=pltpu.CompilerParams(dimension_semantics=("arbitrary",), vmem_limit_bytes=MOE_VMEM_LIMIT),
        name="expert_mlp",
    )(blk_exp, n_used, row_tok, u, w_gate_up, b_gate_up.reshape(depth, n_exp, 1, ff2),
      w_down, b_down.reshape(depth, n_exp, 1, d))


def _moe(u, route_idx, counts, layer, w_gate_up, b_gate_up, w_down, b_down):
    n_tok, d = u.shape
    tm = MOE_ROW_BLOCK
    n_assign = n_tok * TOP_K
    n_blocks = n_assign // tm + N_EXPERTS + 2
    n_rows = n_blocks * tm
    top_exp = route_idx[:, :TOP_K]
    rank = route_idx[:, TOP_K:2 * TOP_K]
    counts = counts[0, :N_EXPERTS].astype(jnp.int32)
    padded = (counts + tm - 1) // tm * tm
    start = jnp.cumsum(counts) - counts
    pad_end = jnp.cumsum(padded)
    pad_start = pad_end - padded
    blk_start = jnp.arange(n_blocks, dtype=jnp.int32) * tm
    blk_exp = jnp.minimum(jnp.sum(pad_end[None, :] <= blk_start[:, None], axis=1), N_EXPERTS - 1).astype(jnp.int32)
    n_used = (pad_end[-1:] // tm).astype(jnp.int32)
    order = jnp.argsort(top_exp.reshape(-1), stable=True).astype(jnp.int32)
    row_exp = jnp.repeat(blk_exp, tm)
    within = jnp.arange(n_rows, dtype=jnp.int32) - pad_start[row_exp]
    src = jnp.clip(start[row_exp] + within, 0, n_assign - 1)
    row_tok = jnp.where(within < counts[row_exp], order[src] // TOP_K, 0)
    y_rows = _expert_mlp(u, row_tok, blk_exp, n_used, layer, w_gate_up, b_gate_up, w_down, b_down)
    pos = pad_start[top_exp] + rank
    return y_rows[pos.T.reshape(-1)].reshape(TOP_K, n_tok, d)


def kernel(x, c, ctx, c_ctx, ada_w, ada_b, ln_g, ln_b, gla_w_in, gla_gate_w1, gla_gate_w2, gla_gate_b, gla_norm_g, gla_w_out, swa_w_qkv, swa_b_qkv, swa_sink, swa_w_out, swa_b_out, conv_w_in, conv_w, conv_w_out, moe_router_w, moe_router_b, moe_w_gate_up, moe_b_gate_up, moe_w_down, moe_b_down):
    batch, seq, d = x.shape
    n_ctx = ctx.shape[1]
    depth = ada_w.shape[0]
    alpha = (2 * depth) ** 0.25
    rows_full = _Rows(batch, n_ctx // ROW_TILE, seq // ROW_TILE)
    rows_lat = _Rows(batch, 0, seq // ROW_TILE)

    cond = jnp.zeros((8, d), F32).at[0].set(c_ctx).at[1:1 + batch].set(c)
    ada = _ada_mods(cond, ada_w, ada_b)
    mods_ctx = jnp.broadcast_to(ada[:, 0:1], (depth, batch, N_MOD * d))
    mods = jnp.stack([mods_ctx, ada[:, 1:1 + batch]], axis=2).reshape(depth, batch, 2, N_MOD, d)

    router_w = jnp.zeros((depth, d, LANES), F32).at[:, :, :N_EXPERTS].set(moe_router_w)
    router_b = jnp.zeros((depth, 1, LANES), F32).at[:, 0, :N_EXPERTS].set(moe_router_b)

    h = jnp.concatenate([ctx, x], axis=1).reshape(rows_full.n_rows, d)
    u = _modulate(h, mods[0], rows_full, 0, 1)
    rows = rows_full
    for i in range(depth):
        last = i == depth - 1
        kind, j = i % 3, i // 3
        if kind == 0:
            y = _gla_mixer(u, rows, gla_w_in[j], gla_gate_w1[j], gla_gate_w2[j], gla_gate_b[j], gla_norm_g[j], gla_w_out[j])
        elif kind == 1:
            y = _window_mixer(u, rows, n_ctx, seq, swa_w_qkv[j], swa_b_qkv[j], swa_sink[j], swa_w_out[j], swa_b_out[j])
        else:
            y = _conv_mixer(u, rows, n_ctx, conv_w_in[j], conv_w[j], conv_w_out[j])
        rows_out = rows_lat if last else rows
        h, u, route_idx, route_gate, counts = _resid_ln(
            h, y, mods[i], mods[i], ln_g[i, 0], ln_b[i, 0], rows, rows_out, alpha=alpha, gate_idx=2,
            next_mod=(3, 4), router=(router_w[i], router_b[i]))
        rows = rows_out
        y = _moe(u, route_idx, counts, i, moe_w_gate_up, moe_b_gate_up, moe_w_down, moe_b_down)
        if last:
            (h,) = _resid_ln(h, y, mods[i], mods[i], ln_g[i, 1], ln_b[i, 1], rows, rows, alpha=alpha, gate_idx=5,
                             combine_gates=route_gate)
        else:
            h, u = _resid_ln(h, y, mods[i], mods[i + 1], ln_g[i, 1], ln_b[i, 1], rows, rows, alpha=alpha, gate_idx=5,
                             next_mod=(0, 1), combine_gates=route_gate)
    return h.reshape(batch, seq, d)
```

```python
import functools

import jax
import jax.numpy as jnp
from jax import lax
from jax.experimental import pallas as pl
from jax.experimental.pallas import tpu as pltpu

F32 = jnp.float32
BF16 = jnp.bfloat16
HIGHEST = lax.Precision.HIGHEST

GRID_W = 64
GLA_HEADS = 4
GLA_GATE_RANK = 16
GLA_GATE_NORM = 16.0
GLA_CHUNK = 64
SWA_HEAD_DIM = 64
SWA_GROUP = 8
WINDOW = 128
ROPE_BASE = 10000.0
CONV_WIDTH = 3
N_EXPERTS = 32
TOP_K = 4
SWIGLU_LIMIT = 7.0
SWIGLU_ALPHA = 1.702
N_MOD = 6
LN_EPS = 1e-5
NEG_INF = -1e30

LANES = 128
ROW_TILE = 256
MOE_ROW_BLOCK = 256
VMEM_LIMIT = 48 * 1024 * 1024
MOE_VMEM_LIMIT = 60 * 1024 * 1024


def _cparams(*sem):
    return pltpu.CompilerParams(dimension_semantics=sem, vmem_limit_bytes=VMEM_LIMIT)


def _sigmoid(x):
    return 1.0 / (1.0 + jnp.exp(-x))


def _dot(a, b):
    return jnp.dot(a, b, preferred_element_type=F32)


def _dot_nt(a, b):
    return lax.dot_general(a, b, (((1,), (1,)), ((), ())), preferred_element_type=F32)


def _split_bf16(x):
    hi = x.astype(BF16)
    lo = (x - hi.astype(F32)).astype(BF16)
    return hi, lo


class _Rows:
    def __init__(self, batch, nb_ctx, nb_lat, tile=ROW_TILE):
        self.batch, self.nb_ctx, self.nb_lat, self.tile = batch, nb_ctx, nb_lat, tile
        self.nb_seq = nb_ctx + nb_lat
        self.n_blocks = batch * self.nb_seq
        self.n_rows = self.n_blocks * tile

    def batch_of(self, i):
        return i // self.nb_seq

    def seg_of(self, i):
        if self.nb_ctx == 0:
            return 1
        return jnp.where(i % self.nb_seq >= self.nb_ctx, 1, 0)

    def latent_block(self, i):
        return (i // self.nb_lat) * self.nb_seq + self.nb_ctx + i % self.nb_lat


def _ada_kernel(c_ref, w_ref, b_ref, o_ref):
    c = c_ref[...]
    s = c * _sigmoid(c)
    o_ref[...] = jnp.dot(s, w_ref[...], precision=HIGHEST, preferred_element_type=F32) + b_ref[...]


def _ada_mods(cond, ada_w, ada_b):
    depth, d, n = ada_w.shape
    tn = 1024 if n % 1024 == 0 else n
    return pl.pallas_call(
        _ada_kernel,
        out_shape=jax.ShapeDtypeStruct((depth, 8, n), F32),
        grid=(depth, n // tn),
        in_specs=[
            pl.BlockSpec((8, d), lambda l, j: (0, 0)),
            pl.BlockSpec((None, d, tn), lambda l, j: (l, 0, j)),
            pl.BlockSpec((None, 1, tn), lambda l, j: (l, 0, j)),
        ],
        out_specs=pl.BlockSpec((None, 8, tn), lambda l, j: (l, 0, j)),
        compiler_params=_cparams("parallel", "parallel"),
        name="ada_mods",
    )(cond, ada_w, ada_b.reshape(depth, 1, n))


def _modulate_kernel(h_ref, m_ref, u_ref, *, shift_idx, scale_idx):
    h = h_ref[...]
    u = h * (1.0 + m_ref[scale_idx:scale_idx + 1, :]) + m_ref[shift_idx:shift_idx + 1, :]
    u_ref[...] = u.astype(u_ref.dtype)


def _modulate(h, mods, rows, shift_idx, scale_idx):
    d = h.shape[1]
    t = rows.tile
    return pl.pallas_call(
        functools.partial(_modulate_kernel, shift_idx=shift_idx, scale_idx=scale_idx),
        out_shape=jax.ShapeDtypeStruct(h.shape, BF16),
        grid=(rows.n_blocks,),
        in_specs=[
            pl.BlockSpec((t, d), lambda i: (i, 0)),
            pl.BlockSpec((None, None, N_MOD, d), lambda i: (rows.batch_of(i), rows.seg_of(i), 0, 0)),
        ],
        out_specs=pl.BlockSpec((t, d), lambda i: (i, 0)),
        compiler_params=_cparams("parallel"),
        name="modulate",
    )(h, mods)


def _mm_kernel(x_ref, w_ref, o_ref):
    o_ref[...] = _dot(x_ref[...], w_ref[...]).astype(o_ref.dtype)


def _mm_bias_kernel(x_ref, w_ref, b_ref, o_ref):
    o_ref[...] = (_dot(x_ref[...], w_ref[...]) + b_ref[...]).astype(o_ref.dtype)


def _pick_tile(n, candidates):
    for c in candidates:
        if n % c == 0:
            return c
    return n


def _matmul(x, w, bias=None, out_dtype=BF16, name="proj"):
    m, k = x.shape
    n = w.shape[1]
    tm = _pick_tile(m, (1024, 512, 256))
    tn = _pick_tile(n, (1024, 768, 640, 512, 256, 128))
    in_specs = [
        pl.BlockSpec((tm, k), lambda i, j: (i, 0)),
        pl.BlockSpec((k, tn), lambda i, j: (0, j)),
    ]
    args = [x, w]
    body = _mm_kernel
    if bias is not None:
        in_specs.append(pl.BlockSpec((1, tn), lambda i, j: (0, j)))
        args.append(bias.reshape(1, n).astype(F32))
        body = _mm_bias_kernel
    return pl.pallas_call(
        body,
        out_shape=jax.ShapeDtypeStruct((m, n), out_dtype),
        grid=(m // tm, n // tn),
        in_specs=in_specs,
        out_specs=pl.BlockSpec((tm, tn), lambda i, j: (i, j)),
        compiler_params=_cparams("parallel", "parallel"),
        name=name,
    )(*args)


def _route_block(logits, count_ref):
    t = logits.shape[0]
    lane = lax.broadcasted_iota(jnp.int32, (t, LANES), 1)
    lane_f = lane.astype(F32)
    left = jnp.where(lane < N_EXPERTS, logits, NEG_INF)
    hot = jnp.zeros((t, LANES), F32)
    sels, vals = [], []
    for _ in range(TOP_K):
        m = jnp.max(left, axis=-1, keepdims=True)
        sel = jnp.min(jnp.where(left == m, lane_f, float(LANES)), axis=-1, keepdims=True)
        pick = lane_f == sel
        left = jnp.where(pick, NEG_INF, left)
        hot = jnp.where(pick, 1.0, hot)
        sels.append(sel)
        vals.append(m)
    row = lax.broadcasted_iota(jnp.int32, (t, t), 0)
    col = lax.broadcasted_iota(jnp.int32, (t, t), 1)
    earlier = jnp.where(row > col, 1.0, 0.0).astype(BF16)
    before = _dot(earlier, hot.astype(BF16)) + count_ref[...]
    count_ref[...] += jnp.sum(hot, axis=0, keepdims=True)
    exps = [jnp.exp(v - vals[0]) for v in vals]
    denom = exps[0]
    for e in exps[1:]:
        denom = denom + e
    idx = jnp.zeros((t, LANES), jnp.int32)
    gates = jnp.zeros((t, LANES), F32)
    for k in range(TOP_K):
        rank = jnp.sum(jnp.where(lane_f == sels[k], before, 0.0), axis=-1, keepdims=True)
        idx = jnp.where(lane == k, sels[k].astype(jnp.int32), idx)
        idx = jnp.where(lane == TOP_K + k, rank.astype(jnp.int32), idx)
        gates = jnp.where(lane == k, exps[k] / denom, gates)
    return idx, gates


def _resid_ln_kernel(*refs, alpha, gate_idx, shift_idx, scale_idx, with_u, with_router, combine):
    h_ref, y_ref = refs[:2]
    pos = 2
    if combine:
        yg_ref = refs[pos]
        pos += 1
    m_ref, mn_ref, g_ref, b_ref = refs[pos:pos + 4]
    pos += 4
    if with_router:
        wr_hi_ref, wr_lo_ref, br_ref = refs[pos:pos + 3]
        pos += 3
    hn_ref = refs[pos]
    pos += 1
    if combine:
        gates = yg_ref[...]
        y = gates[:, 0:1] * y_ref[0].astype(F32)
        for k in range(1, TOP_K):
            y = y + gates[:, k:k + 1] * y_ref[k].astype(F32)
    else:
        y = y_ref[...].astype(F32)
    z = alpha * h_ref[...] + y * m_ref[gate_idx:gate_idx + 1, :]
    zc = z - jnp.mean(z, axis=-1, keepdims=True)
    var = jnp.mean(zc * zc, axis=-1, keepdims=True)
    hn = zc * lax.rsqrt(var + LN_EPS) * g_ref[...] + b_ref[...]
    hn_ref[...] = hn
    if with_u:
        u_ref = refs[pos]
        pos += 1
        u = hn * (1.0 + mn_ref[scale_idx:scale_idx + 1, :]) + mn_ref[shift_idx:shift_idx + 1, :]
        u_ref[...] = u.astype(u_ref.dtype)
        if with_router:
            idx_ref, gate_ref, cnt_ref, count_ref = refs[pos:pos + 4]

            @pl.when(pl.program_id(0) == 0)
            def _():
                count_ref[...] = jnp.zeros_like(count_ref)

            u_hi, u_lo = _split_bf16(u)
            w_hi = wr_hi_ref[...]
            logits = _dot(u_hi, w_hi) + _dot(u_lo, w_hi) + _dot(u_hi, wr_lo_ref[...]) + br_ref[...]
            idx, gates = _route_block(logits, count_ref)
            idx_ref[...] = idx
            gate_ref[...] = gates
            cnt_ref[...] = count_ref[...]


def _resid_ln(h, y, mods, mods_next, ln_g, ln_b, rows_in, rows_out, *, alpha, gate_idx, next_mod=None, router=None,
              combine_gates=None):
    d = h.shape[1]
    t = rows_out.tile
    if rows_in is rows_out:
        in_row = lambda i: i
    else:
        in_row = rows_in.latent_block
    with_u = next_mod is not None
    with_router = router is not None
    combine = combine_gates is not None
    shift_idx, scale_idx = next_mod if with_u else (0, 0)
    mod_spec = pl.BlockSpec((None, None, N_MOD, d), lambda i: (rows_out.batch_of(i), rows_out.seg_of(i), 0, 0))
    in_specs = [pl.BlockSpec((t, d), lambda i: (in_row(i), 0))]
    args = [h, y]
    if combine:
        in_specs += [pl.BlockSpec((TOP_K, t, d), lambda i: (0, in_row(i), 0)),
                     pl.BlockSpec((t, LANES), lambda i: (in_row(i), 0))]
        args.append(combine_gates)
    else:
        in_specs.append(pl.BlockSpec((t, d), lambda i: (in_row(i), 0)))
    in_specs += [mod_spec, mod_spec, pl.BlockSpec((1, d), lambda i: (0, 0)), pl.BlockSpec((1, d), lambda i: (0, 0))]
    args += [mods, mods_next, ln_g.reshape(1, d), ln_b.reshape(1, d)]
    out_shape = [jax.ShapeDtypeStruct((rows_out.n_rows, d), F32)]
    out_specs = [pl.BlockSpec((t, d), lambda i: (i, 0))]
    scratch = []
    if with_router:
        wr, br = router
        w_spec = pl.BlockSpec((d, LANES), lambda i: (0, 0))
        in_specs += [w_spec, w_spec, pl.BlockSpec((1, LANES), lambda i: (0, 0))]
        args += [*_split_bf16(wr), br]
    if with_u:
        out_shape.append(jax.ShapeDtypeStruct((rows_out.n_rows, d), F32 if with_router else BF16))
        out_specs.append(pl.BlockSpec((t, d), lambda i: (i, 0)))
    if with_router:
        out_shape += [jax.ShapeDtypeStruct((rows_out.n_rows, LANES), jnp.int32),
                      jax.ShapeDtypeStruct((rows_out.n_rows, LANES), F32),
                      jax.ShapeDtypeStruct((1, LANES), F32)]
        out_specs += [pl.BlockSpec((t, LANES), lambda i: (i, 0)), pl.BlockSpec((t, LANES), lambda i: (i, 0)),
                      pl.BlockSpec((1, LANES), lambda i: (0, 0))]
        scratch.append(pltpu.VMEM((1, LANES), F32))
    return pl.pallas_call(
        functools.partial(_resid_ln_kernel, alpha=alpha, gate_idx=gate_idx, shift_idx=shift_idx,
                          scale_idx=scale_idx, with_u=with_u, with_router=with_router, combine=combine),
        out_shape=out_shape,
        grid=(rows_out.n_blocks,),
        in_specs=in_specs,
        out_specs=out_specs,
        scratch_shapes=scratch,
        compiler_params=_cparams("arbitrary" if with_router else "parallel"),
        name="resid_ln",
    )(*args)


def _gla_kernel(*refs, reverse, fuse_post, q_scale, dk, dv):
    if fuse_post:
        q_ref, k_ref, v_ref, r_ref, w2_ref, gb_ref, other_ref, g_ref, ng_ref, out_ref, state_ref = refs
    else:
        q_ref, k_ref, v_ref, r_ref, w2_ref, gb_ref, out_ref, state_ref = refs

    @pl.when(pl.program_id(1) == 0)
    def _():
        state_ref[...] = jnp.zeros_like(state_ref)

    t = q_ref.shape[0]
    c = GLA_CHUNK
    shift = c.bit_length() - 1
    r_hi, r_lo = _split_bf16(r_ref[...])
    w_hi, w_lo = _split_bf16(w2_ref[...])
    z = _dot(r_hi, w_hi) + _dot(r_lo, w_hi) + _dot(r_hi, w_lo) + gb_ref[...]
    log_a = (jnp.minimum(z, 0.0) - jnp.log(1.0 + jnp.exp(-jnp.abs(z)))) * (1.0 / GLA_GATE_NORM)
    row = lax.broadcasted_iota(jnp.int32, (t, t), 0)
    col = lax.broadcasted_iota(jnp.int32, (t, t), 1)
    same_chunk = lax.shift_right_logical(row, shift) == lax.shift_right_logical(col, shift)
    ahead = (col - row) if reverse else (row - col)
    tri = jnp.where(jnp.where(same_chunk, ahead, -1) >= 0, 1.0, 0.0).astype(BF16)
    a_hi, a_lo = _split_bf16(log_a)
    cum = _dot(tri, a_hi) + _dot(tri, a_lo)
    grow = jnp.exp(cum)
    shrink = jnp.exp(-cum)
    crow = lax.broadcasted_iota(jnp.int32, (c, c), 0)
    ccol = lax.broadcasted_iota(jnp.int32, (c, c), 1)
    causal = (ccol >= crow) if reverse else (crow >= ccol)
    n_chunks = t // c
    for h in range(GLA_HEADS):
        ks = slice(h * dk, (h + 1) * dk)
        vs = slice(h * dv, (h + 1) * dv)
        q_dec = (q_ref[:, ks].astype(F32) * q_scale * grow[:, ks]).astype(BF16)
        k_in = k_ref[:, ks].astype(F32) * shrink[:, ks]
        for step in range(n_chunks):
            j = n_chunks - 1 - step if reverse else step
            rs = slice(j * c, (j + 1) * c)
            edge = j * c if reverse else (j + 1) * c - 1
            decay = jnp.exp(cum[edge:edge + 1, ks])
            qd = q_dec[rs]
            k_intra = k_in[rs].astype(BF16)
            k_state = (k_in[rs] * decay).astype(BF16)
            v = v_ref[rs, vs]
            scores = jnp.where(causal, _dot_nt(qd, k_intra), 0.0)
            state_t = state_ref[h]
            o = _dot(scores.astype(BF16), v) + _dot_nt(qd, state_t.astype(BF16))
            kv_t = lax.dot_general(v, k_state, (((0,), (0,)), ((), ())), preferred_element_type=F32)
            state_ref[h] = state_t * decay + kv_t
            if fuse_post:
                o = o + other_ref[rs, vs]
                o = o * lax.rsqrt(jnp.mean(o * o, axis=-1, keepdims=True) + LN_EPS) * ng_ref[...]
                g = g_ref[rs, vs].astype(F32)
                out_ref[rs, vs] = (o * (g * _sigmoid(g))).astype(out_ref.dtype)
            else:
                out_ref[rs, vs] = o


def _gla_scan(qkvg, r, w2, gate_b, rows, direction, other=None, norm_g=None):
    t_rows = qkvg.shape[0]
    dk_tot = w2.shape[2]
    dk = dk_tot // GLA_HEADS
    dv_tot = (qkvg.shape[1] - 2 * dk_tot) // 2
    dv = dv_tot // GLA_HEADS
    t = rows.tile
    reverse = direction == 1
    fuse_post = other is not None

    def block(b, n):
        if reverse:
            n = jnp.where(n < rows.nb_ctx, rows.nb_ctx - 1 - n, rows.nb_seq + rows.nb_ctx - 1 - n)
        return b * rows.nb_seq + n

    in_specs = [
        pl.BlockSpec((t, dk_tot), lambda b, n: (block(b, n), 0)),
        pl.BlockSpec((t, dk_tot), lambda b, n: (block(b, n), 1)),
        pl.BlockSpec((t, dv_tot), lambda b, n: (block(b, n), (2 * dk_tot) // dv_tot)),
        pl.BlockSpec((t, LANES), lambda b, n: (block(b, n), direction)),
        pl.BlockSpec((None, LANES, dk_tot), lambda b, n: (direction, 0, 0)),
        pl.BlockSpec((None, 1, dk_tot), lambda b, n: (direction, 0, 0)),
    ]
    args = [qkvg, qkvg, qkvg, r, w2, gate_b]
    if fuse_post:
        in_specs += [
            pl.BlockSpec((t, dv_tot), lambda b, n: (block(b, n), 0)),
            pl.BlockSpec((t, dv_tot), lambda b, n: (block(b, n), (2 * dk_tot + dv_tot) // dv_tot)),
            pl.BlockSpec((1, dv), lambda b, n: (0, 0)),
        ]
        args += [other, qkvg, norm_g.reshape(1, dv).astype(F32)]
    return pl.pallas_call(
        functools.partial(_gla_kernel, reverse=reverse, fuse_post=fuse_post, q_scale=float(dk) ** -0.5, dk=dk, dv=dv),
        out_shape=jax.ShapeDtypeStruct((t_rows, dv_tot), BF16 if fuse_post else F32),
        grid=(rows.batch, rows.nb_seq),
        in_specs=in_specs,
        out_specs=pl.BlockSpec((t, dv_tot), lambda b, n: (block(b, n), 0)),
        scratch_shapes=[pltpu.VMEM((GLA_HEADS, dv, dk), F32)],
        compiler_params=_cparams("parallel", "arbitrary"),
        name="gla_scan_bwd" if reverse else "gla_scan_fwd",
    )(*args)


def _gla_mixer(u, rows, w_in, gate_w1, gate_w2, gate_b, norm_g, w_out):
    d = u.shape[1]
    dk_tot = gate_w2.shape[2]
    qkvg = _matmul(u, w_in.astype(BF16), name="gla_in")
    w1 = jnp.zeros((d, 2 * LANES), F32)
    w2 = jnp.zeros((2, LANES, dk_tot), F32)
    for dd in range(2):
        w1 = w1.at[:, dd * LANES:dd * LANES + GLA_GATE_RANK].set(gate_w1[dd])
        w2 = w2.at[dd, :GLA_GATE_RANK].set(gate_w2[dd])
    r = _matmul(u, w1.astype(BF16), out_dtype=F32, name="gla_gate_in")
    gate_b = gate_b.reshape(2, 1, dk_tot).astype(F32)
    o_fwd = _gla_scan(qkvg, r, w2, gate_b, rows, 0)
    y = _gla_scan(qkvg, r, w2, gate_b, rows, 1, other=o_fwd, norm_g=norm_g)
    return _matmul(y, w_out.astype(BF16), name="gla_out")


def _rope_kernel(x_ref, cos_ref, sin_ref, q_ref, k2_ref, v2_ref, *, q_width, kv_width):
    cos = cos_ref[...]
    sin = sin_ref[...]
    lane = lax.broadcasted_iota(jnp.int32, cos.shape, 1)
    first = (lane % 32) < 16
    low = lane < SWA_HEAD_DIM

    def rope(slab):
        partner = jnp.where(first, pltpu.roll(slab, LANES - 16, 1), pltpu.roll(slab, 16, 1))
        return slab * cos + partner * sin

    scale = SWA_HEAD_DIM ** -0.5
    for s in range(q_width // LANES):
        slab = x_ref[:, s * LANES:(s + 1) * LANES].astype(F32)
        q_ref[:, s * LANES:(s + 1) * LANES] = (rope(slab) * scale).astype(q_ref.dtype)
    for s in range(kv_width // LANES):
        k_slab = rope(x_ref[:, q_width + s * LANES:q_width + (s + 1) * LANES].astype(F32))
        v_slab = x_ref[:, q_width + kv_width + s * LANES:q_width + kv_width + (s + 1) * LANES].astype(F32)
        for slab, out_ref in ((k_slab, k2_ref), (v_slab, v2_ref)):
            swapped = pltpu.roll(slab, SWA_HEAD_DIM, 1)
            out_ref[2 * s] = jnp.where(low, slab, swapped).astype(out_ref.dtype)
            out_ref[2 * s + 1] = jnp.where(low, swapped, slab).astype(out_ref.dtype)


def _rope_tables(n_ctx, seq):
    t = jnp.arange(seq, dtype=jnp.int32)
    row = (t // GRID_W).astype(F32)
    col = (t % GRID_W).astype(F32)
    n_freq = SWA_HEAD_DIM // 4
    inv_freq = ROPE_BASE ** (-jnp.arange(n_freq, dtype=F32) / n_freq)
    lane = jnp.arange(LANES)
    within = lane % SWA_HEAD_DIM
    pos = jnp.where((within < SWA_HEAD_DIM // 2)[None, :], row[:, None], col[:, None])
    ang = pos * inv_freq[lane % n_freq][None, :]
    sign = jnp.where((lane % 32) < 16, -1.0, 1.0)[None, :]
    cos = jnp.concatenate([jnp.ones((n_ctx, LANES), F32), jnp.cos(ang)], axis=0)
    sin = jnp.concatenate([jnp.zeros((n_ctx, LANES), F32), jnp.sin(ang) * sign], axis=0)
    return cos, sin


def _rope_split(qkv, cos, sin, rows, q_width, kv_width):
    t_rows = qkv.shape[0]
    t = rows.tile
    n_kv = kv_width // SWA_HEAD_DIM
    seq_rows = rows.nb_seq * t
    kv_shape = jax.ShapeDtypeStruct((rows.batch, n_kv, seq_rows, LANES), BF16)
    kv_spec = pl.BlockSpec((None, n_kv, t, LANES), lambda i: (i // rows.nb_seq, 0, i % rows.nb_seq, 0))
    return pl.pallas_call(
        functools.partial(_rope_kernel, q_width=q_width, kv_width=kv_width),
        out_shape=[jax.ShapeDtypeStruct((t_rows, q_width), BF16), kv_shape, kv_shape],
        grid=(rows.n_blocks,),
        in_specs=[
            pl.BlockSpec((t, qkv.shape[1]), lambda i: (i, 0)),
            pl.BlockSpec((t, LANES), lambda i: (i % rows.nb_seq, 0)),
            pl.BlockSpec((t, LANES), lambda i: (i % rows.nb_seq, 0)),
        ],
        out_specs=[pl.BlockSpec((t, q_width), lambda i: (i, 0)), kv_spec, kv_spec],
        compiler_params=_cparams("parallel"),
        name="rope_split",
    )(qkv, cos, sin)


def _attend_heads(q_ref, kk, vv, mask, sink_ref, head0, o_ref):
    tq = q_ref.shape[0]
    lane = lax.broadcasted_iota(jnp.int32, (tq, LANES), 1)
    low = lane < SWA_HEAD_DIM
    q_all = q_ref[...]
    sinks = [sink_ref[head0 + h] for h in range(SWA_GROUP)]
    scores = []
    for h in range(SWA_GROUP):
        q2 = q_all[:, (h // 2) * LANES:(h // 2 + 1) * LANES]
        qh = jnp.where(low if h % 2 == 0 else lane >= SWA_HEAD_DIM, q2, jnp.zeros_like(q2))
        s = _dot_nt(qh, kk)
        scores.append(s if mask is None else jnp.where(mask, s, NEG_INF))
    probs, denoms = [], []
    for h in range(SWA_GROUP):
        m = jnp.maximum(jnp.max(scores[h], axis=-1, keepdims=True), sinks[h])
        p = jnp.exp(scores[h] - m)
        denoms.append(jnp.sum(p, axis=-1, keepdims=True) + jnp.exp(sinks[h] - m))
        probs.append(p.astype(BF16))
    outs = [_dot(probs[h], vv) / denoms[h] for h in range(SWA_GROUP)]
    o_ref[...] = jnp.concatenate(
        [jnp.where(low, outs[2 * pair], outs[2 * pair + 1]).astype(o_ref.dtype) for pair in range(SWA_GROUP // 2)],
        axis=1)


def _attn_kernel(sink_ref, q_ref, kc_ref, vc_ref, kp_ref, vp_ref, km_ref, vm_ref, kn_ref, vn_ref,
                 o_ref, kk_ref, vv_ref, *, n_ctx, nq_ctx, nq_lat):
    n = pl.program_id(2)
    head0 = pl.program_id(1) * SWA_GROUP
    w = WINDOW

    @pl.when(n < nq_ctx)
    def _():
        _attend_heads(q_ref, kc_ref[...], vc_ref[...], None, sink_ref, head0, o_ref)

    @pl.when(n >= nq_ctx)
    def _():
        m = n - nq_ctx
        for ref, parts in ((kk_ref, (kc_ref, kp_ref, km_ref, kn_ref)), (vv_ref, (vc_ref, vp_ref, vm_ref, vn_ref))):
            ref[0:n_ctx] = parts[0][...]
            ref[n_ctx:n_ctx + w] = parts[1][...]
            ref[n_ctx + w:n_ctx + 2 * w] = parts[2][...]
            ref[n_ctx + 2 * w:n_ctx + 3 * w] = parts[3][...]
        nk = n_ctx + 3 * w
        qpos = lax.broadcasted_iota(jnp.int32, (w, nk), 0)
        col = lax.broadcasted_iota(jnp.int32, (w, nk), 1)
        rel = jnp.where(col < n_ctx, 0, col - (n_ctx + w) - qpos)
        far = 4 * w
        no_prev = jnp.where(m > 0, 0, far)
        no_next = jnp.where(m < nq_lat - 1, 0, far)
        missing = jnp.where(col < n_ctx, 0,
                            jnp.where(col < n_ctx + w, no_prev, jnp.where(col >= n_ctx + 2 * w, no_next, 0)))
        mask = jnp.abs(rel) + missing <= w
        _attend_heads(q_ref, kk_ref[...], vv_ref[...], mask, sink_ref, head0, o_ref)


def _window_attention(q, k2, v2, sink, rows, n_ctx):
    t_rows, q_width = q.shape
    batch, n_kv, seq_rows, _ = k2.shape
    w = WINDOW
    gw = SWA_GROUP * SWA_HEAD_DIM
    nq_seq = seq_rows // w
    nq_ctx = n_ctx // w
    nq_lat = nq_seq - nq_ctx
    ctx_kv = pl.BlockSpec((None, None, n_ctx, LANES), lambda b, g, n: (b, g, 0, 0))

    def band(shift):
        def index(b, g, n):
            return (b, g, nq_ctx + jnp.clip(n - nq_ctx + shift, 0, nq_lat - 1), 0)
        return pl.BlockSpec((None, None, w, LANES), index)

    nk = n_ctx + 3 * w
    return pl.pallas_call(
        functools.partial(_attn_kernel, n_ctx=n_ctx, nq_ctx=nq_ctx, nq_lat=nq_lat),
        out_shape=jax.ShapeDtypeStruct((t_rows, q_width), BF16),
        grid=(batch, n_kv, nq_seq),
        in_specs=[pl.BlockSpec(memory_space=pltpu.SMEM),
                  pl.BlockSpec((w, gw), lambda b, g, n: (b * nq_seq + n, g)),
                  ctx_kv, ctx_kv, band(-1), band(-1), band(0), band(0), band(1), band(1)],
        out_specs=pl.BlockSpec((w, gw), lambda b, g, n: (b * nq_seq + n, g)),
        scratch_shapes=[pltpu.VMEM((nk, LANES), BF16), pltpu.VMEM((nk, LANES), BF16)],
        compiler_params=_cparams("parallel", "parallel", "parallel"),
        name="window_attn",
    )(sink, q, k2, v2, k2, v2, k2, v2, k2, v2)


def _window_mixer(u, rows, n_ctx, seq, w_qkv, b_qkv, sink, w_out, b_out):
    q_width = w_out.shape[0]
    kv_width = (w_qkv.shape[1] - q_width) // 2
    qkv = _matmul(u, w_qkv.astype(BF16), bias=b_qkv, name="swa_qkv")
    cos, sin = _rope_tables(n_ctx, seq)
    q, k2, v2 = _rope_split(qkv, cos, sin, rows, q_width, kv_width)
    o = _window_attention(q, k2, v2, sink.astype(F32), rows, n_ctx)
    return _matmul(o, w_out.astype(BF16), bias=b_out, name="swa_out")


def _conv_kernel(gi_ref, go_ref, val_ref, w_ref, o_ref, p_ref, *, n_ctx):
    length = gi_ref.shape[0]
    pad = 8
    p_ref[0:pad] = jnp.zeros((pad, p_ref.shape[1]), F32)
    p_ref[pad + length:pad + length + pad] = jnp.zeros((pad, p_ref.shape[1]), F32)
    p_ref[pad:pad + length] = gi_ref[...].astype(F32) * val_ref[...].astype(F32)
    t = lax.broadcasted_iota(jnp.int32, (length, 1), 0)
    seg_start = t * (t - n_ctx) == 0
    seg_end = (t - (n_ctx - 1)) * (t - (length - 1)) == 0
    prev = jnp.where(seg_start, 0.0, p_ref[pad - 1:pad - 1 + length])
    nxt = jnp.where(seg_end, 0.0, p_ref[pad + 1:pad + 1 + length])
    z = prev * w_ref[0:1, :] + p_ref[pad:pad + length] * w_ref[1:2, :] + nxt * w_ref[2:3, :]
    o_ref[...] = (go_ref[...].astype(F32) * z).astype(o_ref.dtype)


def _conv_gate(proj, conv_w, rows, n_ctx):
    t_rows = proj.shape[0]
    d = proj.shape[1] // 3
    length = rows.nb_seq * rows.tile
    tc = LANES
    nc = d // tc
    return pl.pallas_call(
        functools.partial(_conv_kernel, n_ctx=n_ctx),
        out_shape=jax.ShapeDtypeStruct((t_rows, d), BF16),
        grid=(rows.batch, nc),
        in_specs=[
            pl.BlockSpec((length, tc), lambda b, c: (b, c)),
            pl.BlockSpec((length, tc), lambda b, c: (b, nc + c)),
            pl.BlockSpec((length, tc), lambda b, c: (b, 2 * nc + c)),
            pl.BlockSpec((CONV_WIDTH, tc), lambda b, c: (0, c)),
        ],
        out_specs=pl.BlockSpec((length, tc), lambda b, c: (b, c)),
        scratch_shapes=[pltpu.VMEM((length + 16, tc), F32)],
        compiler_params=_cparams("parallel", "parallel"),
        name="conv_gate",
    )(proj, proj, proj, conv_w.astype(F32))


def _conv_mixer(u, rows, n_ctx, w_in, w_conv, w_out):
    proj = _matmul(u, w_in.astype(BF16), name="conv_in")
    zg = _conv_gate(proj, w_conv, rows, n_ctx)
    return _matmul(zg, w_out.astype(BF16), name="conv_out")


def _row_gather_copy(u_hbm, x_buf, sem, slot, tok, r):
    return pltpu.make_async_copy(u_hbm.at[pl.ds(tok, 1)], x_buf.at[slot, pl.ds(r, 1)], sem.at[slot])


def _moe_kernel(blk_exp_ref, n_used_ref, row_tok_ref, u_hbm, wgu_ref, bgu_ref, wdn_ref, bdn_ref, o_ref,
                x_buf, sem, wgu_bf_ref, wdn_bf_ref):
    i = pl.program_id(0)
    tm = o_ref.shape[0]
    n_used = n_used_ref[0]
    used = i < n_used
    slot = lax.rem(i, 3)

    def wait_gather():
        pltpu.make_async_copy(u_hbm.at[pl.ds(0, tm)], x_buf.at[slot], sem.at[slot]).wait()

    @pl.when(used & (i == 0))
    def _():
        def body(r, carry):
            _row_gather_copy(u_hbm, x_buf, sem, 0, row_tok_ref[r], r).start()
            _row_gather_copy(u_hbm, x_buf, sem, 1, row_tok_ref[tm + r], r).start(priority=1)
            return carry
        lax.fori_loop(0, tm, body, 0, unroll=4)

    new_expert = (i == 0) | (blk_exp_ref[i] != blk_exp_ref[jnp.maximum(i - 1, 0)])

    @pl.when(used & new_expert)
    def _():
        wgu_bf_ref[...] = wgu_ref[...].astype(BF16)
        wdn_bf_ref[...] = wdn_ref[...].astype(BF16)

    @pl.when(used)
    def _():
        wait_gather()
        x = x_buf[slot].astype(BF16)
        base = (i + 2) * tm
        ahead = lax.rem(i + 2, 3)
        for r in range(tm):
            _row_gather_copy(u_hbm, x_buf, sem, ahead, row_tok_ref[base + r], r).start(priority=r % 2)
        ff = wdn_ref.shape[0]
        gu = _dot(x, wgu_bf_ref[...]) + bgu_ref[...]
        glu = jnp.minimum(gu[:, :ff], SWIGLU_LIMIT)
        lin = jnp.clip(gu[:, ff:], -SWIGLU_LIMIT, SWIGLU_LIMIT)
        act = glu * _sigmoid(SWIGLU_ALPHA * glu) * (lin + 1.0)
        o_ref[...] = (_dot(act.astype(BF16), wdn_bf_ref[...]) + bdn_ref[...]).astype(o_ref.dtype)

    @pl.when((i >= n_used) & (i <= n_used + 1) & (n_used > 0))
    def _():
        wait_gather()

    @pl.when(i >= n_used)
    def _():
        o_ref[...] = jnp.zeros_like(o_ref)


def _expert_mlp(u, row_tok, blk_exp, n_used, layer, w_gate_up, b_gate_up, w_down, b_down):
    d = u.shape[1]
    n_rows = row_tok.shape[0]
    depth, n_exp, _, ff2 = w_gate_up.shape
    ff = ff2 // 2
    tm = MOE_ROW_BLOCK
    grid_spec = pltpu.PrefetchScalarGridSpec(
        num_scalar_prefetch=3,
        grid=(n_rows // tm,),
        in_specs=[
            pl.BlockSpec(memory_space=pl.ANY),
            pl.BlockSpec((None, None, d, ff2), lambda i, be, nu, rt: (layer, be[i], 0, 0)),
            pl.BlockSpec((None, None, 1, ff2), lambda i, be, nu, rt: (layer, be[i], 0, 0)),
            pl.BlockSpec((None, None, ff, d), lambda i, be, nu, rt: (layer, be[i], 0, 0)),
            pl.BlockSpec((None, None, 1, d), lambda i, be, nu, rt: (layer, be[i], 0, 0)),
        ],
        out_specs=pl.BlockSpec((tm, d), lambda i, be, nu, rt: (i, 0)),
        scratch_shapes=[pltpu.VMEM((3, tm, d), F32), pltpu.SemaphoreType.DMA((3,)),
                        pltpu.VMEM((d, ff2), BF16), pltpu.VMEM((ff, d), BF16)],
    )
    return pl.pallas_call(
        _moe_kernel,
        out_shape=jax.ShapeDtypeStruct((n_rows, d), BF16),
        grid_spec=grid_spec,
        compiler_params=pltpu.CompilerParams(dimension_semantics=("arbitrary",), vmem_limit_bytes=MOE_VMEM_LIMIT),
        name="expert_mlp",
    )(blk_exp, n_used, row_tok, u, w_gate_up, b_gate_up.reshape(depth, n_exp, 1, ff2),
      w_down, b_down.reshape(depth, n_exp, 1, d))


def _moe(u, route_idx, counts, layer, w_gate_up, b_gate_up, w_down, b_down):
    n_tok, d = u.shape
    tm = MOE_ROW_BLOCK
    n_assign = n_tok * TOP_K
    n_blocks = n_assign // tm + N_EXPERTS + 2
    n_rows = n_blocks * tm
    top_exp = route_idx[:, :TOP_K]
    rank = route_idx[:, TOP_K:2 * TOP_K]
    counts = counts[0, :N_EXPERTS].astype(jnp.int32)
    padded = (counts + tm - 1) // tm * tm
    start = jnp.cumsum(counts) - counts
    pad_end = jnp.cumsum(padded)
    pad_start = pad_end - padded
    blk_start = jnp.arange(n_blocks, dtype=jnp.int32) * tm
    blk_exp = jnp.minimum(jnp.sum(pad_end[None, :] <= blk_start[:, None], axis=1), N_EXPERTS - 1).astype(jnp.int32)
    n_used = (pad_end[-1:] // tm).astype(jnp.int32)
    order = jnp.argsort(top_exp.reshape(-1), stable=True).astype(jnp.int32)
    row_exp = jnp.repeat(blk_exp, tm)
    within = jnp.arange(n_rows, dtype=jnp.int32) - pad_start[row_exp]
    src = jnp.clip(start[row_exp] + within, 0, n_assign - 1)
    row_tok = jnp.where(within < counts[row_exp], order[src] // TOP_K, 0)
    y_rows = _expert_mlp(u, row_tok, blk_exp, n_used, layer, w_gate_up, b_gate_up, w_down, b_down)
    pos = pad_start[top_exp] + rank
    return y_rows[pos.T.reshape(-1)].reshape(TOP_K, n_tok, d)


def kernel(x, c, ctx, c_ctx, ada_w, ada_b, ln_g, ln_b, gla_w_in, gla_gate_w1, gla_gate_w2, gla_gate_b, gla_norm_g, gla_w_out, swa_w_qkv, swa_b_qkv, swa_sink, swa_w_out, swa_b_out, conv_w_in, conv_w, conv_w_out, moe_router_w, moe_router_b, moe_w_gate_up, moe_b_gate_up, moe_w_down, moe_b_down):
    batch, seq, d = x.shape
    n_ctx = ctx.shape[1]
    depth = ada_w.shape[0]
    alpha = (2 * depth) ** 0.25
    rows_full = _Rows(batch, n_ctx // ROW_TILE, seq // ROW_TILE)
    rows_lat = _Rows(batch, 0, seq // ROW_TILE)

    cond = jnp.zeros((8, d), F32).at[0].set(c_ctx).at[1:1 + batch].set(c)
    ada = _ada_mods(cond, ada_w, ada_b)
    mods_ctx = jnp.broadcast_to(ada[:, 0:1], (depth, batch, N_MOD * d))
    mods = jnp.stack([mods_ctx, ada[:, 1:1 + batch]], axis=2).reshape(depth, batch, 2, N_MOD, d)

    router_w = jnp.zeros((depth, d, LANES), F32).at[:, :, :N_EXPERTS].set(moe_router_w)
    router_b = jnp.zeros((depth, 1, LANES), F32).at[:, 0, :N_EXPERTS].set(moe_router_b)

    h = jnp.concatenate([ctx, x], axis=1).reshape(rows_full.n_rows, d)
    u = _modulate(h, mods[0], rows_full, 0, 1)
    rows = rows_full
    for i in range(depth):
        last = i == depth - 1
        kind, j = i % 3, i // 3
        if kind == 0:
            y = _gla_mixer(u, rows, gla_w_in[j], gla_gate_w1[j], gla_gate_w2[j], gla_gate_b[j], gla_norm_g[j], gla_w_out[j])
        elif kind == 1:
            y = _window_mixer(u, rows, n_ctx, seq, swa_w_qkv[j], swa_b_qkv[j], swa_sink[j], swa_w_out[j], swa_b_out[j])
        else:
            y = _conv_mixer(u, rows, n_ctx, conv_w_in[j], conv_w[j], conv_w_out[j])
        rows_out = rows_lat if last else rows
        h, u, route_idx, route_gate, counts = _resid_ln(
            h, y, mods[i], mods[i], ln_g[i, 0], ln_b[i, 0], rows, rows_out, alpha=alpha, gate_idx=2,
            next_mod=(3, 4), router=(router_w[i], router_b[i]))
        rows = rows_out
        y = _moe(u, route_idx, counts, i, moe_w_gate_up, moe_b_gate_up, moe_w_down, moe_b_down)
        if last:
            (h,) = _resid_ln(h, y, mods[i], mods[i], ln_g[i, 1], ln_b[i, 1], rows, rows, alpha=alpha, gate_idx=5,
                             combine_gates=route_gate)
        else:
            h, u = _resid_ln(h, y, mods[i], mods[i + 1], ln_g[i, 1], ln_b[i, 1], rows, rows, alpha=alpha, gate_idx=5,
                             next_mod=(0, 1), combine_gates=route_gate)
    return h.reshape(batch, seq, d)
```

```python
import functools

import jax
import jax.numpy as jnp
from jax import lax
from jax.experimental import pallas as pl
from jax.experimental.pallas import tpu as pltpu

F32 = jnp.float32
BF16 = jnp.bfloat16
HIGHEST = lax.Precision.HIGHEST

GRID_W = 64
GLA_HEADS = 4
GLA_GATE_RANK = 16
GLA_GATE_NORM = 16.0
GLA_CHUNK = 64
SWA_HEAD_DIM = 64
SWA_GROUP = 8
WINDOW = 128
ROPE_BASE = 10000.0
CONV_WIDTH = 3
N_EXPERTS = 32
TOP_K = 4
SWIGLU_LIMIT = 7.0
SWIGLU_ALPHA = 1.702
N_MOD = 6
LN_EPS = 1e-5
NEG_INF = -1e30

LANES = 128
ROW_TILE = 256
MOE_ROW_BLOCK = 256
VMEM_LIMIT = 48 * 1024 * 1024
MOE_VMEM_LIMIT = 60 * 1024 * 1024


def _cparams(*sem):
    return pltpu.CompilerParams(dimension_semantics=sem, vmem_limit_bytes=VMEM_LIMIT)


def _sigmoid(x):
    return 1.0 / (1.0 + jnp.exp(-x))


def _dot(a, b):
    return jnp.dot(a, b, preferred_element_type=F32)


def _dot_nt(a, b):
    return lax.dot_general(a, b, (((1,), (1,)), ((), ())), preferred_element_type=F32)


def _split_bf16(x):
    hi = x.astype(BF16)
    lo = (x - hi.astype(F32)).astype(BF16)
    return hi, lo


class _Rows:
    def __init__(self, batch, nb_ctx, nb_lat, tile=ROW_TILE):
        self.batch, self.nb_ctx, self.nb_lat, self.tile = batch, nb_ctx, nb_lat, tile
        self.nb_seq = nb_ctx + nb_lat
        self.n_blocks = batch * self.nb_seq
        self.n_rows = self.n_blocks * tile

    def batch_of(self, i):
        return i // self.nb_seq

    def seg_of(self, i):
        if self.nb_ctx == 0:
            return 1
        return jnp.where(i % self.nb_seq >= self.nb_ctx, 1, 0)

    def latent_block(self, i):
        return (i // self.nb_lat) * self.nb_seq + self.nb_ctx + i % self.nb_lat


def _ada_kernel(c_ref, w_ref, b_ref, o_ref):
    c = c_ref[...]
    s = c * _sigmoid(c)
    o_ref[...] = jnp.dot(s, w_ref[...], precision=HIGHEST, preferred_element_type=F32) + b_ref[...]


def _ada_mods(cond, ada_w, ada_b):
    depth, d, n = ada_w.shape
    tn = 1024 if n % 1024 == 0 else n
    return pl.pallas_call(
        _ada_kernel,
        out_shape=jax.ShapeDtypeStruct((depth, 8, n), F32),
        grid=(depth, n // tn),
        in_specs=[
            pl.BlockSpec((8, d), lambda l, j: (0, 0)),
            pl.BlockSpec((None, d, tn), lambda l, j: (l, 0, j)),
            pl.BlockSpec((None, 1, tn), lambda l, j: (l, 0, j)),
        ],
        out_specs=pl.BlockSpec((None, 8, tn), lambda l, j: (l, 0, j)),
        compiler_params=_cparams("parallel", "parallel"),
        name="ada_mods",
    )(cond, ada_w, ada_b.reshape(depth, 1, n))


def _modulate_kernel(h_ref, m_ref, u_ref, *, shift_idx, scale_idx):
    h = h_ref[...]
    u = h * (1.0 + m_ref[scale_idx:scale_idx + 1, :]) + m_ref[shift_idx:shift_idx + 1, :]
    u_ref[...] = u.astype(u_ref.dtype)


def _modulate(h, mods, rows, shift_idx, scale_idx):
    d = h.shape[1]
    t = rows.tile
    return pl.pallas_call(
        functools.partial(_modulate_kernel, shift_idx=shift_idx, scale_idx=scale_idx),
        out_shape=jax.ShapeDtypeStruct(h.shape, BF16),
        grid=(rows.n_blocks,),
        in_specs=[
            pl.BlockSpec((t, d), lambda i: (i, 0)),
            pl.BlockSpec((None, None, N_MOD, d), lambda i: (rows.batch_of(i), rows.seg_of(i), 0, 0)),
        ],
        out_specs=pl.BlockSpec((t, d), lambda i: (i, 0)),
        compiler_params=_cparams("parallel"),
        name="modulate",
    )(h, mods)


def _mm_kernel(x_ref, w_ref, o_ref):
    o_ref[...] = _dot(x_ref[...], w_ref[...]).astype(o_ref.dtype)


def _mm_bias_kernel(x_ref, w_ref, b_ref, o_ref):
    o_ref[...] = (_dot(x_ref[...], w_ref[...]) + b_ref[...]).astype(o_ref.dtype)


def _pick_tile(n, candidates):
    for c in candidates:
        if n % c == 0:
            return c
    return n


def _matmul(x, w, bias=None, out_dtype=BF16, name="proj"):
    m, k = x.shape
    n = w.shape[1]
    tm = _pick_tile(m, (1024, 512, 256))
    tn = _pick_tile(n, (1024, 768, 640, 512, 256, 128))
    in_specs = [
        pl.BlockSpec((tm, k), lambda i, j: (i, 0)),
        pl.BlockSpec((k, tn), lambda i, j: (0, j)),
    ]
    args = [x, w]
    body = _mm_kernel
    if bias is not None:
        in_specs.append(pl.BlockSpec((1, tn), lambda i, j: (0, j)))
        args.append(bias.reshape(1, n).astype(F32))
        body = _mm_bias_kernel
    return pl.pallas_call(
        body,
        out_shape=jax.ShapeDtypeStruct((m, n), out_dtype),
        grid=(m // tm, n // tn),
        in_specs=in_specs,
        out_specs=pl.BlockSpec((tm, tn), lambda i, j: (i, j)),
        compiler_params=_cparams("parallel", "parallel"),
        name=name,
    )(*args)


def _route_block(logits, count_ref):
    t = logits.shape[0]
    lane = lax.broadcasted_iota(jnp.int32, (t, LANES), 1)
    lane_f = lane.astype(F32)
    left = jnp.where(lane < N_EXPERTS, logits, NEG_INF)
    hot = jnp.zeros((t, LANES), F32)
    sels, vals = [], []
    for _ in range(TOP_K):
        m = jnp.max(left, axis=-1, keepdims=True)
        sel = jnp.min(jnp.where(left == m, lane_f, float(LANES)), axis=-1, keepdims=True)
        pick = lane_f == sel
        left = jnp.where(pick, NEG_INF, left)
        hot = jnp.where(pick, 1.0, hot)
        sels.append(sel)
        vals.append(m)
    row = lax.broadcasted_iota(jnp.int32, (t, t), 0)
    col = lax.broadcasted_iota(jnp.int32, (t, t), 1)
    earlier = jnp.where(row > col, 1.0, 0.0).astype(BF16)
    before = _dot(earlier, hot.astype(BF16)) + count_ref[...]
    count_ref[...] += jnp.sum(hot, axis=0, keepdims=True)
    exps = [jnp.exp(v - vals[0]) for v in vals]
    denom = exps[0]
    for e in exps[1:]:
        denom = denom + e
    idx = jnp.zeros((t, LANES), jnp.int32)
    gates = jnp.zeros((t, LANES), F32)
    for k in range(TOP_K):
        rank = jnp.sum(jnp.where(lane_f == sels[k], before, 0.0), axis=-1, keepdims=True)
        idx = jnp.where(lane == k, sels[k].astype(jnp.int32), idx)
        idx = jnp.where(lane == TOP_K + k, rank.astype(jnp.int32), idx)
        gates = jnp.where(lane == k, exps[k] / denom, gates)
    return idx, gates


def _resid_ln_kernel(*refs, alpha, gate_idx, shift_idx, scale_idx, with_u, with_router, combine):
    h_ref = refs[0]
    if combine:
        y_refs = refs[1:1 + TOP_K]
        yg_ref = refs[1 + TOP_K]
        pos = 2 + TOP_K
    else:
        y_ref = refs[1]
        pos = 2
    m_ref, mn_ref, g_ref, b_ref = refs[pos:pos + 4]
    pos += 4
    if with_router:
        wr_hi_ref, wr_lo_ref, br_ref = refs[pos:pos + 3]
        pos += 3
    hn_ref = refs[pos]
    pos += 1
    if combine:
        gates = yg_ref[...]
        y = gates[:, 0:1] * y_refs[0][...]
        for k in range(1, TOP_K):
            y = y + gates[:, k:k + 1] * y_refs[k][...]
    else:
        y = y_ref[...].astype(F32)
    z = alpha * h_ref[...] + y * m_ref[gate_idx:gate_idx + 1, :]
    zc = z - jnp.mean(z, axis=-1, keepdims=True)
    var = jnp.mean(zc * zc, axis=-1, keepdims=True)
    hn = zc * lax.rsqrt(var + LN_EPS) * g_ref[...] + b_ref[...]
    hn_ref[...] = hn
    if with_u:
        u_ref = refs[pos]
        pos += 1
        u = hn * (1.0 + mn_ref[scale_idx:scale_idx + 1, :]) + mn_ref[shift_idx:shift_idx + 1, :]
        u_ref[...] = u.astype(u_ref.dtype)
        if with_router:
            idx_ref, gate_ref, cnt_ref, count_ref = refs[pos:pos + 4]

            @pl.when(pl.program_id(0) == 0)
            def _():
                count_ref[...] = jnp.zeros_like(count_ref)

            u_hi, u_lo = _split_bf16(u)
            w_hi = wr_hi_ref[...]
            logits = _dot(u_hi, w_hi) + _dot(u_lo, w_hi) + _dot(u_hi, wr_lo_ref[...]) + br_ref[...]
            idx, gates = _route_block(logits, count_ref)
            idx_ref[...] = idx
            gate_ref[...] = gates
            cnt_ref[...] = count_ref[...]


def _resid_ln(h, y, mods, mods_next, ln_g, ln_b, rows_in, rows_out, *, alpha, gate_idx, next_mod=None, router=None,
              combine_gates=None):
    d = h.shape[1]
    t = rows_out.tile
    if rows_in is rows_out:
        in_row = lambda i: i
    else:
        in_row = rows_in.latent_block
    with_u = next_mod is not None
    with_router = router is not None
    combine = combine_gates is not None
    shift_idx, scale_idx = next_mod if with_u else (0, 0)
    mod_spec = pl.BlockSpec((None, None, N_MOD, d), lambda i: (rows_out.batch_of(i), rows_out.seg_of(i), 0, 0))
    in_specs = [pl.BlockSpec((t, d), lambda i: (in_row(i), 0))]
    args = [h]
    if combine:
        for k in range(TOP_K):
            in_specs.append(pl.BlockSpec((t, d), lambda i, k=k: (k * rows_in.n_blocks + in_row(i), 0)))
        in_specs.append(pl.BlockSpec((t, LANES), lambda i: (in_row(i), 0)))
        args += [y] * TOP_K + [combine_gates]
    else:
        in_specs.append(pl.BlockSpec((t, d), lambda i: (in_row(i), 0)))
        args.append(y)
    in_specs += [mod_spec, mod_spec, pl.BlockSpec((1, d), lambda i: (0, 0)), pl.BlockSpec((1, d), lambda i: (0, 0))]
    args += [mods, mods_next, ln_g.reshape(1, d), ln_b.reshape(1, d)]
    out_shape = [jax.ShapeDtypeStruct((rows_out.n_rows, d), F32)]
    out_specs = [pl.BlockSpec((t, d), lambda i: (i, 0))]
    scratch = []
    if with_router:
        wr, br = router
        w_spec = pl.BlockSpec((d, LANES), lambda i: (0, 0))
        in_specs += [w_spec, w_spec, pl.BlockSpec((1, LANES), lambda i: (0, 0))]
        args += [*_split_bf16(wr), br]
    if with_u:
        out_shape.append(jax.ShapeDtypeStruct((rows_out.n_rows, d), F32 if with_router else BF16))
        out_specs.append(pl.BlockSpec((t, d), lambda i: (i, 0)))
    if with_router:
        out_shape += [jax.ShapeDtypeStruct((rows_out.n_rows, LANES), jnp.int32),
                      jax.ShapeDtypeStruct((rows_out.n_rows, LANES), F32),
                      jax.ShapeDtypeStruct((1, LANES), F32)]
        out_specs += [pl.BlockSpec((t, LANES), lambda i: (i, 0)), pl.BlockSpec((t, LANES), lambda i: (i, 0)),
                      pl.BlockSpec((1, LANES), lambda i: (0, 0))]
        scratch.append(pltpu.VMEM((1, LANES), F32))
    return pl.pallas_call(
        functools.partial(_resid_ln_kernel, alpha=alpha, gate_idx=gate_idx, shift_idx=shift_idx,
                          scale_idx=scale_idx, with_u=with_u, with_router=with_router, combine=combine),
        out_shape=out_shape,
        grid=(rows_out.n_blocks,),
        in_specs=in_specs,
        out_specs=out_specs,
        scratch_shapes=scratch,
        compiler_params=_cparams("arbitrary" if with_router else "parallel"),
        name="resid_ln",
    )(*args)


def _gla_kernel(*refs, reverse, fuse_post, q_scale, dk, dv):
    if fuse_post:
        q_ref, k_ref, v_ref, r_ref, w2_ref, gb_ref, other_ref, g_ref, ng_ref, out_ref, state_ref = refs
    else:
        q_ref, k_ref, v_ref, r_ref, w2_ref, gb_ref, out_ref, state_ref = refs

    @pl.when(pl.program_id(1) == 0)
    def _():
        state_ref[...] = jnp.zeros_like(state_ref)

    t = q_ref.shape[0]
    c = GLA_CHUNK
    shift = c.bit_length() - 1
    r_hi, r_lo = _split_bf16(r_ref[...])
    w_hi, w_lo = _split_bf16(w2_ref[...])
    z = _dot(r_hi, w_hi) + _dot(r_lo, w_hi) + _dot(r_hi, w_lo) + gb_ref[...]
    log_a = (jnp.minimum(z, 0.0) - jnp.log(1.0 + jnp.exp(-jnp.abs(z)))) * (1.0 / GLA_GATE_NORM)
    row = lax.broadcasted_iota(jnp.int32, (t, t), 0)
    col = lax.broadcasted_iota(jnp.int32, (t, t), 1)
    same_chunk = lax.shift_right_logical(row, shift) == lax.shift_right_logical(col, shift)
    ahead = (col - row) if reverse else (row - col)
    tri = jnp.where(jnp.where(same_chunk, ahead, -1) >= 0, 1.0, 0.0).astype(BF16)
    a_hi, a_lo = _split_bf16(log_a)
    cum = _dot(tri, a_hi) + _dot(tri, a_lo)
    grow = jnp.exp(cum)
    shrink = jnp.exp(-cum)
    crow = lax.broadcasted_iota(jnp.int32, (c, c), 0)
    ccol = lax.broadcasted_iota(jnp.int32, (c, c), 1)
    causal = (ccol >= crow) if reverse else (crow >= ccol)
    n_chunks = t // c
    for h in range(GLA_HEADS):
        ks = slice(h * dk, (h + 1) * dk)
        vs = slice(h * dv, (h + 1) * dv)
        q_dec = (q_ref[:, ks].astype(F32) * q_scale * grow[:, ks]).astype(BF16)
        k_in = k_ref[:, ks].astype(F32) * shrink[:, ks]
        for step in range(n_chunks):
            j = n_chunks - 1 - step if reverse else step
            rs = slice(j * c, (j + 1) * c)
            edge = j * c if reverse else (j + 1) * c - 1
            decay = jnp.exp(cum[edge:edge + 1, ks])
            qd = q_dec[rs]
            k_intra = k_in[rs].astype(BF16)
            k_state = (k_in[rs] * decay).astype(BF16)
            v = v_ref[rs, vs]
            scores = jnp.where(causal, _dot_nt(qd, k_intra), 0.0)
            state_t = state_ref[h]
            o = _dot(scores.astype(BF16), v) + _dot_nt(qd, state_t.astype(BF16))
            kv_t = lax.dot_general(v, k_state, (((0,), (0,)), ((), ())), preferred_element_type=F32)
            state_ref[h] = state_t * decay + kv_t
            if fuse_post:
                o = o + other_ref[rs, vs]
                o = o * lax.rsqrt(jnp.mean(o * o, axis=-1, keepdims=True) + LN_EPS) * ng_ref[...]
                g = g_ref[rs, vs].astype(F32)
                out_ref[rs, vs] = (o * (g * _sigmoid(g))).astype(out_ref.dtype)
            else:
                out_ref[rs, vs] = o


def _gla_scan(qkvg, r, w2, gate_b, rows, direction, other=None, norm_g=None):
    t_rows = qkvg.shape[0]
    dk_tot = w2.shape[2]
    dk = dk_tot // GLA_HEADS
    dv_tot = (qkvg.shape[1] - 2 * dk_tot) // 2
    dv = dv_tot // GLA_HEADS
    t = rows.tile
    reverse = direction == 1
    fuse_post = other is not None

    def block(b, n):
        if reverse:
            n = jnp.where(n < rows.nb_ctx, rows.nb_ctx - 1 - n, rows.nb_seq + rows.nb_ctx - 1 - n)
        return b * rows.nb_seq + n

    in_specs = [
        pl.BlockSpec((t, dk_tot), lambda b, n: (block(b, n), 0)),
        pl.BlockSpec((t, dk_tot), lambda b, n: (block(b, n), 1)),
        pl.BlockSpec((t, dv_tot), lambda b, n: (block(b, n), (2 * dk_tot) // dv_tot)),
        pl.BlockSpec((t, LANES), lambda b, n: (block(b, n), direction)),
        pl.BlockSpec((None, LANES, dk_tot), lambda b, n: (direction, 0, 0)),
        pl.BlockSpec((None, 1, dk_tot), lambda b, n: (direction, 0, 0)),
    ]
    args = [qkvg, qkvg, qkvg, r, w2, gate_b]
    if fuse_post:
        in_specs += [
            pl.BlockSpec((t, dv_tot), lambda b, n: (block(b, n), 0)),
            pl.BlockSpec((t, dv_tot), lambda b, n: (block(b, n), (2 * dk_tot + dv_tot) // dv_tot)),
            pl.BlockSpec((1, dv), lambda b, n: (0, 0)),
        ]
        args += [other, qkvg, norm_g.reshape(1, dv).astype(F32)]
    return pl.pallas_call(
        functools.partial(_gla_kernel, reverse=reverse, fuse_post=fuse_post, q_scale=float(dk) ** -0.5, dk=dk, dv=dv),
        out_shape=jax.ShapeDtypeStruct((t_rows, dv_tot), BF16 if fuse_post else F32),
        grid=(rows.batch, rows.nb_seq),
        in_specs=in_specs,
        out_specs=pl.BlockSpec((t, dv_tot), lambda b, n: (block(b, n), 0)),
        scratch_shapes=[pltpu.VMEM((GLA_HEADS, dv, dk), F32)],
        compiler_params=_cparams("parallel", "arbitrary"),
        name="gla_scan_bwd" if reverse else "gla_scan_fwd",
    )(*args)


def _gla_mixer(u, rows, w_in, gate_w1, gate_w2, gate_b, norm_g, w_out):
    d = u.shape[1]
    dk_tot = gate_w2.shape[2]
    qkvg = _matmul(u, w_in.astype(BF16), name="gla_in")
    w1 = jnp.zeros((d, 2 * LANES), F32)
    w2 = jnp.zeros((2, LANES, dk_tot), F32)
    for dd in range(2):
        w1 = w1.at[:, dd * LANES:dd * LANES + GLA_GATE_RANK].set(gate_w1[dd])
        w2 = w2.at[dd, :GLA_GATE_RANK].set(gate_w2[dd])
    r = _matmul(u, w1.astype(BF16), out_dtype=F32, name="gla_gate_in")
    gate_b = gate_b.reshape(2, 1, dk_tot).astype(F32)
    o_fwd = _gla_scan(qkvg, r, w2, gate_b, rows, 0)
    y = _gla_scan(qkvg, r, w2, gate_b, rows, 1, other=o_fwd, norm_g=norm_g)
    return _matmul(y, w_out.astype(BF16), name="gla_out")


def _rope_kernel(x_ref, cos_ref, sin_ref, q_ref, k2_ref, v2_ref, *, q_width, kv_width):
    cos = cos_ref[...]
    sin = sin_ref[...]
    lane = lax.broadcasted_iota(jnp.int32, cos.shape, 1)
    first = (lane % 32) < 16
    low = lane < SWA_HEAD_DIM

    def rope(slab):
        partner = jnp.where(first, pltpu.roll(slab, LANES - 16, 1), pltpu.roll(slab, 16, 1))
        return slab * cos + partner * sin

    scale = SWA_HEAD_DIM ** -0.5
    for s in range(q_width // LANES):
        slab = x_ref[:, s * LANES:(s + 1) * LANES].astype(F32)
        q_ref[:, s * LANES:(s + 1) * LANES] = (rope(slab) * scale).astype(q_ref.dtype)
    for s in range(kv_width // LANES):
        k_slab = rope(x_ref[:, q_width + s * LANES:q_width + (s + 1) * LANES].astype(F32))
        v_slab = x_ref[:, q_width + kv_width + s * LANES:q_width + kv_width + (s + 1) * LANES].astype(F32)
        for slab, out_ref in ((k_slab, k2_ref), (v_slab, v2_ref)):
            swapped = pltpu.roll(slab, SWA_HEAD_DIM, 1)
            out_ref[2 * s] = jnp.where(low, slab, swapped).astype(out_ref.dtype)
            out_ref[2 * s + 1] = jnp.where(low, swapped, slab).astype(out_ref.dtype)


def _rope_tables(n_ctx, seq):
    t = jnp.arange(seq, dtype=jnp.int32)
    row = (t // GRID_W).astype(F32)
    col = (t % GRID_W).astype(F32)
    n_freq = SWA_HEAD_DIM // 4
    inv_freq = ROPE_BASE ** (-jnp.arange(n_freq, dtype=F32) / n_freq)
    lane = jnp.arange(LANES)
    within = lane % SWA_HEAD_DIM
    pos = jnp.where((within < SWA_HEAD_DIM // 2)[None, :], row[:, None], col[:, None])
    ang = pos * inv_freq[lane % n_freq][None, :]
    sign = jnp.where((lane % 32) < 16, -1.0, 1.0)[None, :]
    cos = jnp.concatenate([jnp.ones((n_ctx, LANES), F32), jnp.cos(ang)], axis=0)
    sin = jnp.concatenate([jnp.zeros((n_ctx, LANES), F32), jnp.sin(ang) * sign], axis=0)
    return cos, sin


def _rope_split(qkv, cos, sin, rows, q_width, kv_width):
    t_rows = qkv.shape[0]
    t = rows.tile
    n_kv = kv_width // SWA_HEAD_DIM
    seq_rows = rows.nb_seq * t
    kv_shape = jax.ShapeDtypeStruct((rows.batch, n_kv, seq_rows, LANES), BF16)
    kv_spec = pl.BlockSpec((None, n_kv, t, LANES), lambda i: (i // rows.nb_seq, 0, i % rows.nb_seq, 0))
    return pl.pallas_call(
        functools.partial(_rope_kernel, q_width=q_width, kv_width=kv_width),
        out_shape=[jax.ShapeDtypeStruct((t_rows, q_width), BF16), kv_shape, kv_shape],
        grid=(rows.n_blocks,),
        in_specs=[
            pl.BlockSpec((t, qkv.shape[1]), lambda i: (i, 0)),
            pl.BlockSpec((t, LANES), lambda i: (i % rows.nb_seq, 0)),
            pl.BlockSpec((t, LANES), lambda i: (i % rows.nb_seq, 0)),
        ],
        out_specs=[pl.BlockSpec((t, q_width), lambda i: (i, 0)), kv_spec, kv_spec],
        compiler_params=_cparams("parallel"),
        name="rope_split",
    )(qkv, cos, sin)


def _attend_heads(q_ref, kk, vv, mask, sink_ref, head0, o_ref):
    tq = q_ref.shape[0]
    lane = lax.broadcasted_iota(jnp.int32, (tq, LANES), 1)
    low = lane < SWA_HEAD_DIM
    q_all = q_ref[...]
    sinks = [sink_ref[head0 + h] for h in range(SWA_GROUP)]
    scores = []
    for h in range(SWA_GROUP):
        q2 = q_all[:, (h // 2) * LANES:(h // 2 + 1) * LANES]
        qh = jnp.where(low if h % 2 == 0 else lane >= SWA_HEAD_DIM, q2, jnp.zeros_like(q2))
        s = _dot_nt(qh, kk)
        scores.append(s if mask is None else jnp.where(mask, s, NEG_INF))
    probs, denoms = [], []
    for h in range(SWA_GROUP):
        m = jnp.maximum(jnp.max(scores[h], axis=-1, keepdims=True), sinks[h])
        p = jnp.exp(scores[h] - m)
        denoms.append(jnp.sum(p, axis=-1, keepdims=True) + jnp.exp(sinks[h] - m))
        probs.append(p.astype(BF16))
    outs = [_dot(probs[h], vv) / denoms[h] for h in range(SWA_GROUP)]
    o_ref[...] = jnp.concatenate(
        [jnp.where(low, outs[2 * pair], outs[2 * pair + 1]).astype(o_ref.dtype) for pair in range(SWA_GROUP // 2)],
        axis=1)


def _attn_kernel(sink_ref, q_ref, kc_ref, vc_ref, kp_ref, vp_ref, km_ref, vm_ref, kn_ref, vn_ref,
                 o_ref, kk_ref, vv_ref, *, n_ctx, nq_ctx, nq_lat):
    n = pl.program_id(2)
    head0 = pl.program_id(1) * SWA_GROUP
    w = WINDOW

    @pl.when(n < nq_ctx)
    def _():
        _attend_heads(q_ref, kc_ref[...], vc_ref[...], None, sink_ref, head0, o_ref)

    @pl.when(n >= nq_ctx)
    def _():
        m = n - nq_ctx
        for ref, parts in ((kk_ref, (kc_ref, kp_ref, km_ref, kn_ref)), (vv_ref, (vc_ref, vp_ref, vm_ref, vn_ref))):
            ref[0:n_ctx] = parts[0][...]
            ref[n_ctx:n_ctx + w] = parts[1][...]
            ref[n_ctx + w:n_ctx + 2 * w] = parts[2][...]
            ref[n_ctx + 2 * w:n_ctx + 3 * w] = parts[3][...]
        nk = n_ctx + 3 * w
        qpos = lax.broadcasted_iota(jnp.int32, (w, nk), 0)
        col = lax.broadcasted_iota(jnp.int32, (w, nk), 1)
        rel = jnp.where(col < n_ctx, 0, col - (n_ctx + w) - qpos)
        far = 4 * w
        no_prev = jnp.where(m > 0, 0, far)
        no_next = jnp.where(m < nq_lat - 1, 0, far)
        missing = jnp.where(col < n_ctx, 0,
                            jnp.where(col < n_ctx + w, no_prev, jnp.where(col >= n_ctx + 2 * w, no_next, 0)))
        mask = jnp.abs(rel) + missing <= w
        _attend_heads(q_ref, kk_ref[...], vv_ref[...], mask, sink_ref, head0, o_ref)


def _window_attention(q, k2, v2, sink, rows, n_ctx):
    t_rows, q_width = q.shape
    batch, n_kv, seq_rows, _ = k2.shape
    w = WINDOW
    gw = SWA_GROUP * SWA_HEAD_DIM
    nq_seq = seq_rows // w
    nq_ctx = n_ctx // w
    nq_lat = nq_seq - nq_ctx
    ctx_kv = pl.BlockSpec((None, None, n_ctx, LANES), lambda b, g, n: (b, g, 0, 0))

    def band(shift):
        def index(b, g, n):
            return (b, g, nq_ctx + jnp.clip(n - nq_ctx + shift, 0, nq_lat - 1), 0)
        return pl.BlockSpec((None, None, w, LANES), index)

    nk = n_ctx + 3 * w
    return pl.pallas_call(
        functools.partial(_attn_kernel, n_ctx=n_ctx, nq_ctx=nq_ctx, nq_lat=nq_lat),
        out_shape=jax.ShapeDtypeStruct((t_rows, q_width), BF16),
        grid=(batch, n_kv, nq_seq),
        in_specs=[pl.BlockSpec(memory_space=pltpu.SMEM),
                  pl.BlockSpec((w, gw), lambda b, g, n: (b * nq_seq + n, g)),
                  ctx_kv, ctx_kv, band(-1), band(-1), band(0), band(0), band(1), band(1)],
        out_specs=pl.BlockSpec((w, gw), lambda b, g, n: (b * nq_seq + n, g)),
        scratch_shapes=[pltpu.VMEM((nk, LANES), BF16), pltpu.VMEM((nk, LANES), BF16)],
        compiler_params=_cparams("parallel", "parallel", "parallel"),
        name="window_attn",
    )(sink, q, k2, v2, k2, v2, k2, v2, k2, v2)


def _window_mixer(u, rows, n_ctx, seq, w_qkv, b_qkv, sink, w_out, b_out):
    q_width = w_out.shape[0]
    kv_width = (w_qkv.shape[1] - q_width) // 2
    qkv = _matmul(u, w_qkv.astype(BF16), bias=b_qkv, name="swa_qkv")
    cos, sin = _rope_tables(n_ctx, seq)
    q, k2, v2 = _rope_split(qkv, cos, sin, rows, q_width, kv_width)
    o = _window_attention(q, k2, v2, sink.astype(F32), rows, n_ctx)
    return _matmul(o, w_out.astype(BF16), bias=b_out, name="swa_out")


def _conv_kernel(gi_ref, go_ref, val_ref, w_ref, o_ref, p_ref, *, n_ctx):
    length = gi_ref.shape[0]
    pad = 8
    p_ref[0:pad] = jnp.zeros((pad, p_ref.shape[1]), F32)
    p_ref[pad + length:pad + length + pad] = jnp.zeros((pad, p_ref.shape[1]), F32)
    p_ref[pad:pad + length] = gi_ref[...].astype(F32) * val_ref[...].astype(F32)
    t = lax.broadcasted_iota(jnp.int32, (length, 1), 0)
    seg_start = t * (t - n_ctx) == 0
    seg_end = (t - (n_ctx - 1)) * (t - (length - 1)) == 0
    prev = jnp.where(seg_start, 0.0, p_ref[pad - 1:pad - 1 + length])
    nxt = jnp.where(seg_end, 0.0, p_ref[pad + 1:pad + 1 + length])
    z = prev * w_ref[0:1, :] + p_ref[pad:pad + length] * w_ref[1:2, :] + nxt * w_ref[2:3, :]
    o_ref[...] = (go_ref[...].astype(F32) * z).astype(o_ref.dtype)


def _conv_gate(proj, conv_w, rows, n_ctx):
    t_rows = proj.shape[0]
    d = proj.shape[1] // 3
    length = rows.nb_seq * rows.tile
    tc = LANES
    nc = d // tc
    return pl.pallas_call(
        functools.partial(_conv_kernel, n_ctx=n_ctx),
        out_shape=jax.ShapeDtypeStruct((t_rows, d), BF16),
        grid=(rows.batch, nc),
        in_specs=[
            pl.BlockSpec((length, tc), lambda b, c: (b, c)),
            pl.BlockSpec((length, tc), lambda b, c: (b, nc + c)),
            pl.BlockSpec((length, tc), lambda b, c: (b, 2 * nc + c)),
            pl.BlockSpec((CONV_WIDTH, tc), lambda b, c: (0, c)),
        ],
        out_specs=pl.BlockSpec((length, tc), lambda b, c: (b, c)),
        scratch_shapes=[pltpu.VMEM((length + 16, tc), F32)],
        compiler_params=_cparams("parallel", "parallel"),
        name="conv_gate",
    )(proj, proj, proj, conv_w.astype(F32))


def _conv_mixer(u, rows, n_ctx, w_in, w_conv, w_out):
    proj = _matmul(u, w_in.astype(BF16), name="conv_in")
    zg = _conv_gate(proj, w_conv, rows, n_ctx)
    return _matmul(zg, w_out.astype(BF16), name="conv_out")


def _row_gather_copy(u_hbm, x_buf, sem, slot, tok, r):
    return pltpu.make_async_copy(u_hbm.at[pl.ds(tok, 1)], x_buf.at[slot, pl.ds(r, 1)], sem.at[slot])


MOE_TOKEN_BITS = 15


def _row_scatter_copy(o_buf, y_hbm, sem, par, r, dst):
    return pltpu.make_async_copy(o_buf.at[par, pl.ds(r, 1)], y_hbm.at[pl.ds(dst, 1)], sem.at[par])


def _moe_kernel(blk_exp_ref, n_used_ref, rows_ref, u_hbm, wgu_ref, bgu_ref, wdn_ref, bdn_ref, y_hbm,
                x_buf, gsem, o_buf, osem, wgu_bf_ref, wdn_bf_ref):
    i = pl.program_id(0)
    tm = x_buf.shape[1]
    n_used = n_used_ref[0]
    used = i < n_used
    slot = lax.rem(i, 3)
    par = lax.rem(i, 2)
    tok_mask = (1 << MOE_TOKEN_BITS) - 1

    def wait_gather():
        pltpu.make_async_copy(u_hbm.at[pl.ds(0, tm)], x_buf.at[slot], gsem.at[slot]).wait()

    @pl.when(i == 0)
    def _():
        o_buf[...] = jnp.zeros_like(o_buf)

    @pl.when(used & (i == 0))
    def _():
        def body(r, carry):
            _row_gather_copy(u_hbm, x_buf, gsem, 0, rows_ref[tm + r] & tok_mask, r).start()
            _row_gather_copy(u_hbm, x_buf, gsem, 1, rows_ref[2 * tm + r] & tok_mask, r).start(priority=1)
            return carry
        lax.fori_loop(0, tm, body, 0, unroll=4)

    @pl.when((i >= 1) & (i <= n_used + 1) & (n_used > 0))
    def _():
        pltpu.make_async_copy(o_buf.at[par], y_hbm.at[pl.ds(0, tm)], osem.at[par]).wait()

    new_expert = (i == 0) | (blk_exp_ref[i] != blk_exp_ref[jnp.maximum(i - 1, 0)])

    @pl.when(used & new_expert)
    def _():
        wgu_bf_ref[...] = wgu_ref[...].astype(BF16)
        wdn_bf_ref[...] = wdn_ref[...].astype(BF16)

    @pl.when(used)
    def _():
        wait_gather()
        x = x_buf[slot].astype(BF16)
        ahead = lax.rem(i + 2, 3)
        for r in range(tm):
            tok = rows_ref[(i + 3) * tm + r] & tok_mask
            _row_gather_copy(u_hbm, x_buf, gsem, ahead, tok, r).start(priority=r % 2)
        for r in range(tm):
            dst = lax.shift_right_logical(rows_ref[i * tm + r], MOE_TOKEN_BITS)
            _row_scatter_copy(o_buf, y_hbm, osem, 1 - par, r, dst).start(priority=(r + 1) % 2)
        ff = wdn_ref.shape[0]
        gu = _dot(x, wgu_bf_ref[...]) + bgu_ref[...]
        glu = jnp.minimum(gu[:, :ff], SWIGLU_LIMIT)
        lin = jnp.clip(gu[:, ff:], -SWIGLU_LIMIT, SWIGLU_LIMIT)
        act = glu * _sigmoid(SWIGLU_ALPHA * glu) * (lin + 1.0)
        o_buf[par] = _dot(act.astype(BF16), wdn_bf_ref[...]) + bdn_ref[...]

    @pl.when((i >= n_used) & (i <= n_used + 1) & (n_used > 0))
    def _():
        wait_gather()

    @pl.when((i == n_used) & (i >= 1))
    def _():
        def body(r, carry):
            dst = lax.shift_right_logical(rows_ref[i * tm + r], MOE_TOKEN_BITS)
            _row_scatter_copy(o_buf, y_hbm, osem, 1 - par, r, dst).start()
            return carry
        lax.fori_loop(0, tm, body, 0, unroll=8)


def _expert_mlp(u, rows, blk_exp, n_used, n_out, layer, w_gate_up, b_gate_up, w_down, b_down):
    d = u.shape[1]
    n_rows = rows.shape[0] - MOE_ROW_BLOCK
    depth, n_exp, _, ff2 = w_gate_up.shape
    ff = ff2 // 2
    tm = MOE_ROW_BLOCK
    grid_spec = pltpu.PrefetchScalarGridSpec(
        num_scalar_prefetch=3,
        grid=(n_rows // tm,),
        in_specs=[
            pl.BlockSpec(memory_space=pl.ANY),
            pl.BlockSpec((None, None, d, ff2), lambda i, be, nu, rt: (layer, be[i], 0, 0)),
            pl.BlockSpec((None, None, 1, ff2), lambda i, be, nu, rt: (layer, be[i], 0, 0)),
            pl.BlockSpec((None, None, ff, d), lambda i, be, nu, rt: (layer, be[i], 0, 0)),
            pl.BlockSpec((None, None, 1, d), lambda i, be, nu, rt: (layer, be[i], 0, 0)),
        ],
        out_specs=pl.BlockSpec(memory_space=pl.ANY),
        scratch_shapes=[pltpu.VMEM((3, tm, d), F32), pltpu.SemaphoreType.DMA((3,)),
                        pltpu.VMEM((2, tm, d), F32), pltpu.SemaphoreType.DMA((2,)),
                        pltpu.VMEM((d, ff2), BF16), pltpu.VMEM((ff, d), BF16)],
    )
    return pl.pallas_call(
        _moe_kernel,
        out_shape=jax.ShapeDtypeStruct((n_out, d), F32),
        grid_spec=grid_spec,
        compiler_params=pltpu.CompilerParams(dimension_semantics=("arbitrary",), vmem_limit_bytes=MOE_VMEM_LIMIT),
        name="expert_mlp",
    )(blk_exp, n_used, rows, u, w_gate_up, b_gate_up.reshape(depth, n_exp, 1, ff2),
      w_down, b_down.reshape(depth, n_exp, 1, d))


def _moe(u, route_idx, counts, layer, w_gate_up, b_gate_up, w_down, b_down):
    n_tok, d = u.shape
    assert n_tok < (1 << MOE_TOKEN_BITS)
    tm = MOE_ROW_BLOCK
    n_assign = n_tok * TOP_K
    n_blocks = n_assign // tm + N_EXPERTS + 2
    n_rows = n_blocks * tm
    top_exp = route_idx[:, :TOP_K]
    counts = counts[0, :N_EXPERTS].astype(jnp.int32)
    padded = (counts + tm - 1) // tm * tm
    start = jnp.cumsum(counts) - counts
    pad_end = jnp.cumsum(padded)
    pad_start = pad_end - padded
    blk_start = jnp.arange(n_blocks, dtype=jnp.int32) * tm
    blk_exp = jnp.minimum(jnp.sum(pad_end[None, :] <= blk_start[:, None], axis=1), N_EXPERTS - 1).astype(jnp.int32)
    n_used = (pad_end[-1:] // tm).astype(jnp.int32)
    order = jnp.argsort(top_exp.reshape(-1), stable=True).astype(jnp.int32)
    row_exp = jnp.repeat(blk_exp, tm)
    within = jnp.arange(n_rows, dtype=jnp.int32) - pad_start[row_exp]
    src = jnp.clip(start[row_exp] + within, 0, n_assign - 1)
    assign = order[src]
    valid = within < counts[row_exp]
    spare = n_assign + jnp.arange(n_rows, dtype=jnp.int32) % tm
    row_tok = jnp.where(valid, assign // TOP_K, 0)
    row_dst = jnp.where(valid, (assign % TOP_K) * n_tok + assign // TOP_K, spare)
    rows = row_tok | (row_dst << MOE_TOKEN_BITS)
    rows = jnp.concatenate([spare[:tm] << MOE_TOKEN_BITS, rows])
    return _expert_mlp(u, rows, blk_exp, n_used, n_assign + tm, layer, w_gate_up, b_gate_up, w_down, b_down)


def kernel(x, c, ctx, c_ctx, ada_w, ada_b, ln_g, ln_b, gla_w_in, gla_gate_w1, gla_gate_w2, gla_gate_b, gla_norm_g, gla_w_out, swa_w_qkv, swa_b_qkv, swa_sink, swa_w_out, swa_b_out, conv_w_in, conv_w, conv_w_out, moe_router_w, moe_router_b, moe_w_gate_up, moe_b_gate_up, moe_w_down, moe_b_down):
    batch, seq, d = x.shape
    n_ctx = ctx.shape[1]
    depth = ada_w.shape[0]
    alpha = (2 * depth) ** 0.25
    rows_full = _Rows(batch, n_ctx // ROW_TILE, seq // ROW_TILE)
    rows_lat = _Rows(batch, 0, seq // ROW_TILE)

    cond = jnp.zeros((8, d), F32).at[0].set(c_ctx).at[1:1 + batch].set(c)
    ada = _ada_mods(cond, ada_w, ada_b)
    mods_ctx = jnp.broadcast_to(ada[:, 0:1], (depth, batch, N_MOD * d))
    mods = jnp.stack([mods_ctx, ada[:, 1:1 + batch]], axis=2).reshape(depth, batch, 2, N_MOD, d)

    router_w = jnp.zeros((depth, d, LANES), F32).at[:, :, :N_EXPERTS].set(moe_router_w)
    router_b = jnp.zeros((depth, 1, LANES), F32).at[:, 0, :N_EXPERTS].set(moe_router_b)

    h = jnp.concatenate([ctx, x], axis=1).reshape(rows_full.n_rows, d)
    u = _modulate(h, mods[0], rows_full, 0, 1)
    rows = rows_full
    for i in range(depth):
        last = i == depth - 1
        kind, j = i % 3, i // 3
        if kind == 0:
            y = _gla_mixer(u, rows, gla_w_in[j], gla_gate_w1[j], gla_gate_w2[j], gla_gate_b[j], gla_norm_g[j], gla_w_out[j])
        elif kind == 1:
            y = _window_mixer(u, rows, n_ctx, seq, swa_w_qkv[j], swa_b_qkv[j], swa_sink[j], swa_w_out[j], swa_b_out[j])
        else:
            y = _conv_mixer(u, rows, n_ctx, conv_w_in[j], conv_w[j], conv_w_out[j])
        rows_out = rows_lat if last else rows
        h, u, route_idx, route_gate, counts = _resid_ln(
            h, y, mods[i], mods[i], ln_g[i, 0], ln_b[i, 0], rows, rows_out, alpha=alpha, gate_idx=2,
            next_mod=(3, 4), router=(router_w[i], router_b[i]))
        rows = rows_out
        y = _moe(u, route_idx, counts, i, moe_w_gate_up, moe_b_gate_up, moe_w_down, moe_b_down)
        if last:
            (h,) = _resid_ln(h, y, mods[i], mods[i], ln_g[i, 1], ln_b[i, 1], rows, rows, alpha=alpha, gate_idx=5,
                             combine_gates=route_gate)
        else:
            h, u = _resid_ln(h, y, mods[i], mods[i + 1], ln_g[i, 1], ln_b[i, 1], rows, rows, alpha=alpha, gate_idx=5,
                             next_mod=(0, 1), combine_gates=route_gate)
    return h.reshape(batch, seq, d)
```

```python
import functools

import jax
import jax.numpy as jnp
from jax import lax
from jax.experimental import pallas as pl
from jax.experimental.pallas import tpu as pltpu

F32 = jnp.float32
BF16 = jnp.bfloat16
HIGHEST = lax.Precision.HIGHEST

GRID_W = 64
GLA_HEADS = 4
GLA_GATE_RANK = 16
GLA_GATE_NORM = 16.0
GLA_CHUNK = 64
SWA_HEAD_DIM = 64
SWA_GROUP = 8
WINDOW = 128
ROPE_BASE = 10000.0
CONV_WIDTH = 3
N_EXPERTS = 32
TOP_K = 4
SWIGLU_LIMIT = 7.0
SWIGLU_ALPHA = 1.702
N_MOD = 6
LN_EPS = 1e-5
NEG_INF = -1e30

LANES = 128
ROW_TILE = 256
MOE_ROW_BLOCK = 256
VMEM_LIMIT = 48 * 1024 * 1024
MOE_VMEM_LIMIT = 60 * 1024 * 1024


def _cparams(*sem):
    return pltpu.CompilerParams(dimension_semantics=sem, vmem_limit_bytes=VMEM_LIMIT)


def _sigmoid(x):
    return 1.0 / (1.0 + jnp.exp(-x))


def _dot(a, b):
    return jnp.dot(a, b, preferred_element_type=F32)


def _dot_nt(a, b):
    return lax.dot_general(a, b, (((1,), (1,)), ((), ())), preferred_element_type=F32)


def _split_bf16(x):
    hi = x.astype(BF16)
    lo = (x - hi.astype(F32)).astype(BF16)
    return hi, lo


class _Rows:
    def __init__(self, batch, nb_ctx, nb_lat, tile=ROW_TILE):
        self.batch, self.nb_ctx, self.nb_lat, self.tile = batch, nb_ctx, nb_lat, tile
        self.nb_seq = nb_ctx + nb_lat
        self.n_blocks = batch * self.nb_seq
        self.n_rows = self.n_blocks * tile

    def batch_of(self, i):
        return i // self.nb_seq

    def seg_of(self, i):
        if self.nb_ctx == 0:
            return 1
        return jnp.where(i % self.nb_seq >= self.nb_ctx, 1, 0)

    def latent_block(self, i):
        return (i // self.nb_lat) * self.nb_seq + self.nb_ctx + i % self.nb_lat


def _ada_kernel(c_ref, w_ref, b_ref, o_ref):
    c = c_ref[...]
    s = c * _sigmoid(c)
    o_ref[...] = jnp.dot(s, w_ref[...], precision=HIGHEST, preferred_element_type=F32) + b_ref[...]


def _ada_mods(cond, ada_w, ada_b):
    depth, d, n = ada_w.shape
    tn = 1024 if n % 1024 == 0 else n
    return pl.pallas_call(
        _ada_kernel,
        out_shape=jax.ShapeDtypeStruct((depth, 8, n), F32),
        grid=(depth, n // tn),
        in_specs=[
            pl.BlockSpec((8, d), lambda l, j: (0, 0)),
            pl.BlockSpec((None, d, tn), lambda l, j: (l, 0, j)),
            pl.BlockSpec((None, 1, tn), lambda l, j: (l, 0, j)),
        ],
        out_specs=pl.BlockSpec((None, 8, tn), lambda l, j: (l, 0, j)),
        compiler_params=_cparams("parallel", "parallel"),
        name="ada_mods",
    )(cond, ada_w, ada_b.reshape(depth, 1, n))


def _modulate_kernel(h_ref, m_ref, u_ref, *, shift_idx, scale_idx):
    h = h_ref[...]
    u = h * (1.0 + m_ref[scale_idx:scale_idx + 1, :]) + m_ref[shift_idx:shift_idx + 1, :]
    u_ref[...] = u.astype(u_ref.dtype)


def _modulate(h, mods, rows, shift_idx, scale_idx):
    d = h.shape[1]
    t = rows.tile
    return pl.pallas_call(
        functools.partial(_modulate_kernel, shift_idx=shift_idx, scale_idx=scale_idx),
        out_shape=jax.ShapeDtypeStruct(h.shape, BF16),
        grid=(rows.n_blocks,),
        in_specs=[
            pl.BlockSpec((t, d), lambda i: (i, 0)),
            pl.BlockSpec((None, None, N_MOD, d), lambda i: (rows.batch_of(i), rows.seg_of(i), 0, 0)),
        ],
        out_specs=pl.BlockSpec((t, d), lambda i: (i, 0)),
        compiler_params=_cparams("parallel"),
        name="modulate",
    )(h, mods)


def _mm_kernel(x_ref, w_ref, o_ref):
    o_ref[...] = _dot(x_ref[...], w_ref[...]).astype(o_ref.dtype)


def _mm_bias_kernel(x_ref, w_ref, b_ref, o_ref):
    o_ref[...] = (_dot(x_ref[...], w_ref[...]) + b_ref[...]).astype(o_ref.dtype)


def _pick_tile(n, candidates):
    for c in candidates:
        if n % c == 0:
            return c
    return n


def _matmul(x, w, bias=None, out_dtype=BF16, name="proj"):
    m, k = x.shape
    n = w.shape[1]
    tm = _pick_tile(m, (1024, 512, 256))
    tn = _pick_tile(n, (1024, 768, 640, 512, 256, 128))
    in_specs = [
        pl.BlockSpec((tm, k), lambda i, j: (i, 0)),
        pl.BlockSpec((k, tn), lambda i, j: (0, j)),
    ]
    args = [x, w]
    body = _mm_kernel
    if bias is not None:
        in_specs.append(pl.BlockSpec((1, tn), lambda i, j: (0, j)))
        args.append(bias.reshape(1, n).astype(F32))
        body = _mm_bias_kernel
    return pl.pallas_call(
        body,
        out_shape=jax.ShapeDtypeStruct((m, n), out_dtype),
        grid=(m // tm, n // tn),
        in_specs=in_specs,
        out_specs=pl.BlockSpec((tm, tn), lambda i, j: (i, j)),
        compiler_params=_cparams("parallel", "parallel"),
        name=name,
    )(*args)


def _route_block(logits, count_ref):
    t = logits.shape[0]
    lane = lax.broadcasted_iota(jnp.int32, (t, LANES), 1)
    lane_f = lane.astype(F32)
    left = jnp.where(lane < N_EXPERTS, logits, NEG_INF)
    hot = jnp.zeros((t, LANES), F32)
    sels, vals = [], []
    for _ in range(TOP_K):
        m = jnp.max(left, axis=-1, keepdims=True)
        sel = jnp.min(jnp.where(left == m, lane_f, float(LANES)), axis=-1, keepdims=True)
        pick = lane_f == sel
        left = jnp.where(pick, NEG_INF, left)
        hot = jnp.where(pick, 1.0, hot)
        sels.append(sel)
        vals.append(m)
    row = lax.broadcasted_iota(jnp.int32, (t, t), 0)
    col = lax.broadcasted_iota(jnp.int32, (t, t), 1)
    earlier = jnp.where(row > col, 1.0, 0.0).astype(BF16)
    before = _dot(earlier, hot.astype(BF16)) + count_ref[...]
    count_ref[...] += jnp.sum(hot, axis=0, keepdims=True)
    exps = [jnp.exp(v - vals[0]) for v in vals]
    denom = exps[0]
    for e in exps[1:]:
        denom = denom + e
    idx = jnp.zeros((t, LANES), jnp.int32)
    gates = jnp.zeros((t, LANES), F32)
    for k in range(TOP_K):
        rank = jnp.sum(jnp.where(lane_f == sels[k], before, 0.0), axis=-1, keepdims=True)
        idx = jnp.where(lane == k, sels[k].astype(jnp.int32), idx)
        idx = jnp.where(lane == TOP_K + k, rank.astype(jnp.int32), idx)
        gates = jnp.where(lane == k, exps[k] / denom, gates)
    return idx, gates


def _resid_ln_kernel(*refs, alpha, gate_idx, shift_idx, scale_idx, with_u, with_router, combine):
    h_ref = refs[0]
    if combine:
        y_refs = refs[1:1 + TOP_K]
        yg_ref = refs[1 + TOP_K]
        pos = 2 + TOP_K
    else:
        y_ref = refs[1]
        pos = 2
    m_ref, mn_ref, g_ref, b_ref = refs[pos:pos + 4]
    pos += 4
    if with_router:
        wr_hi_ref, wr_lo_ref, br_ref = refs[pos:pos + 3]
        pos += 3
    hn_ref = refs[pos]
    pos += 1
    if combine:
        gates = yg_ref[...]
        y = gates[:, 0:1] * y_refs[0][...]
        for k in range(1, TOP_K):
            y = y + gates[:, k:k + 1] * y_refs[k][...]
    else:
        y = y_ref[...].astype(F32)
    z = alpha * h_ref[...] + y * m_ref[gate_idx:gate_idx + 1, :]
    zc = z - jnp.mean(z, axis=-1, keepdims=True)
    var = jnp.mean(zc * zc, axis=-1, keepdims=True)
    hn = zc * lax.rsqrt(var + LN_EPS) * g_ref[...] + b_ref[...]
    hn_ref[...] = hn
    if with_u:
        u_ref = refs[pos]
        pos += 1
        u = hn * (1.0 + mn_ref[scale_idx:scale_idx + 1, :]) + mn_ref[shift_idx:shift_idx + 1, :]
        u_ref[...] = u.astype(u_ref.dtype)
        if with_router:
            idx_ref, gate_ref, cnt_ref, count_ref = refs[pos:pos + 4]

            @pl.when(pl.program_id(0) == 0)
            def _():
                count_ref[...] = jnp.zeros_like(count_ref)

            u_hi, u_lo = _split_bf16(u)
            w_hi = wr_hi_ref[...]
            logits = _dot(u_hi, w_hi) + _dot(u_lo, w_hi) + _dot(u_hi, wr_lo_ref[...]) + br_ref[...]
            idx, gates = _route_block(logits, count_ref)
            idx_ref[...] = idx
            gate_ref[...] = gates
            cnt_ref[...] = count_ref[...]


def _resid_ln(h, y, mods, mods_next, ln_g, ln_b, rows_in, rows_out, *, alpha, gate_idx, next_mod=None, router=None,
              combine_gates=None):
    d = h.shape[1]
    t = rows_out.tile
    if rows_in is rows_out:
        in_row = lambda i: i
    else:
        in_row = rows_in.latent_block
    with_u = next_mod is not None
    with_router = router is not None
    combine = combine_gates is not None
    shift_idx, scale_idx = next_mod if with_u else (0, 0)
    mod_spec = pl.BlockSpec((None, None, N_MOD, d), lambda i: (rows_out.batch_of(i), rows_out.seg_of(i), 0, 0))
    in_specs = [pl.BlockSpec((t, d), lambda i: (in_row(i), 0))]
    args = [h]
    if combine:
        for k in range(TOP_K):
            in_specs.append(pl.BlockSpec((t, d), lambda i, k=k: (k * rows_in.n_blocks + in_row(i), 0)))
        in_specs.append(pl.BlockSpec((t, LANES), lambda i: (in_row(i), 0)))
        args += [y] * TOP_K + [combine_gates]
    else:
        in_specs.append(pl.BlockSpec((t, d), lambda i: (in_row(i), 0)))
        args.append(y)
    in_specs += [mod_spec, mod_spec, pl.BlockSpec((1, d), lambda i: (0, 0)), pl.BlockSpec((1, d), lambda i: (0, 0))]
    args += [mods, mods_next, ln_g.reshape(1, d), ln_b.reshape(1, d)]
    out_shape = [jax.ShapeDtypeStruct((rows_out.n_rows, d), F32)]
    out_specs = [pl.BlockSpec((t, d), lambda i: (i, 0))]
    scratch = []
    if with_router:
        wr, br = router
        w_spec = pl.BlockSpec((d, LANES), lambda i: (0, 0))
        in_specs += [w_spec, w_spec, pl.BlockSpec((1, LANES), lambda i: (0, 0))]
        args += [*_split_bf16(wr), br]
    if with_u:
        out_shape.append(jax.ShapeDtypeStruct((rows_out.n_rows, d), F32 if with_router else BF16))
        out_specs.append(pl.BlockSpec((t, d), lambda i: (i, 0)))
    if with_router:
        out_shape += [jax.ShapeDtypeStruct((rows_out.n_rows, LANES), jnp.int32),
                      jax.ShapeDtypeStruct((rows_out.n_rows, LANES), F32),
                      jax.ShapeDtypeStruct((1, LANES), F32)]
        out_specs += [pl.BlockSpec((t, LANES), lambda i: (i, 0)), pl.BlockSpec((t, LANES), lambda i: (i, 0)),
                      pl.BlockSpec((1, LANES), lambda i: (0, 0))]
        scratch.append(pltpu.VMEM((1, LANES), F32))
    return pl.pallas_call(
        functools.partial(_resid_ln_kernel, alpha=alpha, gate_idx=gate_idx, shift_idx=shift_idx,
                          scale_idx=scale_idx, with_u=with_u, with_router=with_router, combine=combine),
        out_shape=out_shape,
        grid=(rows_out.n_blocks,),
        in_specs=in_specs,
        out_specs=out_specs,
        scratch_shapes=scratch,
        compiler_params=_cparams("arbitrary" if with_router else "parallel"),
        name="resid_ln",
    )(*args)


def _gla_kernel(*refs, reverse, fuse_post, q_scale, dk, dv):
    if fuse_post:
        q_ref, k_ref, v_ref, r_ref, w2_ref, gb_ref, other_ref, g_ref, ng_ref, out_ref, state_ref = refs
    else:
        q_ref, k_ref, v_ref, r_ref, w2_ref, gb_ref, out_ref, state_ref = refs

    @pl.when(pl.program_id(1) == 0)
    def _():
        state_ref[...] = jnp.zeros_like(state_ref)

    t = q_ref.shape[0]
    c = GLA_CHUNK
    shift = c.bit_length() - 1
    r_hi, r_lo = _split_bf16(r_ref[...])
    w_hi, w_lo = _split_bf16(w2_ref[...])
    z = _dot(r_hi, w_hi) + _dot(r_lo, w_hi) + _dot(r_hi, w_lo) + gb_ref[...]
    log_a = (jnp.minimum(z, 0.0) - jnp.log(1.0 + jnp.exp(-jnp.abs(z)))) * (1.0 / GLA_GATE_NORM)
    row = lax.broadcasted_iota(jnp.int32, (t, t), 0)
    col = lax.broadcasted_iota(jnp.int32, (t, t), 1)
    same_chunk = lax.shift_right_logical(row, shift) == lax.shift_right_logical(col, shift)
    ahead = (col - row) if reverse else (row - col)
    tri = jnp.where(jnp.where(same_chunk, ahead, -1) >= 0, 1.0, 0.0).astype(BF16)
    a_hi, a_lo = _split_bf16(log_a)
    cum = _dot(tri, a_hi) + _dot(tri, a_lo)
    grow = jnp.exp(cum)
    shrink = jnp.exp(-cum)
    crow = lax.broadcasted_iota(jnp.int32, (c, c), 0)
    ccol = lax.broadcasted_iota(jnp.int32, (c, c), 1)
    causal = (ccol >= crow) if reverse else (crow >= ccol)
    n_chunks = t // c
    for h in range(GLA_HEADS):
        ks = slice(h * dk, (h + 1) * dk)
        vs = slice(h * dv, (h + 1) * dv)
        q_dec = (q_ref[:, ks].astype(F32) * q_scale * grow[:, ks]).astype(BF16)
        k_in = k_ref[:, ks].astype(F32) * shrink[:, ks]
        for step in range(n_chunks):
            j = n_chunks - 1 - step if reverse else step
            rs = slice(j * c, (j + 1) * c)
            edge = j * c if reverse else (j + 1) * c - 1
            decay = jnp.exp(cum[edge:edge + 1, ks])
            qd = q_dec[rs]
            k_intra = k_in[rs].astype(BF16)
            k_state = (k_in[rs] * decay).astype(BF16)
            v = v_ref[rs, vs]
            scores = jnp.where(causal, _dot_nt(qd, k_intra), 0.0)
            state_t = state_ref[h]
            o = _dot(scores.astype(BF16), v) + _dot_nt(qd, state_t.astype(BF16))
            kv_t = lax.dot_general(v, k_state, (((0,), (0,)), ((), ())), preferred_element_type=F32)
            state_ref[h] = state_t * decay + kv_t
            if fuse_post:
                o = o + other_ref[rs, vs]
                o = o * lax.rsqrt(jnp.mean(o * o, axis=-1, keepdims=True) + LN_EPS) * ng_ref[...]
                g = g_ref[rs, vs].astype(F32)
                out_ref[rs, vs] = (o * (g * _sigmoid(g))).astype(out_ref.dtype)
            else:
                out_ref[rs, vs] = o


def _gla_scan(qkvg, r, w2, gate_b, rows, direction, other=None, norm_g=None):
    t_rows = qkvg.shape[0]
    dk_tot = w2.shape[2]
    dk = dk_tot // GLA_HEADS
    dv_tot = (qkvg.shape[1] - 2 * dk_tot) // 2
    dv = dv_tot // GLA_HEADS
    t = rows.tile
    reverse = direction == 1
    fuse_post = other is not None

    def block(b, n):
        if reverse:
            n = jnp.where(n < rows.nb_ctx, rows.nb_ctx - 1 - n, rows.nb_seq + rows.nb_ctx - 1 - n)
        return b * rows.nb_seq + n

    in_specs = [
        pl.BlockSpec((t, dk_tot), lambda b, n: (block(b, n), 0)),
        pl.BlockSpec((t, dk_tot), lambda b, n: (block(b, n), 1)),
        pl.BlockSpec((t, dv_tot), lambda b, n: (block(b, n), (2 * dk_tot) // dv_tot)),
        pl.BlockSpec((t, LANES), lambda b, n: (block(b, n), direction)),
        pl.BlockSpec((None, LANES, dk_tot), lambda b, n: (direction, 0, 0)),
        pl.BlockSpec((None, 1, dk_tot), lambda b, n: (direction, 0, 0)),
    ]
    args = [qkvg, qkvg, qkvg, r, w2, gate_b]
    if fuse_post:
        in_specs += [
            pl.BlockSpec((t, dv_tot), lambda b, n: (block(b, n), 0)),
            pl.BlockSpec((t, dv_tot), lambda b, n: (block(b, n), (2 * dk_tot + dv_tot) // dv_tot)),
            pl.BlockSpec((1, dv), lambda b, n: (0, 0)),
        ]
        args += [other, qkvg, norm_g.reshape(1, dv).astype(F32)]
    return pl.pallas_call(
        functools.partial(_gla_kernel, reverse=reverse, fuse_post=fuse_post, q_scale=float(dk) ** -0.5, dk=dk, dv=dv),
        out_shape=jax.ShapeDtypeStruct((t_rows, dv_tot), BF16 if fuse_post else F32),
        grid=(rows.batch, rows.nb_seq),
        in_specs=in_specs,
        out_specs=pl.BlockSpec((t, dv_tot), lambda b, n: (block(b, n), 0)),
        scratch_shapes=[pltpu.VMEM((GLA_HEADS, dv, dk), F32)],
        compiler_params=_cparams("parallel", "arbitrary"),
        name="gla_scan_bwd" if reverse else "gla_scan_fwd",
    )(*args)


def _gla_mixer(u, rows, w_in, gate_w1, gate_w2, gate_b, norm_g, w_out):
    d = u.shape[1]
    dk_tot = gate_w2.shape[2]
    qkvg = _matmul(u, w_in.astype(BF16), name="gla_in")
    w1 = jnp.zeros((d, 2 * LANES), F32)
    w2 = jnp.zeros((2, LANES, dk_tot), F32)
    for dd in range(2):
        w1 = w1.at[:, dd * LANES:dd * LANES + GLA_GATE_RANK].set(gate_w1[dd])
        w2 = w2.at[dd, :GLA_GATE_RANK].set(gate_w2[dd])
    r = _matmul(u, w1.astype(BF16), out_dtype=F32, name="gla_gate_in")
    gate_b = gate_b.reshape(2, 1, dk_tot).astype(F32)
    o_fwd = _gla_scan(qkvg, r, w2, gate_b, rows, 0)
    y = _gla_scan(qkvg, r, w2, gate_b, rows, 1, other=o_fwd, norm_g=norm_g)
    return _matmul(y, w_out.astype(BF16), name="gla_out")


def _rope_kernel(x_ref, cos_ref, sin_ref, q_ref, k2_ref, v2_ref, *, q_width, kv_width):
    cos = cos_ref[...]
    sin = sin_ref[...]
    lane = lax.broadcasted_iota(jnp.int32, cos.shape, 1)
    first = (lane % 32) < 16
    low = lane < SWA_HEAD_DIM

    def rope(slab):
        partner = jnp.where(first, pltpu.roll(slab, LANES - 16, 1), pltpu.roll(slab, 16, 1))
        return slab * cos + partner * sin

    scale = SWA_HEAD_DIM ** -0.5
    for s in range(q_width // LANES):
        slab = x_ref[:, s * LANES:(s + 1) * LANES].astype(F32)
        q_ref[:, s * LANES:(s + 1) * LANES] = (rope(slab) * scale).astype(q_ref.dtype)
    for s in range(kv_width // LANES):
        k_slab = rope(x_ref[:, q_width + s * LANES:q_width + (s + 1) * LANES].astype(F32))
        v_slab = x_ref[:, q_width + kv_width + s * LANES:q_width + kv_width + (s + 1) * LANES].astype(F32)
        for slab, out_ref in ((k_slab, k2_ref), (v_slab, v2_ref)):
            swapped = pltpu.roll(slab, SWA_HEAD_DIM, 1)
            out_ref[2 * s] = jnp.where(low, slab, swapped).astype(out_ref.dtype)
            out_ref[2 * s + 1] = jnp.where(low, swapped, slab).astype(out_ref.dtype)


def _rope_tables(n_ctx, seq):
    t = jnp.arange(seq, dtype=jnp.int32)
    row = (t // GRID_W).astype(F32)
    col = (t % GRID_W).astype(F32)
    n_freq = SWA_HEAD_DIM // 4
    inv_freq = ROPE_BASE ** (-jnp.arange(n_freq, dtype=F32) / n_freq)
    lane = jnp.arange(LANES)
    within = lane % SWA_HEAD_DIM
    pos = jnp.where((within < SWA_HEAD_DIM // 2)[None, :], row[:, None], col[:, None])
    ang = pos * inv_freq[lane % n_freq][None, :]
    sign = jnp.where((lane % 32) < 16, -1.0, 1.0)[None, :]
    cos = jnp.concatenate([jnp.ones((n_ctx, LANES), F32), jnp.cos(ang)], axis=0)
    sin = jnp.concatenate([jnp.zeros((n_ctx, LANES), F32), jnp.sin(ang) * sign], axis=0)
    return cos, sin


def _rope_split(qkv, cos, sin, rows, q_width, kv_width):
    t_rows = qkv.shape[0]
    t = rows.tile
    n_kv = kv_width // SWA_HEAD_DIM
    seq_rows = rows.nb_seq * t
    kv_shape = jax.ShapeDtypeStruct((rows.batch, n_kv, seq_rows, LANES), BF16)
    kv_spec = pl.BlockSpec((None, n_kv, t, LANES), lambda i: (i // rows.nb_seq, 0, i % rows.nb_seq, 0))
    return pl.pallas_call(
        functools.partial(_rope_kernel, q_width=q_width, kv_width=kv_width),
        out_shape=[jax.ShapeDtypeStruct((t_rows, q_width), BF16), kv_shape, kv_shape],
        grid=(rows.n_blocks,),
        in_specs=[
            pl.BlockSpec((t, qkv.shape[1]), lambda i: (i, 0)),
            pl.BlockSpec((t, LANES), lambda i: (i % rows.nb_seq, 0)),
            pl.BlockSpec((t, LANES), lambda i: (i % rows.nb_seq, 0)),
        ],
        out_specs=[pl.BlockSpec((t, q_width), lambda i: (i, 0)), kv_spec, kv_spec],
        compiler_params=_cparams("parallel"),
        name="rope_split",
    )(qkv, cos, sin)


def _attend_heads(q_ref, kk, vv, mask, sink_ref, head0, o_ref):
    tq = q_ref.shape[0]
    lane = lax.broadcasted_iota(jnp.int32, (tq, LANES), 1)
    low = lane < SWA_HEAD_DIM
    q_all = q_ref[...]
    sinks = [sink_ref[head0 + h] for h in range(SWA_GROUP)]
    scores = []
    for h in range(SWA_GROUP):
        q2 = q_all[:, (h // 2) * LANES:(h // 2 + 1) * LANES]
        qh = jnp.where(low if h % 2 == 0 else lane >= SWA_HEAD_DIM, q2, jnp.zeros_like(q2))
        s = _dot_nt(qh, kk)
        scores.append(s if mask is None else jnp.where(mask, s, NEG_INF))
    probs, denoms = [], []
    for h in range(SWA_GROUP):
        m = jnp.maximum(jnp.max(scores[h], axis=-1, keepdims=True), sinks[h])
        p = jnp.exp(scores[h] - m)
        denoms.append(jnp.sum(p, axis=-1, keepdims=True) + jnp.exp(sinks[h] - m))
        probs.append(p.astype(BF16))
    outs = [_dot(probs[h], vv) / denoms[h] for h in range(SWA_GROUP)]
    o_ref[...] = jnp.concatenate(
        [jnp.where(low, outs[2 * pair], outs[2 * pair + 1]).astype(o_ref.dtype) for pair in range(SWA_GROUP // 2)],
        axis=1)


def _attn_kernel(sink_ref, q_ref, kc_ref, vc_ref, kp_ref, vp_ref, km_ref, vm_ref, kn_ref, vn_ref,
                 o_ref, kk_ref, vv_ref, *, n_ctx, nq_ctx, nq_lat):
    n = pl.program_id(2)
    head0 = pl.program_id(1) * SWA_GROUP
    w = WINDOW

    @pl.when(n < nq_ctx)
    def _():
        _attend_heads(q_ref, kc_ref[...], vc_ref[...], None, sink_ref, head0, o_ref)

    @pl.when(n >= nq_ctx)
    def _():
        m = n - nq_ctx
        for ref, parts in ((kk_ref, (kc_ref, kp_ref, km_ref, kn_ref)), (vv_ref, (vc_ref, vp_ref, vm_ref, vn_ref))):
            ref[0:n_ctx] = parts[0][...]
            ref[n_ctx:n_ctx + w] = parts[1][...]
            ref[n_ctx + w:n_ctx + 2 * w] = parts[2][...]
            ref[n_ctx + 2 * w:n_ctx + 3 * w] = parts[3][...]
        nk = n_ctx + 3 * w
        qpos = lax.broadcasted_iota(jnp.int32, (w, nk), 0)
        col = lax.broadcasted_iota(jnp.int32, (w, nk), 1)
        rel = jnp.where(col < n_ctx, 0, col - (n_ctx + w) - qpos)
        far = 4 * w
        no_prev = jnp.where(m > 0, 0, far)
        no_next = jnp.where(m < nq_lat - 1, 0, far)
        missing = jnp.where(col < n_ctx, 0,
                            jnp.where(col < n_ctx + w, no_prev, jnp.where(col >= n_ctx + 2 * w, no_next, 0)))
        mask = jnp.abs(rel) + missing <= w
        _attend_heads(q_ref, kk_ref[...], vv_ref[...], mask, sink_ref, head0, o_ref)


def _window_attention(q, k2, v2, sink, rows, n_ctx):
    t_rows, q_width = q.shape
    batch, n_kv, seq_rows, _ = k2.shape
    w = WINDOW
    gw = SWA_GROUP * SWA_HEAD_DIM
    nq_seq = seq_rows // w
    nq_ctx = n_ctx // w
    nq_lat = nq_seq - nq_ctx
    ctx_kv = pl.BlockSpec((None, None, n_ctx, LANES), lambda b, g, n: (b, g, 0, 0))

    def band(shift):
        def index(b, g, n):
            return (b, g, nq_ctx + jnp.clip(n - nq_ctx + shift, 0, nq_lat - 1), 0)
        return pl.BlockSpec((None, None, w, LANES), index)

    nk = n_ctx + 3 * w
    return pl.pallas_call(
        functools.partial(_attn_kernel, n_ctx=n_ctx, nq_ctx=nq_ctx, nq_lat=nq_lat),
        out_shape=jax.ShapeDtypeStruct((t_rows, q_width), BF16),
        grid=(batch, n_kv, nq_seq),
        in_specs=[pl.BlockSpec(memory_space=pltpu.SMEM),
                  pl.BlockSpec((w, gw), lambda b, g, n: (b * nq_seq + n, g)),
                  ctx_kv, ctx_kv, band(-1), band(-1), band(0), band(0), band(1), band(1)],
        out_specs=pl.BlockSpec((w, gw), lambda b, g, n: (b * nq_seq + n, g)),
        scratch_shapes=[pltpu.VMEM((nk, LANES), BF16), pltpu.VMEM((nk, LANES), BF16)],
        compiler_params=_cparams("parallel", "parallel", "parallel"),
        name="window_attn",
    )(sink, q, k2, v2, k2, v2, k2, v2, k2, v2)


def _window_mixer(u, rows, n_ctx, seq, w_qkv, b_qkv, sink, w_out, b_out):
    q_width = w_out.shape[0]
    kv_width = (w_qkv.shape[1] - q_width) // 2
    qkv = _matmul(u, w_qkv.astype(BF16), bias=b_qkv, name="swa_qkv")
    cos, sin = _rope_tables(n_ctx, seq)
    q, k2, v2 = _rope_split(qkv, cos, sin, rows, q_width, kv_width)
    o = _window_attention(q, k2, v2, sink.astype(F32), rows, n_ctx)
    return _matmul(o, w_out.astype(BF16), bias=b_out, name="swa_out")


def _conv_kernel(gi_ref, go_ref, val_ref, w_ref, o_ref, p_ref, *, n_ctx):
    length = gi_ref.shape[0]
    pad = 8
    p_ref[0:pad] = jnp.zeros((pad, p_ref.shape[1]), F32)
    p_ref[pad + length:pad + length + pad] = jnp.zeros((pad, p_ref.shape[1]), F32)
    p_ref[pad:pad + length] = gi_ref[...].astype(F32) * val_ref[...].astype(F32)
    t = lax.broadcasted_iota(jnp.int32, (length, 1), 0)
    seg_start = t * (t - n_ctx) == 0
    seg_end = (t - (n_ctx - 1)) * (t - (length - 1)) == 0
    prev = jnp.where(seg_start, 0.0, p_ref[pad - 1:pad - 1 + length])
    nxt = jnp.where(seg_end, 0.0, p_ref[pad + 1:pad + 1 + length])
    z = prev * w_ref[0:1, :] + p_ref[pad:pad + length] * w_ref[1:2, :] + nxt * w_ref[2:3, :]
    o_ref[...] = (go_ref[...].astype(F32) * z).astype(o_ref.dtype)


def _conv_gate(proj, conv_w, rows, n_ctx):
    t_rows = proj.shape[0]
    d = proj.shape[1] // 3
    length = rows.nb_seq * rows.tile
    tc = LANES
    nc = d // tc
    return pl.pallas_call(
        functools.partial(_conv_kernel, n_ctx=n_ctx),
        out_shape=jax.ShapeDtypeStruct((t_rows, d), BF16),
        grid=(rows.batch, nc),
        in_specs=[
            pl.BlockSpec((length, tc), lambda b, c: (b, c)),
            pl.BlockSpec((length, tc), lambda b, c: (b, nc + c)),
            pl.BlockSpec((length, tc), lambda b, c: (b, 2 * nc + c)),
            pl.BlockSpec((CONV_WIDTH, tc), lambda b, c: (0, c)),
        ],
        out_specs=pl.BlockSpec((length, tc), lambda b, c: (b, c)),
        scratch_shapes=[pltpu.VMEM((length + 16, tc), F32)],
        compiler_params=_cparams("parallel", "parallel"),
        name="conv_gate",
    )(proj, proj, proj, conv_w.astype(F32))


def _conv_mixer(u, rows, n_ctx, w_in, w_conv, w_out):
    proj = _matmul(u, w_in.astype(BF16), name="conv_in")
    zg = _conv_gate(proj, w_conv, rows, n_ctx)
    return _matmul(zg, w_out.astype(BF16), name="conv_out")


def _row_gather_copy(u_hbm, x_buf, sem, slot, tok, r):
    return pltpu.make_async_copy(u_hbm.at[pl.ds(tok, 1)], x_buf.at[slot, pl.ds(r, 1)], sem.at[slot])


MOE_TOKEN_BITS = 15


def _row_scatter_copy(o_buf, y_hbm, sem, par, r, dst):
    return pltpu.make_async_copy(o_buf.at[par, pl.ds(r, 1)], y_hbm.at[pl.ds(dst, 1)], sem.at[par])


def _moe_kernel(blk_exp_ref, n_used_ref, rows_ref, u_hbm, wgu_ref, bgu_ref, wdn_ref, bdn_ref, y_hbm,
                x_buf, gsem, o_buf, osem, wgu_bf_ref, wdn_bf_ref):
    i = pl.program_id(0)
    tm = x_buf.shape[1]
    n_used = n_used_ref[0]
    used = i < n_used
    slot = lax.rem(i, 3)
    prev = lax.rem(i + 2, 3)
    tok_mask = (1 << MOE_TOKEN_BITS) - 1

    def wait_gather():
        pltpu.make_async_copy(u_hbm.at[pl.ds(0, tm)], x_buf.at[slot], gsem.at[slot]).wait()

    @pl.when(i == 0)
    def _():
        o_buf[...] = jnp.zeros_like(o_buf)
        for half in range(2):
            fill = pltpu.make_async_copy(o_buf.at[half], y_hbm.at[pl.ds(y_hbm.shape[0] - (2 - half) * tm, tm)],
                                         osem.at[half])
            fill.start()
            fill.wait()

    @pl.when(used & (i == 0))
    def _():
        def body(r, carry):
            _row_gather_copy(u_hbm, x_buf, gsem, 0, rows_ref[tm + r] & tok_mask, r).start()
            _row_gather_copy(u_hbm, x_buf, gsem, 1, rows_ref[2 * tm + r] & tok_mask, r).start(priority=1)
            return carry
        lax.fori_loop(0, tm, body, 0, unroll=4)

    @pl.when((i >= 2) & (i <= n_used + 2) & (n_used > 0))
    def _():
        pltpu.make_async_copy(o_buf.at[slot], y_hbm.at[pl.ds(0, tm)], osem.at[slot]).wait()

    new_expert = (i == 0) | (blk_exp_ref[i] != blk_exp_ref[jnp.maximum(i - 1, 0)])

    @pl.when(used & new_expert)
    def _():
        wgu_bf_ref[...] = wgu_ref[...].astype(BF16)
        wdn_bf_ref[...] = wdn_ref[...].astype(BF16)

    @pl.when(used)
    def _():
        wait_gather()
        x = x_buf[slot].astype(BF16)
        ahead = lax.rem(i + 2, 3)
        for r in range(tm):
            tok = rows_ref[(i + 3) * tm + r] & tok_mask
            _row_gather_copy(u_hbm, x_buf, gsem, ahead, tok, r).start(priority=r % 2)
        for r in range(tm):
            dst = lax.shift_right_logical(rows_ref[i * tm + r], MOE_TOKEN_BITS)
            _row_scatter_copy(o_buf, y_hbm, osem, prev, r, dst).start(priority=(r + 1) % 2)
        ff = wdn_ref.shape[0]
        gu = _dot(x, wgu_bf_ref[...]) + bgu_ref[...]
        glu = jnp.minimum(gu[:, :ff], SWIGLU_LIMIT)
        lin = jnp.clip(gu[:, ff:], -SWIGLU_LIMIT, SWIGLU_LIMIT)
        act = glu * _sigmoid(SWIGLU_ALPHA * glu) * (lin + 1.0)
        o_buf[slot] = _dot(act.astype(BF16), wdn_bf_ref[...]) + bdn_ref[...]

    @pl.when((i >= n_used) & (i <= n_used + 1) & (n_used > 0))
    def _():
        wait_gather()

    @pl.when((i == n_used) & (i >= 1))
    def _():
        def body(r, carry):
            dst = lax.shift_right_logical(rows_ref[i * tm + r], MOE_TOKEN_BITS)
            _row_scatter_copy(o_buf, y_hbm, osem, prev, r, dst).start()
            return carry
        lax.fori_loop(0, tm, body, 0, unroll=8)


def _expert_mlp(u, rows, blk_exp, n_used, n_out, layer, w_gate_up, b_gate_up, w_down, b_down):
    d = u.shape[1]
    n_rows = rows.shape[0] - MOE_ROW_BLOCK
    depth, n_exp, _, ff2 = w_gate_up.shape
    ff = ff2 // 2
    tm = MOE_ROW_BLOCK
    grid_spec = pltpu.PrefetchScalarGridSpec(
        num_scalar_prefetch=3,
        grid=(n_rows // tm,),
        in_specs=[
            pl.BlockSpec(memory_space=pl.ANY),
            pl.BlockSpec((None, None, d, ff2), lambda i, be, nu, rt: (layer, be[i], 0, 0)),
            pl.BlockSpec((None, None, 1, ff2), lambda i, be, nu, rt: (layer, be[i], 0, 0)),
            pl.BlockSpec((None, None, ff, d), lambda i, be, nu, rt: (layer, be[i], 0, 0)),
            pl.BlockSpec((None, None, 1, d), lambda i, be, nu, rt: (layer, be[i], 0, 0)),
        ],
        out_specs=pl.BlockSpec(memory_space=pl.ANY),
        scratch_shapes=[pltpu.VMEM((3, tm, d), F32), pltpu.SemaphoreType.DMA((3,)),
                        pltpu.VMEM((3, tm, d), F32), pltpu.SemaphoreType.DMA((3,)),
                        pltpu.VMEM((d, ff2), BF16), pltpu.VMEM((ff, d), BF16)],
    )
    return pl.pallas_call(
        _moe_kernel,
        out_shape=jax.ShapeDtypeStruct((n_out, d), F32),
        grid_spec=grid_spec,
        compiler_params=pltpu.CompilerParams(dimension_semantics=("arbitrary",), vmem_limit_bytes=MOE_VMEM_LIMIT),
        name="expert_mlp",
    )(blk_exp, n_used, rows, u, w_gate_up, b_gate_up.reshape(depth, n_exp, 1, ff2),
      w_down, b_down.reshape(depth, n_exp, 1, d))


def _moe(u, route_idx, counts, layer, w_gate_up, b_gate_up, w_down, b_down):
    n_tok, d = u.shape
    assert n_tok < (1 << MOE_TOKEN_BITS)
    tm = MOE_ROW_BLOCK
    n_assign = n_tok * TOP_K
    n_blocks = n_assign // tm + N_EXPERTS + 2
    n_rows = n_blocks * tm
    top_exp = route_idx[:, :TOP_K]
    counts = counts[0, :N_EXPERTS].astype(jnp.int32)
    padded = (counts + tm - 1) // tm * tm
    start = jnp.cumsum(counts) - counts
    pad_end = jnp.cumsum(padded)
    pad_start = pad_end - padded
    blk_start = jnp.arange(n_blocks, dtype=jnp.int32) * tm
    blk_exp = jnp.minimum(jnp.sum(pad_end[None, :] <= blk_start[:, None], axis=1), N_EXPERTS - 1).astype(jnp.int32)
    n_used = (pad_end[-1:] // tm).astype(jnp.int32)
    order = jnp.argsort(top_exp.reshape(-1), stable=True).astype(jnp.int32)
    row_exp = jnp.repeat(blk_exp, tm)
    within = jnp.arange(n_rows, dtype=jnp.int32) - pad_start[row_exp]
    src = jnp.clip(start[row_exp] + within, 0, n_assign - 1)
    assign = order[src]
    valid = within < counts[row_exp]
    r = jnp.arange(n_rows, dtype=jnp.int32)
    spare = n_assign + ((r // tm + 1) % 2) * tm + r % tm
    row_tok = jnp.where(valid, assign // TOP_K, 0)
    row_dst = jnp.where(valid, (assign % TOP_K) * n_tok + assign // TOP_K, spare)
    rows = row_tok | (row_dst << MOE_TOKEN_BITS)
    rows = jnp.concatenate([(n_assign + r[:tm]) << MOE_TOKEN_BITS, rows])
    return _expert_mlp(u, rows, blk_exp, n_used, n_assign + 2 * tm, layer, w_gate_up, b_gate_up, w_down, b_down)


def kernel(x, c, ctx, c_ctx, ada_w, ada_b, ln_g, ln_b, gla_w_in, gla_gate_w1, gla_gate_w2, gla_gate_b, gla_norm_g, gla_w_out, swa_w_qkv, swa_b_qkv, swa_sink, swa_w_out, swa_b_out, conv_w_in, conv_w, conv_w_out, moe_router_w, moe_router_b, moe_w_gate_up, moe_b_gate_up, moe_w_down, moe_b_down):
    batch, seq, d = x.shape
    n_ctx = ctx.shape[1]
    depth = ada_w.shape[0]
    alpha = (2 * depth) ** 0.25
    rows_full = _Rows(batch, n_ctx // ROW_TILE, seq // ROW_TILE)
    rows_lat = _Rows(batch, 0, seq // ROW_TILE)

    cond = jnp.zeros((8, d), F32).at[0].set(c_ctx).at[1:1 + batch].set(c)
    ada = _ada_mods(cond, ada_w, ada_b)
    mods_ctx = jnp.broadcast_to(ada[:, 0:1], (depth, batch, N_MOD * d))
    mods = jnp.stack([mods_ctx, ada[:, 1:1 + batch]], axis=2).reshape(depth, batch, 2, N_MOD, d)

    router_w = jnp.zeros((depth, d, LANES), F32).at[:, :, :N_EXPERTS].set(moe_router_w)
    router_b = jnp.zeros((depth, 1, LANES), F32).at[:, 0, :N_EXPERTS].set(moe_router_b)

    h = jnp.concatenate([ctx, x], axis=1).reshape(rows_full.n_rows, d)
    u = _modulate(h, mods[0], rows_full, 0, 1)
    rows = rows_full
    for i in range(depth):
        last = i == depth - 1
        kind, j = i % 3, i // 3
        if kind == 0:
            y = _gla_mixer(u, rows, gla_w_in[j], gla_gate_w1[j], gla_gate_w2[j], gla_gate_b[j], gla_norm_g[j], gla_w_out[j])
        elif kind == 1:
            y = _window_mixer(u, rows, n_ctx, seq, swa_w_qkv[j], swa_b_qkv[j], swa_sink[j], swa_w_out[j], swa_b_out[j])
        else:
            y = _conv_mixer(u, rows, n_ctx, conv_w_in[j], conv_w[j], conv_w_out[j])
        rows_out = rows_lat if last else rows
        h, u, route_idx, route_gate, counts = _resid_ln(
            h, y, mods[i], mods[i], ln_g[i, 0], ln_b[i, 0], rows, rows_out, alpha=alpha, gate_idx=2,
            next_mod=(3, 4), router=(router_w[i], router_b[i]))
        rows = rows_out
        y = _moe(u, route_idx, counts, i, moe_w_gate_up, moe_b_gate_up, moe_w_down, moe_b_down)
        if last:
            (h,) = _resid_ln(h, y, mods[i], mods[i], ln_g[i, 1], ln_b[i, 1], rows, rows, alpha=alpha, gate_idx=5,
                             combine_gates=route_gate)
        else:
            h, u = _resid_ln(h, y, mods[i], mods[i + 1], ln_g[i, 1], ln_b[i, 1], rows, rows, alpha=alpha, gate_idx=5,
                             next_mod=(0, 1), combine_gates=route_gate)
    return h.reshape(batch, seq, d)
```

```python
import functools

import jax
import jax.numpy as jnp
from jax import lax
from jax.experimental import pallas as pl
from jax.experimental.pallas import tpu as pltpu

F32 = jnp.float32
BF16 = jnp.bfloat16
HIGHEST = lax.Precision.HIGHEST

GRID_W = 64
GLA_HEADS = 4
GLA_GATE_RANK = 16
GLA_GATE_NORM = 16.0
GLA_CHUNK = 64
SWA_HEAD_DIM = 64
SWA_GROUP = 8
WINDOW = 128
ROPE_BASE = 10000.0
CONV_WIDTH = 3
N_EXPERTS = 32
TOP_K = 4
SWIGLU_LIMIT = 7.0
SWIGLU_ALPHA = 1.702
N_MOD = 6
LN_EPS = 1e-5
NEG_INF = -1e30

LANES = 128
ROW_TILE = 256
MOE_ROW_BLOCK = 256
VMEM_LIMIT = 48 * 1024 * 1024
MOE_VMEM_LIMIT = 60 * 1024 * 1024


def _cparams(*sem):
    return pltpu.CompilerParams(dimension_semantics=sem, vmem_limit_bytes=VMEM_LIMIT)


def _sigmoid(x):
    return 1.0 / (1.0 + jnp.exp(-x))


def _dot(a, b):
    return jnp.dot(a, b, preferred_element_type=F32)


def _dot_nt(a, b):
    return lax.dot_general(a, b, (((1,), (1,)), ((), ())), preferred_element_type=F32)


def _split_bf16(x):
    hi = x.astype(BF16)
    lo = (x - hi.astype(F32)).astype(BF16)
    return hi, lo


class _Rows:
    def __init__(self, batch, nb_ctx, nb_lat, tile=ROW_TILE):
        self.batch, self.nb_ctx, self.nb_lat, self.tile = batch, nb_ctx, nb_lat, tile
        self.nb_seq = nb_ctx + nb_lat
        self.n_blocks = batch * self.nb_seq
        self.n_rows = self.n_blocks * tile

    def batch_of(self, i):
        return i // self.nb_seq

    def seg_of(self, i):
        if self.nb_ctx == 0:
            return 1
        return jnp.where(i % self.nb_seq >= self.nb_ctx, 1, 0)

    def latent_block(self, i):
        return (i // self.nb_lat) * self.nb_seq + self.nb_ctx + i % self.nb_lat


def _ada_kernel(c_ref, w_ref, b_ref, o_ref):
    c = c_ref[...]
    s = c * _sigmoid(c)
    o_ref[...] = jnp.dot(s, w_ref[...], precision=HIGHEST, preferred_element_type=F32) + b_ref[...]


def _ada_mods(cond, ada_w, ada_b):
    depth, d, n = ada_w.shape
    tn = 1024 if n % 1024 == 0 else n
    return pl.pallas_call(
        _ada_kernel,
        out_shape=jax.ShapeDtypeStruct((depth, 8, n), F32),
        grid=(depth, n // tn),
        in_specs=[
            pl.BlockSpec((8, d), lambda l, j: (0, 0)),
            pl.BlockSpec((None, d, tn), lambda l, j: (l, 0, j)),
            pl.BlockSpec((None, 1, tn), lambda l, j: (l, 0, j)),
        ],
        out_specs=pl.BlockSpec((None, 8, tn), lambda l, j: (l, 0, j)),
        compiler_params=_cparams("parallel", "parallel"),
        name="ada_mods",
    )(cond, ada_w, ada_b.reshape(depth, 1, n))


def _modulate_kernel(h_ref, m_ref, u_ref, *, shift_idx, scale_idx):
    h = h_ref[...]
    u = h * (1.0 + m_ref[scale_idx:scale_idx + 1, :]) + m_ref[shift_idx:shift_idx + 1, :]
    u_ref[...] = u.astype(u_ref.dtype)


def _modulate(h, mods, rows, shift_idx, scale_idx):
    d = h.shape[1]
    t = rows.tile
    return pl.pallas_call(
        functools.partial(_modulate_kernel, shift_idx=shift_idx, scale_idx=scale_idx),
        out_shape=jax.ShapeDtypeStruct(h.shape, BF16),
        grid=(rows.n_blocks,),
        in_specs=[
            pl.BlockSpec((t, d), lambda i: (i, 0)),
            pl.BlockSpec((None, None, N_MOD, d), lambda i: (rows.batch_of(i), rows.seg_of(i), 0, 0)),
        ],
        out_specs=pl.BlockSpec((t, d), lambda i: (i, 0)),
        compiler_params=_cparams("parallel"),
        name="modulate",
    )(h, mods)


def _mm_kernel(x_ref, w_ref, o_ref):
    o_ref[...] = _dot(x_ref[...], w_ref[...]).astype(o_ref.dtype)


def _mm_bias_kernel(x_ref, w_ref, b_ref, o_ref):
    o_ref[...] = (_dot(x_ref[...], w_ref[...]) + b_ref[...]).astype(o_ref.dtype)


def _pick_tile(n, candidates):
    for c in candidates:
        if n % c == 0:
            return c
    return n


def _matmul(x, w, bias=None, out_dtype=BF16, name="proj"):
    m, k = x.shape
    n = w.shape[1]
    tm = _pick_tile(m, (1024, 512, 256))
    tn = _pick_tile(n, (1024, 768, 640, 512, 256, 128))
    in_specs = [
        pl.BlockSpec((tm, k), lambda i, j: (i, 0)),
        pl.BlockSpec((k, tn), lambda i, j: (0, j)),
    ]
    args = [x, w]
    body = _mm_kernel
    if bias is not None:
        in_specs.append(pl.BlockSpec((1, tn), lambda i, j: (0, j)))
        args.append(bias.reshape(1, n).astype(F32))
        body = _mm_bias_kernel
    return pl.pallas_call(
        body,
        out_shape=jax.ShapeDtypeStruct((m, n), out_dtype),
        grid=(m // tm, n // tn),
        in_specs=in_specs,
        out_specs=pl.BlockSpec((tm, tn), lambda i, j: (i, j)),
        compiler_params=_cparams("parallel", "parallel"),
        name=name,
    )(*args)


def _route_block(logits, count_ref):
    t = logits.shape[0]
    lane = lax.broadcasted_iota(jnp.int32, (t, LANES), 1)
    lane_f = lane.astype(F32)
    left = jnp.where(lane < N_EXPERTS, logits, NEG_INF)
    hot = jnp.zeros((t, LANES), F32)
    sels, vals = [], []
    for _ in range(TOP_K):
        m = jnp.max(left, axis=-1, keepdims=True)
        sel = jnp.min(jnp.where(left == m, lane_f, float(LANES)), axis=-1, keepdims=True)
        pick = lane_f == sel
        left = jnp.where(pick, NEG_INF, left)
        hot = jnp.where(pick, 1.0, hot)
        sels.append(sel)
        vals.append(m)
    row = lax.broadcasted_iota(jnp.int32, (t, t), 0)
    col = lax.broadcasted_iota(jnp.int32, (t, t), 1)
    earlier = jnp.where(row > col, 1.0, 0.0).astype(BF16)
    before = _dot(earlier, hot.astype(BF16)) + count_ref[...]
    count_ref[...] += jnp.sum(hot, axis=0, keepdims=True)
    exps = [jnp.exp(v - vals[0]) for v in vals]
    denom = exps[0]
    for e in exps[1:]:
        denom = denom + e
    idx = jnp.zeros((t, LANES), jnp.int32)
    gates = jnp.zeros((t, LANES), F32)
    for k in range(TOP_K):
        rank = jnp.sum(jnp.where(lane_f == sels[k], before, 0.0), axis=-1, keepdims=True)
        idx = jnp.where(lane == k, sels[k].astype(jnp.int32), idx)
        idx = jnp.where(lane == TOP_K + k, rank.astype(jnp.int32), idx)
        gates = jnp.where(lane == k, exps[k] / denom, gates)
    return idx, gates


def _resid_ln_kernel(*refs, alpha, gate_idx, shift_idx, scale_idx, with_u, with_router, combine):
    h_ref = refs[0]
    if combine:
        y_refs = refs[1:1 + TOP_K]
        yg_ref = refs[1 + TOP_K]
        pos = 2 + TOP_K
    else:
        y_ref = refs[1]
        pos = 2
    m_ref, mn_ref, g_ref, b_ref = refs[pos:pos + 4]
    pos += 4
    if with_router:
        wr_hi_ref, wr_lo_ref, br_ref = refs[pos:pos + 3]
        pos += 3
    hn_ref = refs[pos]
    pos += 1
    if combine:
        gates = yg_ref[...]
        y = gates[:, 0:1] * y_refs[0][...]
        for k in range(1, TOP_K):
            y = y + gates[:, k:k + 1] * y_refs[k][...]
    else:
        y = y_ref[...].astype(F32)
    z = alpha * h_ref[...] + y * m_ref[gate_idx:gate_idx + 1, :]
    zc = z - jnp.mean(z, axis=-1, keepdims=True)
    var = jnp.mean(zc * zc, axis=-1, keepdims=True)
    hn = zc * lax.rsqrt(var + LN_EPS) * g_ref[...] + b_ref[...]
    hn_ref[...] = hn
    if with_u:
        u_ref = refs[pos]
        pos += 1
        u = hn * (1.0 + mn_ref[scale_idx:scale_idx + 1, :]) + mn_ref[shift_idx:shift_idx + 1, :]
        u_ref[...] = u.astype(u_ref.dtype)
        if with_router:
            idx_ref, gate_ref, cnt_ref, count_ref = refs[pos:pos + 4]

            @pl.when(pl.program_id(0) == 0)
            def _():
                count_ref[...] = jnp.zeros_like(count_ref)

            u_hi, u_lo = _split_bf16(u)
            w_hi = wr_hi_ref[...]
            logits = _dot(u_hi, w_hi) + _dot(u_lo, w_hi) + _dot(u_hi, wr_lo_ref[...]) + br_ref[...]
            idx, gates = _route_block(logits, count_ref)
            idx_ref[...] = idx
            gate_ref[...] = gates
            cnt_ref[...] = count_ref[...]


def _resid_ln(h, y, mods, mods_next, ln_g, ln_b, rows_in, rows_out, *, alpha, gate_idx, next_mod=None, router=None,
              combine_gates=None):
    d = h.shape[1]
    t = rows_out.tile
    if rows_in is rows_out:
        in_row = lambda i: i
    else:
        in_row = rows_in.latent_block
    with_u = next_mod is not None
    with_router = router is not None
    combine = combine_gates is not None
    shift_idx, scale_idx = next_mod if with_u else (0, 0)
    mod_spec = pl.BlockSpec((None, None, N_MOD, d), lambda i: (rows_out.batch_of(i), rows_out.seg_of(i), 0, 0))
    in_specs = [pl.BlockSpec((t, d), lambda i: (in_row(i), 0))]
    args = [h]
    if combine:
        for k in range(TOP_K):
            in_specs.append(pl.BlockSpec((t, d), lambda i, k=k: (k * rows_in.n_blocks + in_row(i), 0)))
        in_specs.append(pl.BlockSpec((t, LANES), lambda i: (in_row(i), 0)))
        args += [y] * TOP_K + [combine_gates]
    else:
        in_specs.append(pl.BlockSpec((t, d), lambda i: (in_row(i), 0)))
        args.append(y)
    in_specs += [mod_spec, mod_spec, pl.BlockSpec((1, d), lambda i: (0, 0)), pl.BlockSpec((1, d), lambda i: (0, 0))]
    args += [mods, mods_next, ln_g.reshape(1, d), ln_b.reshape(1, d)]
    out_shape = [jax.ShapeDtypeStruct((rows_out.n_rows, d), F32)]
    out_specs = [pl.BlockSpec((t, d), lambda i: (i, 0))]
    scratch = []
    if with_router:
        wr, br = router
        w_spec = pl.BlockSpec((d, LANES), lambda i: (0, 0))
        in_specs += [w_spec, w_spec, pl.BlockSpec((1, LANES), lambda i: (0, 0))]
        args += [*_split_bf16(wr), br]
    if with_u:
        out_shape.append(jax.ShapeDtypeStruct((rows_out.n_rows, d), F32 if with_router else BF16))
        out_specs.append(pl.BlockSpec((t, d), lambda i: (i, 0)))
    if with_router:
        out_shape += [jax.ShapeDtypeStruct((rows_out.n_rows, LANES), jnp.int32),
                      jax.ShapeDtypeStruct((rows_out.n_rows, LANES), F32),
                      jax.ShapeDtypeStruct((1, LANES), F32)]
        out_specs += [pl.BlockSpec((t, LANES), lambda i: (i, 0)), pl.BlockSpec((t, LANES), lambda i: (i, 0)),
                      pl.BlockSpec((1, LANES), lambda i: (0, 0))]
        scratch.append(pltpu.VMEM((1, LANES), F32))
    return pl.pallas_call(
        functools.partial(_resid_ln_kernel, alpha=alpha, gate_idx=gate_idx, shift_idx=shift_idx,
                          scale_idx=scale_idx, with_u=with_u, with_router=with_router, combine=combine),
        out_shape=out_shape,
        grid=(rows_out.n_blocks,),
        in_specs=in_specs,
        out_specs=out_specs,
        scratch_shapes=scratch,
        compiler_params=_cparams("arbitrary" if with_router else "parallel"),
        name="resid_ln",
    )(*args)


def _gla_kernel(*refs, reverse, fuse_post, q_scale, dk, dv):
    if fuse_post:
        q_ref, k_ref, v_ref, r_ref, w2_ref, gb_ref, other_ref, g_ref, ng_ref, out_ref, state_ref = refs
    else:
        q_ref, k_ref, v_ref, r_ref, w2_ref, gb_ref, out_ref, state_ref = refs

    @pl.when(pl.program_id(1) == 0)
    def _():
        state_ref[...] = jnp.zeros_like(state_ref)

    t = q_ref.shape[0]
    c = GLA_CHUNK
    shift = c.bit_length() - 1
    r_hi, r_lo = _split_bf16(r_ref[...])
    w_hi, w_lo = _split_bf16(w2_ref[...])
    z = _dot(r_hi, w_hi) + _dot(r_lo, w_hi) + _dot(r_hi, w_lo) + gb_ref[...]
    log_a = (jnp.minimum(z, 0.0) - jnp.log(1.0 + jnp.exp(-jnp.abs(z)))) * (1.0 / GLA_GATE_NORM)
    row = lax.broadcasted_iota(jnp.int32, (t, t), 0)
    col = lax.broadcasted_iota(jnp.int32, (t, t), 1)
    same_chunk = lax.shift_right_logical(row, shift) == lax.shift_right_logical(col, shift)
    ahead = (col - row) if reverse else (row - col)
    tri = jnp.where(jnp.where(same_chunk, ahead, -1) >= 0, 1.0, 0.0).astype(BF16)
    a_hi, a_lo = _split_bf16(log_a)
    cum = _dot(tri, a_hi) + _dot(tri, a_lo)
    grow = jnp.exp(cum)
    shrink = jnp.exp(-cum)
    crow = lax.broadcasted_iota(jnp.int32, (c, c), 0)
    ccol = lax.broadcasted_iota(jnp.int32, (c, c), 1)
    causal = (ccol >= crow) if reverse else (crow >= ccol)
    n_chunks = t // c
    heads = range(GLA_HEADS)
    ks = [slice(h * dk, (h + 1) * dk) for h in heads]
    vs = [slice(h * dv, (h + 1) * dv) for h in heads]
    q_dec = [(q_ref[:, ks[h]].astype(F32) * q_scale * grow[:, ks[h]]).astype(BF16) for h in heads]
    k_in = [k_ref[:, ks[h]].astype(F32) * shrink[:, ks[h]] for h in heads]
    for step in range(n_chunks):
        j = n_chunks - 1 - step if reverse else step
        rs = slice(j * c, (j + 1) * c)
        edge = j * c if reverse else (j + 1) * c - 1
        decay = [jnp.exp(cum[edge:edge + 1, ks[h]]) for h in heads]
        v = [v_ref[rs, vs[h]] for h in heads]
        scores = [jnp.where(causal, _dot_nt(q_dec[h][rs], k_in[h][rs].astype(BF16)), 0.0).astype(BF16)
                  for h in heads]
        state_t = [state_ref[h] for h in heads]
        o = [_dot(scores[h], v[h]) + _dot_nt(q_dec[h][rs], state_t[h].astype(BF16)) for h in heads]
        for h in heads:
            k_state = (k_in[h][rs] * decay[h]).astype(BF16)
            kv_t = lax.dot_general(v[h], k_state, (((0,), (0,)), ((), ())), preferred_element_type=F32)
            state_ref[h] = state_t[h] * decay[h] + kv_t
        for h in heads:
            if fuse_post:
                oh = o[h] + other_ref[rs, vs[h]]
                oh = oh * lax.rsqrt(jnp.mean(oh * oh, axis=-1, keepdims=True) + LN_EPS) * ng_ref[...]
                g = g_ref[rs, vs[h]].astype(F32)
                out_ref[rs, vs[h]] = (oh * (g * _sigmoid(g))).astype(out_ref.dtype)
            else:
                out_ref[rs, vs[h]] = o[h]


def _gla_scan(qkvg, r, w2, gate_b, rows, direction, other=None, norm_g=None):
    t_rows = qkvg.shape[0]
    dk_tot = w2.shape[2]
    dk = dk_tot // GLA_HEADS
    dv_tot = (qkvg.shape[1] - 2 * dk_tot) // 2
    dv = dv_tot // GLA_HEADS
    t = rows.tile
    reverse = direction == 1
    fuse_post = other is not None

    def block(b, n):
        if reverse:
            n = jnp.where(n < rows.nb_ctx, rows.nb_ctx - 1 - n, rows.nb_seq + rows.nb_ctx - 1 - n)
        return b * rows.nb_seq + n

    in_specs = [
        pl.BlockSpec((t, dk_tot), lambda b, n: (block(b, n), 0)),
        pl.BlockSpec((t, dk_tot), lambda b, n: (block(b, n), 1)),
        pl.BlockSpec((t, dv_tot), lambda b, n: (block(b, n), (2 * dk_tot) // dv_tot)),
        pl.BlockSpec((t, LANES), lambda b, n: (block(b, n), direction)),
        pl.BlockSpec((None, LANES, dk_tot), lambda b, n: (direction, 0, 0)),
        pl.BlockSpec((None, 1, dk_tot), lambda b, n: (direction, 0, 0)),
    ]
    args = [qkvg, qkvg, qkvg, r, w2, gate_b]
    if fuse_post:
        in_specs += [
            pl.BlockSpec((t, dv_tot), lambda b, n: (block(b, n), 0)),
            pl.BlockSpec((t, dv_tot), lambda b, n: (block(b, n), (2 * dk_tot + dv_tot) // dv_tot)),
            pl.BlockSpec((1, dv), lambda b, n: (0, 0)),
        ]
        args += [other, qkvg, norm_g.reshape(1, dv).astype(F32)]
    return pl.pallas_call(
        functools.partial(_gla_kernel, reverse=reverse, fuse_post=fuse_post, q_scale=float(dk) ** -0.5, dk=dk, dv=dv),
        out_shape=jax.ShapeDtypeStruct((t_rows, dv_tot), BF16 if fuse_post else F32),
        grid=(rows.batch, rows.nb_seq),
        in_specs=in_specs,
        out_specs=pl.BlockSpec((t, dv_tot), lambda b, n: (block(b, n), 0)),
        scratch_shapes=[pltpu.VMEM((GLA_HEADS, dv, dk), F32)],
        compiler_params=_cparams("parallel", "arbitrary"),
        name="gla_scan_bwd" if reverse else "gla_scan_fwd",
    )(*args)


def _gla_mixer(u, rows, w_in, gate_w1, gate_w2, gate_b, norm_g, w_out):
    d = u.shape[1]
    dk_tot = gate_w2.shape[2]
    qkvg = _matmul(u, w_in.astype(BF16), name="gla_in")
    w1 = jnp.zeros((d, 2 * LANES), F32)
    w2 = jnp.zeros((2, LANES, dk_tot), F32)
    for dd in range(2):
        w1 = w1.at[:, dd * LANES:dd * LANES + GLA_GATE_RANK].set(gate_w1[dd])
        w2 = w2.at[dd, :GLA_GATE_RANK].set(gate_w2[dd])
    r = _matmul(u, w1.astype(BF16), out_dtype=F32, name="gla_gate_in")
    gate_b = gate_b.reshape(2, 1, dk_tot).astype(F32)
    o_fwd = _gla_scan(qkvg, r, w2, gate_b, rows, 0)
    y = _gla_scan(qkvg, r, w2, gate_b, rows, 1, other=o_fwd, norm_g=norm_g)
    return _matmul(y, w_out.astype(BF16), name="gla_out")


def _rope_kernel(x_ref, cos_ref, sin_ref, q_ref, k2_ref, v2_ref, *, q_width, kv_width):
    cos = cos_ref[...]
    sin = sin_ref[...]
    lane = lax.broadcasted_iota(jnp.int32, cos.shape, 1)
    first = (lane % 32) < 16
    low = lane < SWA_HEAD_DIM

    def rope(slab):
        partner = jnp.where(first, pltpu.roll(slab, LANES - 16, 1), pltpu.roll(slab, 16, 1))
        return slab * cos + partner * sin

    scale = SWA_HEAD_DIM ** -0.5
    for s in range(q_width // LANES):
        slab = x_ref[:, s * LANES:(s + 1) * LANES].astype(F32)
        q_ref[:, s * LANES:(s + 1) * LANES] = (rope(slab) * scale).astype(q_ref.dtype)
    for s in range(kv_width // LANES):
        k_slab = rope(x_ref[:, q_width + s * LANES:q_width + (s + 1) * LANES].astype(F32))
        v_slab = x_ref[:, q_width + kv_width + s * LANES:q_width + kv_width + (s + 1) * LANES].astype(F32)
        for slab, out_ref in ((k_slab, k2_ref), (v_slab, v2_ref)):
            swapped = pltpu.roll(slab, SWA_HEAD_DIM, 1)
            out_ref[2 * s] = jnp.where(low, slab, swapped).astype(out_ref.dtype)
            out_ref[2 * s + 1] = jnp.where(low, swapped, slab).astype(out_ref.dtype)


def _rope_tables(n_ctx, seq):
    t = jnp.arange(seq, dtype=jnp.int32)
    row = (t // GRID_W).astype(F32)
    col = (t % GRID_W).astype(F32)
    n_freq = SWA_HEAD_DIM // 4
    inv_freq = ROPE_BASE ** (-jnp.arange(n_freq, dtype=F32) / n_freq)
    lane = jnp.arange(LANES)
    within = lane % SWA_HEAD_DIM
    pos = jnp.where((within < SWA_HEAD_DIM // 2)[None, :], row[:, None], col[:, None])
    ang = pos * inv_freq[lane % n_freq][None, :]
    sign = jnp.where((lane % 32) < 16, -1.0, 1.0)[None, :]
    cos = jnp.concatenate([jnp.ones((n_ctx, LANES), F32), jnp.cos(ang)], axis=0)
    sin = jnp.concatenate([jnp.zeros((n_ctx, LANES), F32), jnp.sin(ang) * sign], axis=0)
    return cos, sin


def _rope_split(qkv, cos, sin, rows, q_width, kv_width):
    t_rows = qkv.shape[0]
    t = rows.tile
    n_kv = kv_width // SWA_HEAD_DIM
    seq_rows = rows.nb_seq * t
    kv_shape = jax.ShapeDtypeStruct((rows.batch, n_kv, seq_rows, LANES), BF16)
    kv_spec = pl.BlockSpec((None, n_kv, t, LANES), lambda i: (i // rows.nb_seq, 0, i % rows.nb_seq, 0))
    return pl.pallas_call(
        functools.partial(_rope_kernel, q_width=q_width, kv_width=kv_width),
        out_shape=[jax.ShapeDtypeStruct((t_rows, q_width), BF16), kv_shape, kv_shape],
        grid=(rows.n_blocks,),
        in_specs=[
            pl.BlockSpec((t, qkv.shape[1]), lambda i: (i, 0)),
            pl.BlockSpec((t, LANES), lambda i: (i % rows.nb_seq, 0)),
            pl.BlockSpec((t, LANES), lambda i: (i % rows.nb_seq, 0)),
        ],
        out_specs=[pl.BlockSpec((t, q_width), lambda i: (i, 0)), kv_spec, kv_spec],
        compiler_params=_cparams("parallel"),
        name="rope_split",
    )(qkv, cos, sin)


def _attend_heads(q_ref, kk, vv, mask, sink_ref, head0, o_ref):
    tq = q_ref.shape[0]
    lane = lax.broadcasted_iota(jnp.int32, (tq, LANES), 1)
    low = lane < SWA_HEAD_DIM
    q_all = q_ref[...]
    sinks = [sink_ref[head0 + h] for h in range(SWA_GROUP)]
    scores = []
    for h in range(SWA_GROUP):
        q2 = q_all[:, (h // 2) * LANES:(h // 2 + 1) * LANES]
        qh = jnp.where(low if h % 2 == 0 else lane >= SWA_HEAD_DIM, q2, jnp.zeros_like(q2))
        s = _dot_nt(qh, kk)
        scores.append(s if mask is None else jnp.where(mask, s, NEG_INF))
    probs, denoms = [], []
    for h in range(SWA_GROUP):
        m = jnp.maximum(jnp.max(scores[h], axis=-1, keepdims=True), sinks[h])
        p = jnp.exp(scores[h] - m)
        denoms.append(jnp.sum(p, axis=-1, keepdims=True) + jnp.exp(sinks[h] - m))
        probs.append(p.astype(BF16))
    outs = [_dot(probs[h], vv) / denoms[h] for h in range(SWA_GROUP)]
    o_ref[...] = jnp.concatenate(
        [jnp.where(low, outs[2 * pair], outs[2 * pair + 1]).astype(o_ref.dtype) for pair in range(SWA_GROUP // 2)],
        axis=1)


def _attn_kernel(sink_ref, q_ref, kc_ref, vc_ref, kp_ref, vp_ref, km_ref, vm_ref, kn_ref, vn_ref,
                 o_ref, kk_ref, vv_ref, *, n_ctx, nq_ctx, nq_lat):
    n = pl.program_id(2)
    head0 = pl.program_id(1) * SWA_GROUP
    w = WINDOW

    @pl.when(n < nq_ctx)
    def _():
        _attend_heads(q_ref, kc_ref[...], vc_ref[...], None, sink_ref, head0, o_ref)

    @pl.when(n >= nq_ctx)
    def _():
        m = n - nq_ctx
        for ref, parts in ((kk_ref, (kc_ref, kp_ref, km_ref, kn_ref)), (vv_ref, (vc_ref, vp_ref, vm_ref, vn_ref))):
            ref[0:n_ctx] = parts[0][...]
            ref[n_ctx:n_ctx + w] = parts[1][...]
            ref[n_ctx + w:n_ctx + 2 * w] = parts[2][...]
            ref[n_ctx + 2 * w:n_ctx + 3 * w] = parts[3][...]
        nk = n_ctx + 3 * w
        qpos = lax.broadcasted_iota(jnp.int32, (w, nk), 0)
        col = lax.broadcasted_iota(jnp.int32, (w, nk), 1)
        rel = jnp.where(col < n_ctx, 0, col - (n_ctx + w) - qpos)
        far = 4 * w
        no_prev = jnp.where(m > 0, 0, far)
        no_next = jnp.where(m < nq_lat - 1, 0, far)
        missing = jnp.where(col < n_ctx, 0,
                            jnp.where(col < n_ctx + w, no_prev, jnp.where(col >= n_ctx + 2 * w, no_next, 0)))
        mask = jnp.abs(rel) + missing <= w
        _attend_heads(q_ref, kk_ref[...], vv_ref[...], mask, sink_ref, head0, o_ref)


def _window_attention(q, k2, v2, sink, rows, n_ctx):
    t_rows, q_width = q.shape
    batch, n_kv, seq_rows, _ = k2.shape
    w = WINDOW
    gw = SWA_GROUP * SWA_HEAD_DIM
    nq_seq = seq_rows // w
    nq_ctx = n_ctx // w
    nq_lat = nq_seq - nq_ctx
    ctx_kv = pl.BlockSpec((None, None, n_ctx, LANES), lambda b, g, n: (b, g, 0, 0))

    def band(shift):
        def index(b, g, n):
            return (b, g, nq_ctx + jnp.clip(n - nq_ctx + shift, 0, nq_lat - 1), 0)
        return pl.BlockSpec((None, None, w, LANES), index)

    nk = n_ctx + 3 * w
    return pl.pallas_call(
        functools.partial(_attn_kernel, n_ctx=n_ctx, nq_ctx=nq_ctx, nq_lat=nq_lat),
        out_shape=jax.ShapeDtypeStruct((t_rows, q_width), BF16),
        grid=(batch, n_kv, nq_seq),
        in_specs=[pl.BlockSpec(memory_space=pltpu.SMEM),
                  pl.BlockSpec((w, gw), lambda b, g, n: (b * nq_seq + n, g)),
                  ctx_kv, ctx_kv, band(-1), band(-1), band(0), band(0), band(1), band(1)],
        out_specs=pl.BlockSpec((w, gw), lambda b, g, n: (b * nq_seq + n, g)),
        scratch_shapes=[pltpu.VMEM((nk, LANES), BF16), pltpu.VMEM((nk, LANES), BF16)],
        compiler_params=_cparams("parallel", "parallel", "parallel"),
        name="window_attn",
    )(sink, q, k2, v2, k2, v2, k2, v2, k2, v2)


def _window_mixer(u, rows, n_ctx, seq, w_qkv, b_qkv, sink, w_out, b_out):
    q_width = w_out.shape[0]
    kv_width = (w_qkv.shape[1] - q_width) // 2
    qkv = _matmul(u, w_qkv.astype(BF16), bias=b_qkv, name="swa_qkv")
    cos, sin = _rope_tables(n_ctx, seq)
    q, k2, v2 = _rope_split(qkv, cos, sin, rows, q_width, kv_width)
    o = _window_attention(q, k2, v2, sink.astype(F32), rows, n_ctx)
    return _matmul(o, w_out.astype(BF16), bias=b_out, name="swa_out")


def _conv_kernel(gi_ref, go_ref, val_ref, w_ref, o_ref, p_ref, *, n_ctx):
    length = gi_ref.shape[0]
    pad = 8
    p_ref[0:pad] = jnp.zeros((pad, p_ref.shape[1]), F32)
    p_ref[pad + length:pad + length + pad] = jnp.zeros((pad, p_ref.shape[1]), F32)
    p_ref[pad:pad + length] = gi_ref[...].astype(F32) * val_ref[...].astype(F32)
    t = lax.broadcasted_iota(jnp.int32, (length, 1), 0)
    seg_start = t * (t - n_ctx) == 0
    seg_end = (t - (n_ctx - 1)) * (t - (length - 1)) == 0
    prev = jnp.where(seg_start, 0.0, p_ref[pad - 1:pad - 1 + length])
    nxt = jnp.where(seg_end, 0.0, p_ref[pad + 1:pad + 1 + length])
    z = prev * w_ref[0:1, :] + p_ref[pad:pad + length] * w_ref[1:2, :] + nxt * w_ref[2:3, :]
    o_ref[...] = (go_ref[...].astype(F32) * z).astype(o_ref.dtype)


def _conv_gate(proj, conv_w, rows, n_ctx):
    t_rows = proj.shape[0]
    d = proj.shape[1] // 3
    length = rows.nb_seq * rows.tile
    tc = LANES
    nc = d // tc
    return pl.pallas_call(
        functools.partial(_conv_kernel, n_ctx=n_ctx),
        out_shape=jax.ShapeDtypeStruct((t_rows, d), BF16),
        grid=(rows.batch, nc),
        in_specs=[
            pl.BlockSpec((length, tc), lambda b, c: (b, c)),
            pl.BlockSpec((length, tc), lambda b, c: (b, nc + c)),
            pl.BlockSpec((length, tc), lambda b, c: (b, 2 * nc + c)),
            pl.BlockSpec((CONV_WIDTH, tc), lambda b, c: (0, c)),
        ],
        out_specs=pl.BlockSpec((length, tc), lambda b, c: (b, c)),
        scratch_shapes=[pltpu.VMEM((length + 16, tc), F32)],
        compiler_params=_cparams("parallel", "parallel"),
        name="conv_gate",
    )(proj, proj, proj, conv_w.astype(F32))


def _conv_mixer(u, rows, n_ctx, w_in, w_conv, w_out):
    proj = _matmul(u, w_in.astype(BF16), name="conv_in")
    zg = _conv_gate(proj, w_conv, rows, n_ctx)
    return _matmul(zg, w_out.astype(BF16), name="conv_out")


def _row_gather_copy(u_hbm, x_buf, sem, slot, tok, r):
    return pltpu.make_async_copy(u_hbm.at[pl.ds(tok, 1)], x_buf.at[slot, pl.ds(r, 1)], sem.at[slot])


MOE_TOKEN_BITS = 15


def _row_scatter_copy(o_buf, y_hbm, sem, par, r, dst):
    return pltpu.make_async_copy(o_buf.at[par, pl.ds(r, 1)], y_hbm.at[pl.ds(dst, 1)], sem.at[par])


def _moe_kernel(blk_exp_ref, n_used_ref, rows_ref, u_hbm, wgu_ref, bgu_ref, wdn_ref, bdn_ref, y_hbm,
                x_buf, gsem, o_buf, osem, wgu_bf_ref, wdn_bf_ref):
    i = pl.program_id(0)
    tm = x_buf.shape[1]
    n_used = n_used_ref[0]
    used = i < n_used
    slot = lax.rem(i, 3)
    prev = lax.rem(i + 2, 3)
    tok_mask = (1 << MOE_TOKEN_BITS) - 1

    def wait_gather():
        pltpu.make_async_copy(u_hbm.at[pl.ds(0, tm)], x_buf.at[slot], gsem.at[slot]).wait()

    @pl.when(i == 0)
    def _():
        o_buf[...] = jnp.zeros_like(o_buf)
        for half in range(2):
            fill = pltpu.make_async_copy(o_buf.at[half], y_hbm.at[pl.ds(y_hbm.shape[0] - (2 - half) * tm, tm)],
                                         osem.at[half])
            fill.start()
            fill.wait()

    @pl.when(used & (i == 0))
    def _():
        def body(r, carry):
            _row_gather_copy(u_hbm, x_buf, gsem, 0, rows_ref[tm + r] & tok_mask, r).start()
            _row_gather_copy(u_hbm, x_buf, gsem, 1, rows_ref[2 * tm + r] & tok_mask, r).start(priority=1)
            return carry
        lax.fori_loop(0, tm, body, 0, unroll=4)

    @pl.when((i >= 2) & (i <= n_used + 2) & (n_used > 0))
    def _():
        pltpu.make_async_copy(o_buf.at[slot], y_hbm.at[pl.ds(0, tm)], osem.at[slot]).wait()

    new_expert = (i == 0) | (blk_exp_ref[i] != blk_exp_ref[jnp.maximum(i - 1, 0)])

    @pl.when(used & new_expert)
    def _():
        wgu_bf_ref[...] = wgu_ref[...].astype(BF16)
        wdn_bf_ref[...] = wdn_ref[...].astype(BF16)

    @pl.when(used)
    def _():
        wait_gather()
        x = x_buf[slot].astype(BF16)
        ahead = lax.rem(i + 2, 3)
        for r in range(tm):
            tok = rows_ref[(i + 3) * tm + r] & tok_mask
            _row_gather_copy(u_hbm, x_buf, gsem, ahead, tok, r).start(priority=r % 2)
        for r in range(tm):
            dst = lax.shift_right_logical(rows_ref[i * tm + r], MOE_TOKEN_BITS)
            _row_scatter_copy(o_buf, y_hbm, osem, prev, r, dst).start(priority=(r + 1) % 2)
        ff = wdn_ref.shape[0]
        gu = _dot(x, wgu_bf_ref[...]) + bgu_ref[...]
        glu = jnp.minimum(gu[:, :ff], SWIGLU_LIMIT)
        lin = jnp.clip(gu[:, ff:], -SWIGLU_LIMIT, SWIGLU_LIMIT)
        act = glu * _sigmoid(SWIGLU_ALPHA * glu) * (lin + 1.0)
        o_buf[slot] = _dot(act.astype(BF16), wdn_bf_ref[...]) + bdn_ref[...]

    @pl.when((i >= n_used) & (i <= n_used + 1) & (n_used > 0))
    def _():
        wait_gather()

    @pl.when((i == n_used) & (i >= 1))
    def _():
        def body(r, carry):
            dst = lax.shift_right_logical(rows_ref[i * tm + r], MOE_TOKEN_BITS)
            _row_scatter_copy(o_buf, y_hbm, osem, prev, r, dst).start()
            return carry
        lax.fori_loop(0, tm, body, 0, unroll=8)


def _expert_mlp(u, rows, blk_exp, n_used, n_out, layer, w_gate_up, b_gate_up, w_down, b_down):
    d = u.shape[1]
    n_rows = rows.shape[0] - MOE_ROW_BLOCK
    depth, n_exp, _, ff2 = w_gate_up.shape
    ff = ff2 // 2
    tm = MOE_ROW_BLOCK
    grid_spec = pltpu.PrefetchScalarGridSpec(
        num_scalar_prefetch=3,
        grid=(n_rows // tm,),
        in_specs=[
            pl.BlockSpec(memory_space=pl.ANY),
            pl.BlockSpec((None, None, d, ff2), lambda i, be, nu, rt: (layer, be[i], 0, 0)),
            pl.BlockSpec((None, None, 1, ff2), lambda i, be, nu, rt: (layer, be[i], 0, 0)),
            pl.BlockSpec((None, None, ff, d), lambda i, be, nu, rt: (layer, be[i], 0, 0)),
            pl.BlockSpec((None, None, 1, d), lambda i, be, nu, rt: (layer, be[i], 0, 0)),
        ],
        out_specs=pl.BlockSpec(memory_space=pl.ANY),
        scratch_shapes=[pltpu.VMEM((3, tm, d), F32), pltpu.SemaphoreType.DMA((3,)),
                        pltpu.VMEM((3, tm, d), F32), pltpu.SemaphoreType.DMA((3,)),
                        pltpu.VMEM((d, ff2), BF16), pltpu.VMEM((ff, d), BF16)],
    )
    return pl.pallas_call(
        _moe_kernel,
        out_shape=jax.ShapeDtypeStruct((n_out, d), F32),
        grid_spec=grid_spec,
        compiler_params=pltpu.CompilerParams(dimension_semantics=("arbitrary",), vmem_limit_bytes=MOE_VMEM_LIMIT),
        name="expert_mlp",
    )(blk_exp, n_used, rows, u, w_gate_up, b_gate_up.reshape(depth, n_exp, 1, ff2),
      w_down, b_down.reshape(depth, n_exp, 1, d))


def _moe(u, route_idx, counts, layer, w_gate_up, b_gate_up, w_down, b_down):
    n_tok, d = u.shape
    assert n_tok < (1 << MOE_TOKEN_BITS)
    tm = MOE_ROW_BLOCK
    n_assign = n_tok * TOP_K
    n_blocks = n_assign // tm + N_EXPERTS + 2
    n_rows = n_blocks * tm
    top_exp = route_idx[:, :TOP_K]
    counts = counts[0, :N_EXPERTS].astype(jnp.int32)
    padded = (counts + tm - 1) // tm * tm
    start = jnp.cumsum(counts) - counts
    pad_end = jnp.cumsum(padded)
    pad_start = pad_end - padded
    blk_start = jnp.arange(n_blocks, dtype=jnp.int32) * tm
    blk_exp = jnp.minimum(jnp.sum(pad_end[None, :] <= blk_start[:, None], axis=1), N_EXPERTS - 1).astype(jnp.int32)
    n_used = (pad_end[-1:] // tm).astype(jnp.int32)
    order = jnp.argsort(top_exp.reshape(-1), stable=True).astype(jnp.int32)
    row_exp = jnp.repeat(blk_exp, tm)
    within = jnp.arange(n_rows, dtype=jnp.int32) - pad_start[row_exp]
    src = jnp.clip(start[row_exp] + within, 0, n_assign - 1)
    assign = order[src]
    valid = within < counts[row_exp]
    r = jnp.arange(n_rows, dtype=jnp.int32)
    spare = n_assign + ((r // tm + 1) % 2) * tm + r % tm
    row_tok = jnp.where(valid, assign // TOP_K, 0)
    row_dst = jnp.where(valid, (assign % TOP_K) * n_tok + assign // TOP_K, spare)
    rows = row_tok | (row_dst << MOE_TOKEN_BITS)
    rows = jnp.concatenate([(n_assign + r[:tm]) << MOE_TOKEN_BITS, rows])
    return _expert_mlp(u, rows, blk_exp, n_used, n_assign + 2 * tm, layer, w_gate_up, b_gate_up, w_down, b_down)


def kernel(x, c, ctx, c_ctx, ada_w, ada_b, ln_g, ln_b, gla_w_in, gla_gate_w1, gla_gate_w2, gla_gate_b, gla_norm_g, gla_w_out, swa_w_qkv, swa_b_qkv, swa_sink, swa_w_out, swa_b_out, conv_w_in, conv_w, conv_w_out, moe_router_w, moe_router_b, moe_w_gate_up, moe_b_gate_up, moe_w_down, moe_b_down):
    batch, seq, d = x.shape
    n_ctx = ctx.shape[1]
    depth = ada_w.shape[0]
    alpha = (2 * depth) ** 0.25
    rows_full = _Rows(batch, n_ctx // ROW_TILE, seq // ROW_TILE)
    rows_lat = _Rows(batch, 0, seq // ROW_TILE)

    cond = jnp.zeros((8, d), F32).at[0].set(c_ctx).at[1:1 + batch].set(c)
    ada = _ada_mods(cond, ada_w, ada_b)
    mods_ctx = jnp.broadcast_to(ada[:, 0:1], (depth, batch, N_MOD * d))
    mods = jnp.stack([mods_ctx, ada[:, 1:1 + batch]], axis=2).reshape(depth, batch, 2, N_MOD, d)

    router_w = jnp.zeros((depth, d, LANES), F32).at[:, :, :N_EXPERTS].set(moe_router_w)
    router_b = jnp.zeros((depth, 1, LANES), F32).at[:, 0, :N_EXPERTS].set(moe_router_b)

    h = jnp.concatenate([ctx, x], axis=1).reshape(rows_full.n_rows, d)
    u = _modulate(h, mods[0], rows_full, 0, 1)
    rows = rows_full
    for i in range(depth):
        last = i == depth - 1
        kind, j = i % 3, i // 3
        if kind == 0:
            y = _gla_mixer(u, rows, gla_w_in[j], gla_gate_w1[j], gla_gate_w2[j], gla_gate_b[j], gla_norm_g[j], gla_w_out[j])
        elif kind == 1:
            y = _window_mixer(u, rows, n_ctx, seq, swa_w_qkv[j], swa_b_qkv[j], swa_sink[j], swa_w_out[j], swa_b_out[j])
        else:
            y = _conv_mixer(u, rows, n_ctx, conv_w_in[j], conv_w[j], conv_w_out[j])
        rows_out = rows_lat if last else rows
        h, u, route_idx, route_gate, counts = _resid_ln(
            h, y, mods[i], mods[i], ln_g[i, 0], ln_b[i, 0], rows, rows_out, alpha=alpha, gate_idx=2,
            next_mod=(3, 4), router=(router_w[i], router_b[i]))
        rows = rows_out
        y = _moe(u, route_idx, counts, i, moe_w_gate_up, moe_b_gate_up, moe_w_down, moe_b_down)
        if last:
            (h,) = _resid_ln(h, y, mods[i], mods[i], ln_g[i, 1], ln_b[i, 1], rows, rows, alpha=alpha, gate_idx=5,
                             combine_gates=route_gate)
        else:
            h, u = _resid_ln(h, y, mods[i], mods[i + 1], ln_g[i, 1], ln_b[i, 1], rows, rows, alpha=alpha, gate_idx=5,
                             next_mod=(0, 1), combine_gates=route_gate)
    return h.reshape(batch, seq, d)
```

```python
import functools

import jax
import jax.numpy as jnp
from jax import lax
from jax.experimental import pallas as pl
from jax.experimental.pallas import tpu as pltpu

F32 = jnp.float32
BF16 = jnp.bfloat16
HIGHEST = lax.Precision.HIGHEST

GRID_W = 64
GLA_HEADS = 4
GLA_GATE_RANK = 16
GLA_GATE_NORM = 16.0
GLA_CHUNK = 64
SWA_HEAD_DIM = 64
SWA_GROUP = 8
WINDOW = 128
ROPE_BASE = 10000.0
CONV_WIDTH = 3
N_EXPERTS = 32
TOP_K = 4
SWIGLU_LIMIT = 7.0
SWIGLU_ALPHA = 1.702
N_MOD = 6
LN_EPS = 1e-5
NEG_INF = -1e30

LANES = 128
ROW_TILE = 256
MOE_ROW_BLOCK = 256
VMEM_LIMIT = 48 * 1024 * 1024
MOE_VMEM_LIMIT = 60 * 1024 * 1024


def _cparams(*sem):
    return pltpu.CompilerParams(dimension_semantics=sem, vmem_limit_bytes=VMEM_LIMIT)


def _sigmoid(x):
    return 1.0 / (1.0 + jnp.exp(-x))


def _dot(a, b):
    return jnp.dot(a, b, preferred_element_type=F32)


def _dot_nt(a, b):
    return lax.dot_general(a, b, (((1,), (1,)), ((), ())), preferred_element_type=F32)


def _split_bf16(x):
    hi = x.astype(BF16)
    lo = (x - hi.astype(F32)).astype(BF16)
    return hi, lo


class _Rows:
    def __init__(self, batch, nb_ctx, nb_lat, tile=ROW_TILE):
        self.batch, self.nb_ctx, self.nb_lat, self.tile = batch, nb_ctx, nb_lat, tile
        self.nb_seq = nb_ctx + nb_lat
        self.n_blocks = batch * self.nb_seq
        self.n_rows = self.n_blocks * tile

    def batch_of(self, i):
        return i // self.nb_seq

    def seg_of(self, i):
        if self.nb_ctx == 0:
            return 1
        return jnp.where(i % self.nb_seq >= self.nb_ctx, 1, 0)

    def latent_block(self, i):
        return (i // self.nb_lat) * self.nb_seq + self.nb_ctx + i % self.nb_lat


def _ada_kernel(c_ref, w_ref, b_ref, o_ref):
    c = c_ref[...]
    s = c * _sigmoid(c)
    o_ref[...] = jnp.dot(s, w_ref[...], precision=HIGHEST, preferred_element_type=F32) + b_ref[...]


def _ada_mods(cond, ada_w, ada_b):
    depth, d, n = ada_w.shape
    tn = 1024 if n % 1024 == 0 else n
    return pl.pallas_call(
        _ada_kernel,
        out_shape=jax.ShapeDtypeStruct((depth, 8, n), F32),
        grid=(depth, n // tn),
        in_specs=[
            pl.BlockSpec((8, d), lambda l, j: (0, 0)),
            pl.BlockSpec((None, d, tn), lambda l, j: (l, 0, j)),
            pl.BlockSpec((None, 1, tn), lambda l, j: (l, 0, j)),
        ],
        out_specs=pl.BlockSpec((None, 8, tn), lambda l, j: (l, 0, j)),
        compiler_params=_cparams("parallel", "parallel"),
        name="ada_mods",
    )(cond, ada_w, ada_b.reshape(depth, 1, n))


def _modulate_kernel(h_ref, m_ref, u_ref, *, shift_idx, scale_idx):
    h = h_ref[...]
    u = h * (1.0 + m_ref[scale_idx:scale_idx + 1, :]) + m_ref[shift_idx:shift_idx + 1, :]
    u_ref[...] = u.astype(u_ref.dtype)


def _modulate(h, mods, rows, shift_idx, scale_idx):
    d = h.shape[1]
    t = rows.tile
    return pl.pallas_call(
        functools.partial(_modulate_kernel, shift_idx=shift_idx, scale_idx=scale_idx),
        out_shape=jax.ShapeDtypeStruct(h.shape, BF16),
        grid=(rows.n_blocks,),
        in_specs=[
            pl.BlockSpec((t, d), lambda i: (i, 0)),
            pl.BlockSpec((None, None, N_MOD, d), lambda i: (rows.batch_of(i), rows.seg_of(i), 0, 0)),
        ],
        out_specs=pl.BlockSpec((t, d), lambda i: (i, 0)),
        compiler_params=_cparams("parallel"),
        name="modulate",
    )(h, mods)


def _mm_kernel(x_ref, w_ref, o_ref):
    o_ref[...] = _dot(x_ref[...], w_ref[...]).astype(o_ref.dtype)


def _mm_bias_kernel(x_ref, w_ref, b_ref, o_ref):
    o_ref[...] = (_dot(x_ref[...], w_ref[...]) + b_ref[...]).astype(o_ref.dtype)


def _pick_tile(n, candidates):
    for c in candidates:
        if n % c == 0:
            return c
    return n


def _matmul(x, w, bias=None, out_dtype=BF16, name="proj"):
    m, k = x.shape
    n = w.shape[1]
    tm = _pick_tile(m, (1024, 512, 256))
    tn = _pick_tile(n, (1024, 768, 640, 512, 256, 128))
    in_specs = [
        pl.BlockSpec((tm, k), lambda i, j: (i, 0)),
        pl.BlockSpec((k, tn), lambda i, j: (0, j)),
    ]
    args = [x, w]
    body = _mm_kernel
    if bias is not None:
        in_specs.append(pl.BlockSpec((1, tn), lambda i, j: (0, j)))
        args.append(bias.reshape(1, n).astype(F32))
        body = _mm_bias_kernel
    return pl.pallas_call(
        body,
        out_shape=jax.ShapeDtypeStruct((m, n), out_dtype),
        grid=(m // tm, n // tn),
        in_specs=in_specs,
        out_specs=pl.BlockSpec((tm, tn), lambda i, j: (i, j)),
        compiler_params=_cparams("parallel", "parallel"),
        name=name,
    )(*args)


def _route_block(logits, count_ref):
    t = logits.shape[0]
    lane = lax.broadcasted_iota(jnp.int32, (t, LANES), 1)
    lane_f = lane.astype(F32)
    left = jnp.where(lane < N_EXPERTS, logits, NEG_INF)
    hot = jnp.zeros((t, LANES), F32)
    sels, vals = [], []
    for _ in range(TOP_K):
        m = jnp.max(left, axis=-1, keepdims=True)
        sel = jnp.min(jnp.where(left == m, lane_f, float(LANES)), axis=-1, keepdims=True)
        pick = lane_f == sel
        left = jnp.where(pick, NEG_INF, left)
        hot = jnp.where(pick, 1.0, hot)
        sels.append(sel)
        vals.append(m)
    count_ref[...] += jnp.sum(hot, axis=0, keepdims=True)
    exps = [jnp.exp(v - vals[0]) for v in vals]
    denom = exps[0]
    for e in exps[1:]:
        denom = denom + e
    idx = jnp.zeros((t, LANES), jnp.int32)
    gates = jnp.zeros((t, LANES), F32)
    for k in range(TOP_K):
        idx = jnp.where(lane == k, sels[k].astype(jnp.int32), idx)
        gates = jnp.where(lane == k, exps[k] / denom, gates)
    return idx, gates


def _resid_ln_kernel(*refs, alpha, gate_idx, shift_idx, scale_idx, with_u, with_router, combine):
    h_ref = refs[0]
    if combine:
        y_refs = refs[1:1 + TOP_K]
        yg_ref = refs[1 + TOP_K]
        pos = 2 + TOP_K
    else:
        y_ref = refs[1]
        pos = 2
    m_ref, mn_ref, g_ref, b_ref = refs[pos:pos + 4]
    pos += 4
    if with_router:
        wr_hi_ref, wr_lo_ref, br_ref = refs[pos:pos + 3]
        pos += 3
    hn_ref = refs[pos]
    pos += 1
    if combine:
        gates = yg_ref[...]
        y = gates[:, 0:1] * y_refs[0][...]
        for k in range(1, TOP_K):
            y = y + gates[:, k:k + 1] * y_refs[k][...]
    else:
        y = y_ref[...].astype(F32)
    z = alpha * h_ref[...] + y * m_ref[gate_idx:gate_idx + 1, :]
    zc = z - jnp.mean(z, axis=-1, keepdims=True)
    var = jnp.mean(zc * zc, axis=-1, keepdims=True)
    hn = zc * lax.rsqrt(var + LN_EPS) * g_ref[...] + b_ref[...]
    hn_ref[...] = hn
    if with_u:
        u_ref = refs[pos]
        pos += 1
        u = hn * (1.0 + mn_ref[scale_idx:scale_idx + 1, :]) + mn_ref[shift_idx:shift_idx + 1, :]
        u_ref[...] = u.astype(u_ref.dtype)
        if with_router:
            idx_ref, gate_ref, cnt_ref, count_ref = refs[pos:pos + 4]

            @pl.when(pl.program_id(0) == 0)
            def _():
                count_ref[...] = jnp.zeros_like(count_ref)

            u_hi, u_lo = _split_bf16(u)
            w_hi = wr_hi_ref[...]
            logits = _dot(u_hi, w_hi) + _dot(u_lo, w_hi) + _dot(u_hi, wr_lo_ref[...]) + br_ref[...]
            idx, gates = _route_block(logits, count_ref)
            idx_ref[...] = idx
            gate_ref[...] = gates
            cnt_ref[...] = count_ref[...]


def _resid_ln(h, y, mods, mods_next, ln_g, ln_b, rows_in, rows_out, *, alpha, gate_idx, next_mod=None, router=None,
              combine_gates=None):
    d = h.shape[1]
    t = rows_out.tile
    if rows_in is rows_out:
        in_row = lambda i: i
    else:
        in_row = rows_in.latent_block
    with_u = next_mod is not None
    with_router = router is not None
    combine = combine_gates is not None
    shift_idx, scale_idx = next_mod if with_u else (0, 0)
    mod_spec = pl.BlockSpec((None, None, N_MOD, d), lambda i: (rows_out.batch_of(i), rows_out.seg_of(i), 0, 0))
    in_specs = [pl.BlockSpec((t, d), lambda i: (in_row(i), 0))]
    args = [h]
    if combine:
        for k in range(TOP_K):
            in_specs.append(pl.BlockSpec((t, d), lambda i, k=k: (k * rows_in.n_blocks + in_row(i), 0)))
        in_specs.append(pl.BlockSpec((t, LANES), lambda i: (in_row(i), 0)))
        args += [y] * TOP_K + [combine_gates]
    else:
        in_specs.append(pl.BlockSpec((t, d), lambda i: (in_row(i), 0)))
        args.append(y)
    in_specs += [mod_spec, mod_spec, pl.BlockSpec((1, d), lambda i: (0, 0)), pl.BlockSpec((1, d), lambda i: (0, 0))]
    args += [mods, mods_next, ln_g.reshape(1, d), ln_b.reshape(1, d)]
    out_shape = [jax.ShapeDtypeStruct((rows_out.n_rows, d), F32)]
    out_specs = [pl.BlockSpec((t, d), lambda i: (i, 0))]
    scratch = []
    if with_router:
        wr, br = router
        w_spec = pl.BlockSpec((d, LANES), lambda i: (0, 0))
        in_specs += [w_spec, w_spec, pl.BlockSpec((1, LANES), lambda i: (0, 0))]
        args += [*_split_bf16(wr), br]
    if with_u:
        out_shape.append(jax.ShapeDtypeStruct((rows_out.n_rows, d), F32 if with_router else BF16))
        out_specs.append(pl.BlockSpec((t, d), lambda i: (i, 0)))
    if with_router:
        out_shape += [jax.ShapeDtypeStruct((rows_out.n_rows, LANES), jnp.int32),
                      jax.ShapeDtypeStruct((rows_out.n_rows, LANES), F32),
                      jax.ShapeDtypeStruct((1, LANES), F32)]
        out_specs += [pl.BlockSpec((t, LANES), lambda i: (i, 0)), pl.BlockSpec((t, LANES), lambda i: (i, 0)),
                      pl.BlockSpec((1, LANES), lambda i: (0, 0))]
        scratch.append(pltpu.VMEM((1, LANES), F32))
    return pl.pallas_call(
        functools.partial(_resid_ln_kernel, alpha=alpha, gate_idx=gate_idx, shift_idx=shift_idx,
                          scale_idx=scale_idx, with_u=with_u, with_router=with_router, combine=combine),
        out_shape=out_shape,
        grid=(rows_out.n_blocks,),
        in_specs=in_specs,
        out_specs=out_specs,
        scratch_shapes=scratch,
        compiler_params=_cparams("arbitrary" if with_router else "parallel"),
        name="resid_ln",
    )(*args)


def _gla_kernel(*refs, reverse, fuse_post, q_scale, dk, dv):
    if fuse_post:
        q_ref, k_ref, v_ref, r_ref, w2_ref, gb_ref, other_ref, g_ref, ng_ref, out_ref, state_ref = refs
    else:
        q_ref, k_ref, v_ref, r_ref, w2_ref, gb_ref, out_ref, state_ref = refs

    @pl.when(pl.program_id(1) == 0)
    def _():
        state_ref[...] = jnp.zeros_like(state_ref)

    t = q_ref.shape[0]
    c = GLA_CHUNK
    shift = c.bit_length() - 1
    r_hi, r_lo = _split_bf16(r_ref[...])
    w_hi, w_lo = _split_bf16(w2_ref[...])
    z = _dot(r_hi, w_hi) + _dot(r_lo, w_hi) + _dot(r_hi, w_lo) + gb_ref[...]
    log_a = (jnp.minimum(z, 0.0) - jnp.log(1.0 + jnp.exp(-jnp.abs(z)))) * (1.0 / GLA_GATE_NORM)
    row = lax.broadcasted_iota(jnp.int32, (t, t), 0)
    col = lax.broadcasted_iota(jnp.int32, (t, t), 1)
    same_chunk = lax.shift_right_logical(row, shift) == lax.shift_right_logical(col, shift)
    ahead = (col - row) if reverse else (row - col)
    tri = jnp.where(jnp.where(same_chunk, ahead, -1) >= 0, 1.0, 0.0).astype(BF16)
    a_hi, a_lo = _split_bf16(log_a)
    cum = _dot(tri, a_hi) + _dot(tri, a_lo)
    grow = jnp.exp(cum)
    shrink = jnp.exp(-cum)
    crow = lax.broadcasted_iota(jnp.int32, (c, c), 0)
    ccol = lax.broadcasted_iota(jnp.int32, (c, c), 1)
    causal = (ccol >= crow) if reverse else (crow >= ccol)
    n_chunks = t // c
    heads = range(GLA_HEADS)
    ks = [slice(h * dk, (h + 1) * dk) for h in heads]
    vs = [slice(h * dv, (h + 1) * dv) for h in heads]
    q_dec = [(q_ref[:, ks[h]].astype(F32) * q_scale * grow[:, ks[h]]).astype(BF16) for h in heads]
    k_in = [k_ref[:, ks[h]].astype(F32) * shrink[:, ks[h]] for h in heads]
    for step in range(n_chunks):
        j = n_chunks - 1 - step if reverse else step
        rs = slice(j * c, (j + 1) * c)
        edge = j * c if reverse else (j + 1) * c - 1
        decay = [jnp.exp(cum[edge:edge + 1, ks[h]]) for h in heads]
        v = [v_ref[rs, vs[h]] for h in heads]
        scores = [jnp.where(causal, _dot_nt(q_dec[h][rs], k_in[h][rs].astype(BF16)), 0.0).astype(BF16)
                  for h in heads]
        state_t = [state_ref[h] for h in heads]
        o = [_dot(scores[h], v[h]) + _dot_nt(q_dec[h][rs], state_t[h].astype(BF16)) for h in heads]
        for h in heads:
            k_state = (k_in[h][rs] * decay[h]).astype(BF16)
            kv_t = lax.dot_general(v[h], k_state, (((0,), (0,)), ((), ())), preferred_element_type=F32)
            state_ref[h] = state_t[h] * decay[h] + kv_t
        for h in heads:
            if fuse_post:
                oh = o[h] + other_ref[rs, vs[h]]
                oh = oh * lax.rsqrt(jnp.mean(oh * oh, axis=-1, keepdims=True) + LN_EPS) * ng_ref[...]
                g = g_ref[rs, vs[h]].astype(F32)
                out_ref[rs, vs[h]] = (oh * (g * _sigmoid(g))).astype(out_ref.dtype)
            else:
                out_ref[rs, vs[h]] = o[h]


def _gla_scan(qkvg, r, w2, gate_b, rows, direction, other=None, norm_g=None):
    t_rows = qkvg.shape[0]
    dk_tot = w2.shape[2]
    dk = dk_tot // GLA_HEADS
    dv_tot = (qkvg.shape[1] - 2 * dk_tot) // 2
    dv = dv_tot // GLA_HEADS
    t = rows.tile
    reverse = direction == 1
    fuse_post = other is not None

    def block(b, n):
        if reverse:
            n = jnp.where(n < rows.nb_ctx, rows.nb_ctx - 1 - n, rows.nb_seq + rows.nb_ctx - 1 - n)
        return b * rows.nb_seq + n

    in_specs = [
        pl.BlockSpec((t, dk_tot), lambda b, n: (block(b, n), 0)),
        pl.BlockSpec((t, dk_tot), lambda b, n: (block(b, n), 1)),
        pl.BlockSpec((t, dv_tot), lambda b, n: (block(b, n), (2 * dk_tot) // dv_tot)),
        pl.BlockSpec((t, LANES), lambda b, n: (block(b, n), direction)),
        pl.BlockSpec((None, LANES, dk_tot), lambda b, n: (direction, 0, 0)),
        pl.BlockSpec((None, 1, dk_tot), lambda b, n: (direction, 0, 0)),
    ]
    args = [qkvg, qkvg, qkvg, r, w2, gate_b]
    if fuse_post:
        in_specs += [
            pl.BlockSpec((t, dv_tot), lambda b, n: (block(b, n), 0)),
            pl.BlockSpec((t, dv_tot), lambda b, n: (block(b, n), (2 * dk_tot + dv_tot) // dv_tot)),
            pl.BlockSpec((1, dv), lambda b, n: (0, 0)),
        ]
        args += [other, qkvg, norm_g.reshape(1, dv).astype(F32)]
    return pl.pallas_call(
        functools.partial(_gla_kernel, reverse=reverse, fuse_post=fuse_post, q_scale=float(dk) ** -0.5, dk=dk, dv=dv),
        out_shape=jax.ShapeDtypeStruct((t_rows, dv_tot), BF16 if fuse_post else F32),
        grid=(rows.batch, rows.nb_seq),
        in_specs=in_specs,
        out_specs=pl.BlockSpec((t, dv_tot), lambda b, n: (block(b, n), 0)),
        scratch_shapes=[pltpu.VMEM((GLA_HEADS, dv, dk), F32)],
        compiler_params=_cparams("parallel", "arbitrary"),
        name="gla_scan_bwd" if reverse else "gla_scan_fwd",
    )(*args)


def _gla_mixer(u, rows, w_in, gate_w1, gate_w2, gate_b, norm_g, w_out):
    d = u.shape[1]
    dk_tot = gate_w2.shape[2]
    qkvg = _matmul(u, w_in.astype(BF16), name="gla_in")
    w1 = jnp.zeros((d, 2 * LANES), F32)
    w2 = jnp.zeros((2, LANES, dk_tot), F32)
    for dd in range(2):
        w1 = w1.at[:, dd * LANES:dd * LANES + GLA_GATE_RANK].set(gate_w1[dd])
        w2 = w2.at[dd, :GLA_GATE_RANK].set(gate_w2[dd])
    r = _matmul(u, w1.astype(BF16), out_dtype=F32, name="gla_gate_in")
    gate_b = gate_b.reshape(2, 1, dk_tot).astype(F32)
    o_fwd = _gla_scan(qkvg, r, w2, gate_b, rows, 0)
    y = _gla_scan(qkvg, r, w2, gate_b, rows, 1, other=o_fwd, norm_g=norm_g)
    return _matmul(y, w_out.astype(BF16), name="gla_out")


def _rope_kernel(x_ref, cos_ref, sin_ref, q_ref, k2_ref, v2_ref, *, q_width, kv_width):
    cos = cos_ref[...]
    sin = sin_ref[...]
    lane = lax.broadcasted_iota(jnp.int32, cos.shape, 1)
    first = (lane % 32) < 16
    low = lane < SWA_HEAD_DIM

    def rope(slab):
        partner = jnp.where(first, pltpu.roll(slab, LANES - 16, 1), pltpu.roll(slab, 16, 1))
        return slab * cos + partner * sin

    scale = SWA_HEAD_DIM ** -0.5
    for s in range(q_width // LANES):
        slab = x_ref[:, s * LANES:(s + 1) * LANES].astype(F32)
        q_ref[:, s * LANES:(s + 1) * LANES] = (rope(slab) * scale).astype(q_ref.dtype)
    for s in range(kv_width // LANES):
        k_slab = rope(x_ref[:, q_width + s * LANES:q_width + (s + 1) * LANES].astype(F32))
        v_slab = x_ref[:, q_width + kv_width + s * LANES:q_width + kv_width + (s + 1) * LANES].astype(F32)
        for slab, out_ref in ((k_slab, k2_ref), (v_slab, v2_ref)):
            swapped = pltpu.roll(slab, SWA_HEAD_DIM, 1)
            out_ref[2 * s] = jnp.where(low, slab, swapped).astype(out_ref.dtype)
            out_ref[2 * s + 1] = jnp.where(low, swapped, slab).astype(out_ref.dtype)


def _rope_tables(n_ctx, seq):
    t = jnp.arange(seq, dtype=jnp.int32)
    row = (t // GRID_W).astype(F32)
    col = (t % GRID_W).astype(F32)
    n_freq = SWA_HEAD_DIM // 4
    inv_freq = ROPE_BASE ** (-jnp.arange(n_freq, dtype=F32) / n_freq)
    lane = jnp.arange(LANES)
    within = lane % SWA_HEAD_DIM
    pos = jnp.where((within < SWA_HEAD_DIM // 2)[None, :], row[:, None], col[:, None])
    ang = pos * inv_freq[lane % n_freq][None, :]
    sign = jnp.where((lane % 32) < 16, -1.0, 1.0)[None, :]
    cos = jnp.concatenate([jnp.ones((n_ctx, LANES), F32), jnp.cos(ang)], axis=0)
    sin = jnp.concatenate([jnp.zeros((n_ctx, LANES), F32), jnp.sin(ang) * sign], axis=0)
    return cos, sin


def _rope_split(qkv, cos, sin, rows, q_width, kv_width):
    t_rows = qkv.shape[0]
    t = rows.tile
    n_kv = kv_width // SWA_HEAD_DIM
    seq_rows = rows.nb_seq * t
    kv_shape = jax.ShapeDtypeStruct((rows.batch, n_kv, seq_rows, LANES), BF16)
    kv_spec = pl.BlockSpec((None, n_kv, t, LANES), lambda i: (i // rows.nb_seq, 0, i % rows.nb_seq, 0))
    return pl.pallas_call(
        functools.partial(_rope_kernel, q_width=q_width, kv_width=kv_width),
        out_shape=[jax.ShapeDtypeStruct((t_rows, q_width), BF16), kv_shape, kv_shape],
        grid=(rows.n_blocks,),
        in_specs=[
            pl.BlockSpec((t, qkv.shape[1]), lambda i: (i, 0)),
            pl.BlockSpec((t, LANES), lambda i: (i % rows.nb_seq, 0)),
            pl.BlockSpec((t, LANES), lambda i: (i % rows.nb_seq, 0)),
        ],
        out_specs=[pl.BlockSpec((t, q_width), lambda i: (i, 0)), kv_spec, kv_spec],
        compiler_params=_cparams("parallel"),
        name="rope_split",
    )(qkv, cos, sin)


def _attend_heads(q_ref, kk, vv, mask, sink_ref, head0, o_ref):
    tq = q_ref.shape[0]
    lane = lax.broadcasted_iota(jnp.int32, (tq, LANES), 1)
    low = lane < SWA_HEAD_DIM
    q_all = q_ref[...]
    sinks = [sink_ref[head0 + h] for h in range(SWA_GROUP)]
    scores = []
    for h in range(SWA_GROUP):
        q2 = q_all[:, (h // 2) * LANES:(h // 2 + 1) * LANES]
        qh = jnp.where(low if h % 2 == 0 else lane >= SWA_HEAD_DIM, q2, jnp.zeros_like(q2))
        s = _dot_nt(qh, kk)
        scores.append(s if mask is None else jnp.where(mask, s, NEG_INF))
    probs, denoms = [], []
    for h in range(SWA_GROUP):
        m = jnp.maximum(jnp.max(scores[h], axis=-1, keepdims=True), sinks[h])
        p = jnp.exp(scores[h] - m)
        denoms.append(jnp.sum(p, axis=-1, keepdims=True) + jnp.exp(sinks[h] - m))
        probs.append(p.astype(BF16))
    outs = [_dot(probs[h], vv) / denoms[h] for h in range(SWA_GROUP)]
    o_ref[...] = jnp.concatenate(
        [jnp.where(low, outs[2 * pair], outs[2 * pair + 1]).astype(o_ref.dtype) for pair in range(SWA_GROUP // 2)],
        axis=1)


def _attn_kernel(sink_ref, q_ref, kc_ref, vc_ref, kp_ref, vp_ref, km_ref, vm_ref, kn_ref, vn_ref,
                 o_ref, kk_ref, vv_ref, *, n_ctx, nq_ctx, nq_lat):
    n = pl.program_id(2)
    head0 = pl.program_id(1) * SWA_GROUP
    w = WINDOW

    @pl.when(n < nq_ctx)
    def _():
        _attend_heads(q_ref, kc_ref[...], vc_ref[...], None, sink_ref, head0, o_ref)

    @pl.when(n >= nq_ctx)
    def _():
        m = n - nq_ctx
        for ref, parts in ((kk_ref, (kc_ref, kp_ref, km_ref, kn_ref)), (vv_ref, (vc_ref, vp_ref, vm_ref, vn_ref))):
            ref[0:n_ctx] = parts[0][...]
            ref[n_ctx:n_ctx + w] = parts[1][...]
            ref[n_ctx + w:n_ctx + 2 * w] = parts[2][...]
            ref[n_ctx + 2 * w:n_ctx + 3 * w] = parts[3][...]
        nk = n_ctx + 3 * w
        qpos = lax.broadcasted_iota(jnp.int32, (w, nk), 0)
        col = lax.broadcasted_iota(jnp.int32, (w, nk), 1)
        rel = jnp.where(col < n_ctx, 0, col - (n_ctx + w) - qpos)
        far = 4 * w
        no_prev = jnp.where(m > 0, 0, far)
        no_next = jnp.where(m < nq_lat - 1, 0, far)
        missing = jnp.where(col < n_ctx, 0,
                            jnp.where(col < n_ctx + w, no_prev, jnp.where(col >= n_ctx + 2 * w, no_next, 0)))
        mask = jnp.abs(rel) + missing <= w
        _attend_heads(q_ref, kk_ref[...], vv_ref[...], mask, sink_ref, head0, o_ref)


def _window_attention(q, k2, v2, sink, rows, n_ctx):
    t_rows, q_width = q.shape
    batch, n_kv, seq_rows, _ = k2.shape
    w = WINDOW
    gw = SWA_GROUP * SWA_HEAD_DIM
    nq_seq = seq_rows // w
    nq_ctx = n_ctx // w
    nq_lat = nq_seq - nq_ctx
    ctx_kv = pl.BlockSpec((None, None, n_ctx, LANES), lambda b, g, n: (b, g, 0, 0))

    def band(shift):
        def index(b, g, n):
            return (b, g, nq_ctx + jnp.clip(n - nq_ctx + shift, 0, nq_lat - 1), 0)
        return pl.BlockSpec((None, None, w, LANES), index)

    nk = n_ctx + 3 * w
    return pl.pallas_call(
        functools.partial(_attn_kernel, n_ctx=n_ctx, nq_ctx=nq_ctx, nq_lat=nq_lat),
        out_shape=jax.ShapeDtypeStruct((t_rows, q_width), BF16),
        grid=(batch, n_kv, nq_seq),
        in_specs=[pl.BlockSpec(memory_space=pltpu.SMEM),
                  pl.BlockSpec((w, gw), lambda b, g, n: (b * nq_seq + n, g)),
                  ctx_kv, ctx_kv, band(-1), band(-1), band(0), band(0), band(1), band(1)],
        out_specs=pl.BlockSpec((w, gw), lambda b, g, n: (b * nq_seq + n, g)),
        scratch_shapes=[pltpu.VMEM((nk, LANES), BF16), pltpu.VMEM((nk, LANES), BF16)],
        compiler_params=_cparams("parallel", "parallel", "parallel"),
        name="window_attn",
    )(sink, q, k2, v2, k2, v2, k2, v2, k2, v2)


def _window_mixer(u, rows, n_ctx, seq, w_qkv, b_qkv, sink, w_out, b_out):
    q_width = w_out.shape[0]
    kv_width = (w_qkv.shape[1] - q_width) // 2
    qkv = _matmul(u, w_qkv.astype(BF16), bias=b_qkv, name="swa_qkv")
    cos, sin = _rope_tables(n_ctx, seq)
    q, k2, v2 = _rope_split(qkv, cos, sin, rows, q_width, kv_width)
    o = _window_attention(q, k2, v2, sink.astype(F32), rows, n_ctx)
    return _matmul(o, w_out.astype(BF16), bias=b_out, name="swa_out")


def _conv_kernel(gi_ref, go_ref, val_ref, w_ref, o_ref, p_ref, *, n_ctx):
    length = gi_ref.shape[0]
    pad = 8
    p_ref[0:pad] = jnp.zeros((pad, p_ref.shape[1]), F32)
    p_ref[pad + length:pad + length + pad] = jnp.zeros((pad, p_ref.shape[1]), F32)
    p_ref[pad:pad + length] = gi_ref[...].astype(F32) * val_ref[...].astype(F32)
    t = lax.broadcasted_iota(jnp.int32, (length, 1), 0)
    seg_start = t * (t - n_ctx) == 0
    seg_end = (t - (n_ctx - 1)) * (t - (length - 1)) == 0
    prev = jnp.where(seg_start, 0.0, p_ref[pad - 1:pad - 1 + length])
    nxt = jnp.where(seg_end, 0.0, p_ref[pad + 1:pad + 1 + length])
    z = prev * w_ref[0:1, :] + p_ref[pad:pad + length] * w_ref[1:2, :] + nxt * w_ref[2:3, :]
    o_ref[...] = (go_ref[...].astype(F32) * z).astype(o_ref.dtype)


def _conv_gate(proj, conv_w, rows, n_ctx):
    t_rows = proj.shape[0]
    d = proj.shape[1] // 3
    length = rows.nb_seq * rows.tile
    tc = LANES
    nc = d // tc
    return pl.pallas_call(
        functools.partial(_conv_kernel, n_ctx=n_ctx),
        out_shape=jax.ShapeDtypeStruct((t_rows, d), BF16),
        grid=(rows.batch, nc),
        in_specs=[
            pl.BlockSpec((length, tc), lambda b, c: (b, c)),
            pl.BlockSpec((length, tc), lambda b, c: (b, nc + c)),
            pl.BlockSpec((length, tc), lambda b, c: (b, 2 * nc + c)),
            pl.BlockSpec((CONV_WIDTH, tc), lambda b, c: (0, c)),
        ],
        out_specs=pl.BlockSpec((length, tc), lambda b, c: (b, c)),
        scratch_shapes=[pltpu.VMEM((length + 16, tc), F32)],
        compiler_params=_cparams("parallel", "parallel"),
        name="conv_gate",
    )(proj, proj, proj, conv_w.astype(F32))


def _conv_mixer(u, rows, n_ctx, w_in, w_conv, w_out):
    proj = _matmul(u, w_in.astype(BF16), name="conv_in")
    zg = _conv_gate(proj, w_conv, rows, n_ctx)
    return _matmul(zg, w_out.astype(BF16), name="conv_out")


def _row_gather_copy(u_hbm, x_buf, sem, slot, tok, r):
    return pltpu.make_async_copy(u_hbm.at[pl.ds(tok, 1)], x_buf.at[slot, pl.ds(r, 1)], sem.at[slot])


MOE_TOKEN_BITS = 15


def _row_scatter_copy(o_buf, y_hbm, sem, par, r, dst):
    return pltpu.make_async_copy(o_buf.at[par, pl.ds(r, 1)], y_hbm.at[pl.ds(dst, 1)], sem.at[par])


def _moe_kernel(blk_exp_ref, n_used_ref, rows_ref, u_hbm, wgu_ref, bgu_ref, wdn_ref, bdn_ref, y_hbm,
                x_buf, gsem, o_buf, osem, wgu_bf_ref, wdn_bf_ref):
    i = pl.program_id(0)
    tm = x_buf.shape[1]
    n_used = n_used_ref[0]
    used = i < n_used
    slot = lax.rem(i, 3)
    prev = lax.rem(i + 2, 3)
    tok_mask = (1 << MOE_TOKEN_BITS) - 1

    def wait_gather():
        pltpu.make_async_copy(u_hbm.at[pl.ds(0, tm)], x_buf.at[slot], gsem.at[slot]).wait()

    @pl.when(i == 0)
    def _():
        o_buf[...] = jnp.zeros_like(o_buf)
        for half in range(2):
            fill = pltpu.make_async_copy(o_buf.at[half], y_hbm.at[pl.ds(y_hbm.shape[0] - (2 - half) * tm, tm)],
                                         osem.at[half])
            fill.start()
            fill.wait()

    @pl.when(used & (i == 0))
    def _():
        def body(r, carry):
            _row_gather_copy(u_hbm, x_buf, gsem, 0, rows_ref[tm + r] & tok_mask, r).start()
            _row_gather_copy(u_hbm, x_buf, gsem, 1, rows_ref[2 * tm + r] & tok_mask, r).start(priority=1)
            return carry
        lax.fori_loop(0, tm, body, 0, unroll=4)

    @pl.when((i >= 2) & (i <= n_used + 2) & (n_used > 0))
    def _():
        pltpu.make_async_copy(o_buf.at[slot], y_hbm.at[pl.ds(0, tm)], osem.at[slot]).wait()

    new_expert = (i == 0) | (blk_exp_ref[i] != blk_exp_ref[jnp.maximum(i - 1, 0)])

    @pl.when(used & new_expert)
    def _():
        wgu_bf_ref[...] = wgu_ref[...].astype(BF16)
        wdn_bf_ref[...] = wdn_ref[...].astype(BF16)

    @pl.when(used)
    def _():
        wait_gather()
        x = x_buf[slot].astype(BF16)
        ahead = lax.rem(i + 2, 3)
        for r in range(tm):
            tok = rows_ref[(i + 3) * tm + r] & tok_mask
            _row_gather_copy(u_hbm, x_buf, gsem, ahead, tok, r).start(priority=r % 2)
        for r in range(tm):
            dst = lax.shift_right_logical(rows_ref[i * tm + r], MOE_TOKEN_BITS)
            _row_scatter_copy(o_buf, y_hbm, osem, prev, r, dst).start(priority=(r + 1) % 2)
        ff = wdn_ref.shape[0]
        gu = _dot(x, wgu_bf_ref[...]) + bgu_ref[...]
        glu = jnp.minimum(gu[:, :ff], SWIGLU_LIMIT)
        lin = jnp.clip(gu[:, ff:], -SWIGLU_LIMIT, SWIGLU_LIMIT)
        act = glu * _sigmoid(SWIGLU_ALPHA * glu) * (lin + 1.0)
        o_buf[slot] = _dot(act.astype(BF16), wdn_bf_ref[...]) + bdn_ref[...]

    @pl.when((i >= n_used) & (i <= n_used + 1) & (n_used > 0))
    def _():
        wait_gather()

    @pl.when((i == n_used) & (i >= 1))
    def _():
        def body(r, carry):
            dst = lax.shift_right_logical(rows_ref[i * tm + r], MOE_TOKEN_BITS)
            _row_scatter_copy(o_buf, y_hbm, osem, prev, r, dst).start()
            return carry
        lax.fori_loop(0, tm, body, 0, unroll=8)


def _expert_mlp(u, rows, blk_exp, n_used, n_out, layer, w_gate_up, b_gate_up, w_down, b_down):
    d = u.shape[1]
    n_rows = rows.shape[0] - MOE_ROW_BLOCK
    depth, n_exp, _, ff2 = w_gate_up.shape
    ff = ff2 // 2
    tm = MOE_ROW_BLOCK
    grid_spec = pltpu.PrefetchScalarGridSpec(
        num_scalar_prefetch=3,
        grid=(n_rows // tm,),
        in_specs=[
            pl.BlockSpec(memory_space=pl.ANY),
            pl.BlockSpec((None, None, d, ff2), lambda i, be, nu, rt: (layer, be[i], 0, 0)),
            pl.BlockSpec((None, None, 1, ff2), lambda i, be, nu, rt: (layer, be[i], 0, 0)),
            pl.BlockSpec((None, None, ff, d), lambda i, be, nu, rt: (layer, be[i], 0, 0)),
            pl.BlockSpec((None, None, 1, d), lambda i, be, nu, rt: (layer, be[i], 0, 0)),
        ],
        out_specs=pl.BlockSpec(memory_space=pl.ANY),
        scratch_shapes=[pltpu.VMEM((3, tm, d), F32), pltpu.SemaphoreType.DMA((3,)),
                        pltpu.VMEM((3, tm, d), F32), pltpu.SemaphoreType.DMA((3,)),
                        pltpu.VMEM((d, ff2), BF16), pltpu.VMEM((ff, d), BF16)],
    )
    return pl.pallas_call(
        _moe_kernel,
        out_shape=jax.ShapeDtypeStruct((n_out, d), F32),
        grid_spec=grid_spec,
        compiler_params=pltpu.CompilerParams(dimension_semantics=("arbitrary",), vmem_limit_bytes=MOE_VMEM_LIMIT),
        name="expert_mlp",
    )(blk_exp, n_used, rows, u, w_gate_up, b_gate_up.reshape(depth, n_exp, 1, ff2),
      w_down, b_down.reshape(depth, n_exp, 1, d))


def _moe(u, route_idx, counts, layer, w_gate_up, b_gate_up, w_down, b_down):
    n_tok, d = u.shape
    assert n_tok < (1 << MOE_TOKEN_BITS)
    tm = MOE_ROW_BLOCK
    n_assign = n_tok * TOP_K
    n_blocks = n_assign // tm + N_EXPERTS + 2
    n_rows = n_blocks * tm
    top_exp = route_idx[:, :TOP_K]
    counts = counts[0, :N_EXPERTS].astype(jnp.int32)
    padded = (counts + tm - 1) // tm * tm
    start = jnp.cumsum(counts) - counts
    pad_end = jnp.cumsum(padded)
    pad_start = pad_end - padded
    blk_start = jnp.arange(n_blocks, dtype=jnp.int32) * tm
    blk_exp = jnp.minimum(jnp.sum(pad_end[None, :] <= blk_start[:, None], axis=1), N_EXPERTS - 1).astype(jnp.int32)
    n_used = (pad_end[-1:] // tm).astype(jnp.int32)
    order = jnp.argsort(top_exp.reshape(-1), stable=True).astype(jnp.int32)
    row_exp = jnp.repeat(blk_exp, tm)
    within = jnp.arange(n_rows, dtype=jnp.int32) - pad_start[row_exp]
    src = jnp.clip(start[row_exp] + within, 0, n_assign - 1)
    assign = order[src]
    valid = within < counts[row_exp]
    r = jnp.arange(n_rows, dtype=jnp.int32)
    spare = n_assign + ((r // tm + 1) % 2) * tm + r % tm
    row_tok = jnp.where(valid, assign // TOP_K, 0)
    row_dst = jnp.where(valid, (assign % TOP_K) * n_tok + assign // TOP_K, spare)
    rows = row_tok | (row_dst << MOE_TOKEN_BITS)
    rows = jnp.concatenate([(n_assign + r[:tm]) << MOE_TOKEN_BITS, rows])
    return _expert_mlp(u, rows, blk_exp, n_used, n_assign + 2 * tm, layer, w_gate_up, b_gate_up, w_down, b_down)


def kernel(x, c, ctx, c_ctx, ada_w, ada_b, ln_g, ln_b, gla_w_in, gla_gate_w1, gla_gate_w2, gla_gate_b, gla_norm_g, gla_w_out, swa_w_qkv, swa_b_qkv, swa_sink, swa_w_out, swa_b_out, conv_w_in, conv_w, conv_w_out, moe_router_w, moe_router_b, moe_w_gate_up, moe_b_gate_up, moe_w_down, moe_b_down):
    batch, seq, d = x.shape
    n_ctx = ctx.shape[1]
    depth = ada_w.shape[0]
    alpha = (2 * depth) ** 0.25
    rows_full = _Rows(batch, n_ctx // ROW_TILE, seq // ROW_TILE)
    rows_lat = _Rows(batch, 0, seq // ROW_TILE)

    cond = jnp.zeros((8, d), F32).at[0].set(c_ctx).at[1:1 + batch].set(c)
    ada = _ada_mods(cond, ada_w, ada_b)
    mods_ctx = jnp.broadcast_to(ada[:, 0:1], (depth, batch, N_MOD * d))
    mods = jnp.stack([mods_ctx, ada[:, 1:1 + batch]], axis=2).reshape(depth, batch, 2, N_MOD, d)

    router_w = jnp.zeros((depth, d, LANES), F32).at[:, :, :N_EXPERTS].set(moe_router_w)
    router_b = jnp.zeros((depth, 1, LANES), F32).at[:, 0, :N_EXPERTS].set(moe_router_b)

    h = jnp.concatenate([ctx, x], axis=1).reshape(rows_full.n_rows, d)
    u = _modulate(h, mods[0], rows_full, 0, 1)
    rows = rows_full
    for i in range(depth):
        last = i == depth - 1
        kind, j = i % 3, i // 3
        if kind == 0:
            y = _gla_mixer(u, rows, gla_w_in[j], gla_gate_w1[j], gla_gate_w2[j], gla_gate_b[j], gla_norm_g[j], gla_w_out[j])
        elif kind == 1:
            y = _window_mixer(u, rows, n_ctx, seq, swa_w_qkv[j], swa_b_qkv[j], swa_sink[j], swa_w_out[j], swa_b_out[j])
        else:
            y = _conv_mixer(u, rows, n_ctx, conv_w_in[j], conv_w[j], conv_w_out[j])
        rows_out = rows_lat if last else rows
        h, u, route_idx, route_gate, counts = _resid_ln(
            h, y, mods[i], mods[i], ln_g[i, 0], ln_b[i, 0], rows, rows_out, alpha=alpha, gate_idx=2,
            next_mod=(3, 4), router=(router_w[i], router_b[i]))
        rows = rows_out
        y = _moe(u, route_idx, counts, i, moe_w_gate_up, moe_b_gate_up, moe_w_down, moe_b_down)
        if last:
            (h,) = _resid_ln(h, y, mods[i], mods[i], ln_g[i, 1], ln_b[i, 1], rows, rows, alpha=alpha, gate_idx=5,
                             combine_gates=route_gate)
        else:
            h, u = _resid_ln(h, y, mods[i], mods[i + 1], ln_g[i, 1], ln_b[i, 1], rows, rows, alpha=alpha, gate_idx=5,
                             next_mod=(0, 1), combine_gates=route_gate)
    return h.reshape(batch, seq, d)
```
